```python
import jax, jax.numpy as jnp
from jax import lax
import numpy as np

D_MODEL = 1024
BATCH = 8
SEQ = 2048
DEPTH = 2

GRID_W = 64
CTX_LEN = 256
N_MIXERS = 2
N_RET_LAYERS = (DEPTH + 1) // 2
N_SWA_LAYERS = DEPTH // 2
NORM_EPS = 1e-6
RET_HEADS = D_MODEL // 256
RET_DK = D_MODEL // RET_HEADS
RET_DV = 2 * RET_DK
RET_CHUNK = 128
SWA_HEAD_DIM = 64
SWA_Q_HEADS = D_MODEL // SWA_HEAD_DIM
SWA_KV_HEADS = 2
SWA_WINDOW = 128
SWA_BLOCK = SWA_WINDOW
ROPE_BASE = 10000.0
N_EXPERTS = 16
EC_CAPACITY = 2
EXPERT_FF = 2 * D_MODEL

kernel_name = "hybrid_retention_swa_ec_moe_dit"


def rmsnorm(x, g):
    xf = x.astype(jnp.float32)
    xf = xf * lax.rsqrt(jnp.mean(xf * xf, axis=-1, keepdims=True) + NORM_EPS)
    return (xf * g.astype(jnp.float32)).astype(x.dtype)


def adaln_modulation(cond, w, b):
    m = jax.nn.silu(cond) @ w + b
    return jnp.split(m[..., None, :], 6, axis=-1)


def modulate(h, shift, scale):
    return h * (1.0 + scale) + shift


def axial_rope_tables(n_tokens):
    rows = n_tokens // GRID_W
    row = jnp.repeat(jnp.arange(rows), GRID_W).astype(jnp.float32)
    col = jnp.tile(jnp.arange(GRID_W), rows).astype(jnp.float32)
    n_freq = SWA_HEAD_DIM // 4
    inv_freq = ROPE_BASE ** (-jnp.arange(n_freq, dtype=jnp.float32) / n_freq)
    ang = jnp.stack([row[:, None] * inv_freq, col[:, None] * inv_freq], axis=1)
    return jnp.cos(ang), jnp.sin(ang)


def apply_axial_rope(x, cos, sin):
    xr = x.reshape(x.shape[:-1] + (2, 2, SWA_HEAD_DIM // 4)).astype(jnp.float32)
    x1, x2 = xr[..., 0, :], xr[..., 1, :]
    c, s = cos[:, None], sin[:, None]
    out = jnp.stack([x1 * c - x2 * s, x2 * c + x1 * s], axis=-2)
    return out.reshape(x.shape).astype(x.dtype)


def retention_chunkwise(q, k, v, log_gamma, state0):
    b, h, t, _ = q.shape
    dv = v.shape[-1]
    n = t // RET_CHUNK

    def chunks(a):
        return jnp.moveaxis(a.astype(jnp.float32).reshape(b, h, n, RET_CHUNK, a.shape[-1]), 2, 0)

    pos = jnp.arange(RET_CHUNK, dtype=jnp.float32)
    diff = pos[:, None] - pos[None, :]
    lg = log_gamma.astype(jnp.float32)[:, None, None]
    intra_decay = jnp.where(diff >= 0, jnp.exp(jnp.maximum(diff, 0.0) * lg), 0.0)
    q_decay = jnp.exp((pos + 1.0) * lg[:, :, 0])[:, :, None]
    k_decay = jnp.exp((RET_CHUNK - 1.0 - pos) * lg[:, :, 0])[:, :, None]
    chunk_decay = jnp.exp(RET_CHUNK * lg)

    def step(state, qkv):
        qc, kc, vc = qkv
        scores = jnp.einsum('bhqd,bhkd->bhqk', qc, kc) * intra_decay
        o = (jnp.einsum('bhqk,bhkv->bhqv', scores, vc)
             + jnp.einsum('bhqd,bhdv->bhqv', qc, state) * q_decay)
        state = chunk_decay * state + jnp.einsum('bhkd,bhkv->bhdv', kc * k_decay, vc)
        return state, o

    state, o = lax.scan(step, state0, (chunks(q), chunks(k), chunks(v)))
    return jnp.moveaxis(o, 0, 2).reshape(b, h, t, dv), state


def retention_final_state(k, v, log_gamma):
    t = k.shape[2]
    w = jnp.exp((t - 1.0 - jnp.arange(t, dtype=jnp.float32))[None, :] * log_gamma[:, None])
    return jnp.einsum('bhtd,bhtv->bhdv', k.astype(jnp.float32) * w[..., None], v.astype(jnp.float32))


def head_groupnorm(o, gain):
    mu = jnp.mean(o, axis=-1, keepdims=True)
    var = jnp.mean(jnp.square(o - mu), axis=-1, keepdims=True)
    o = (o - mu) * lax.rsqrt(var + NORM_EPS)
    b, h, t, dv = o.shape
    return o.transpose(0, 2, 1, 3).reshape(b, t, h * dv) * gain.astype(jnp.float32)


def retention_mixer(h_lat, h_ctx, w_in, decay_f, decay_b, gn_f, gn_b, w_out, need_ctx_out):
    hk, hv = RET_HEADS * RET_DK, RET_HEADS * RET_DV
    k0, v0, gf0, gb0 = hk, 2 * hk, 2 * hk + hv, 2 * hk + 2 * hv
    lg_f = jax.nn.log_sigmoid(decay_f.astype(jnp.float32))
    lg_b = jax.nn.log_sigmoid(decay_b.astype(jnp.float32))

    def heads(a, dh):
        return a.reshape(a.shape[0], a.shape[1], RET_HEADS, dh).transpose(0, 2, 1, 3)

    def flip(a):
        return jnp.flip(a, axis=2)

    def gated_out(o_f, o_b, g_f, g_b):
        y = (jax.nn.silu(g_f.astype(jnp.float32)) * head_groupnorm(o_f, gn_f)
             + jax.nn.silu(g_b.astype(jnp.float32)) * head_groupnorm(o_b, gn_b))
        return y.astype(w_out.dtype) @ w_out

    zero = jnp.zeros((h_ctx.shape[0], RET_HEADS, RET_DK, RET_DV), jnp.float32)
    if need_ctx_out:
        pc = h_ctx @ w_in
        qc = heads(pc[..., :k0], RET_DK) * RET_DK ** -0.5
        kc, vc = heads(pc[..., k0:v0], RET_DK), heads(pc[..., v0:gf0], RET_DV)
        oc_f, sc_f = retention_chunkwise(qc, kc, vc, lg_f, zero)
        oc_b, sc_b = retention_chunkwise(flip(qc), flip(kc), flip(vc), lg_b, zero)
        y_ctx = gated_out(oc_f, flip(oc_b), pc[..., gf0:gb0], pc[..., gb0:])
    else:
        pc = h_ctx @ w_in[:, k0:gf0]
        kc, vc = heads(pc[..., :hk], RET_DK), heads(pc[..., hk:], RET_DV)
        sc_f = retention_final_state(kc, vc, lg_f)
        sc_b = retention_final_state(flip(kc), flip(vc), lg_b)
        y_ctx = None
    p = h_lat @ w_in
    q = heads(p[..., :k0], RET_DK) * RET_DK ** -0.5
    k, v = heads(p[..., k0:v0], RET_DK), heads(p[..., v0:gf0], RET_DV)
    o_f, _ = retention_chunkwise(q, k, v, lg_f, sc_f)
    o_b, _ = retention_chunkwise(flip(q), flip(k), flip(v), lg_b, sc_b)
    y_lat = gated_out(o_f, flip(o_b), p[..., gf0:gb0], p[..., gb0:])
    return y_lat, y_ctx


def softmax_with_sink(scores, sink):
    s = jnp.concatenate([scores, jnp.broadcast_to(sink, scores.shape[:-1] + (1,))], axis=-1)
    return jax.nn.softmax(s, axis=-1)[..., :-1]


def window_attention_mixer(h_lat, h_ctx, w_qkv, sink, w_out, need_ctx_out):
    b, t, _ = h_lat.shape
    n_ctx = h_ctx.shape[1]
    H, KV, hd = SWA_Q_HEADS, SWA_KV_HEADS, SWA_HEAD_DIM
    G = H // KV
    qw, kw = H * hd, KV * hd
    scale = hd ** -0.5
    sink_kg = sink.astype(jnp.float32).reshape(KV, G)

    p = h_lat @ w_qkv
    cos, sin = axial_rope_tables(t)
    q = apply_axial_rope(p[..., :qw].reshape(b, t, H, hd), cos, sin) * scale
    k = apply_axial_rope(p[..., qw:qw + kw].reshape(b, t, KV, hd), cos, sin)
    v = p[..., qw + kw:].reshape(b, t, KV, hd)

    if need_ctx_out:
        pc = h_ctx @ w_qkv
        qc = pc[..., :qw].reshape(b, n_ctx, KV, G, hd) * scale
        kvc = pc[..., qw:]
    else:
        kvc = h_ctx @ w_qkv[:, qw:]
    kc = kvc[..., :kw].reshape(b, n_ctx, KV, hd)
    vc = kvc[..., kw:].reshape(b, n_ctx, KV, hd)

    nb = t // SWA_BLOCK
    qb = q.reshape(b, nb, SWA_BLOCK, KV, G, hd)

    def band(a):
        a = jnp.pad(a.reshape(b, nb, SWA_BLOCK, KV, hd), ((0, 0), (1, 1), (0, 0), (0, 0), (0, 0)))
        return jnp.concatenate([a[:, :-2], a[:, 1:-1], a[:, 2:]], axis=2)

    kb, vb = band(k), band(v)
    blk = jnp.arange(nb)[:, None, None]
    qpos = blk * SWA_BLOCK + jnp.arange(SWA_BLOCK)[None, :, None]
    kpos = (blk - 1) * SWA_BLOCK + jnp.arange(3 * SWA_BLOCK)[None, None, :]
    valid = (jnp.abs(qpos - kpos) <= SWA_WINDOW) & (kpos >= 0) & (kpos < t)

    s_win = jnp.einsum('bnqkgd,bnskd->bnkgqs', qb, kb).astype(jnp.float32)
    s_win = jnp.where(valid[None, :, None, None], s_win, -jnp.inf)
    s_ctx = jnp.einsum('bnqkgd,bckd->bnkgqc', qb, kc).astype(jnp.float32)
    probs = softmax_with_sink(jnp.concatenate([s_ctx, s_win], axis=-1),
                              sink_kg[None, None, :, :, None, None]).astype(v.dtype)
    o = (jnp.einsum('bnkgqc,bckd->bnqkgd', probs[..., :n_ctx], vc)
         + jnp.einsum('bnkgqs,bnskd->bnqkgd', probs[..., n_ctx:], vb))
    y_lat = o.reshape(b, t, qw) @ w_out

    if need_ctx_out:
        s_c = jnp.einsum('bqkgd,bckd->bkgqc', qc, kc).astype(jnp.float32)
        pr_c = softmax_with_sink(s_c, sink_kg[None, :, :, None, None]).astype(vc.dtype)
        y_ctx = jnp.einsum('bkgqc,bckd->bqkgd', pr_c, vc).reshape(b, n_ctx, qw) @ w_out
    else:
        y_ctx = None
    return y_lat, y_ctx


def expert_choice_moe(h, w_router, w_gate, w_up, w_down):
    b, t, d = h.shape
    cap = EC_CAPACITY * t // N_EXPERTS
    affinity = jax.nn.softmax((h @ w_router).astype(jnp.float32), axis=-1)
    gate, idx = lax.top_k(jnp.swapaxes(affinity, 1, 2), cap)
    xs = jax.vmap(lambda hb, ib: hb[ib])(h, idx)
    hid = (jax.nn.silu(jnp.einsum('becd,edf->becf', xs, w_gate))
           * jnp.einsum('becd,edf->becf', xs, w_up))
    ys = jnp.einsum('becf,efd->becd', hid, w_down) * gate[..., None].astype(h.dtype)
    return jax.vmap(lambda yb, ib: jnp.zeros((t, d), yb.dtype).at[ib.reshape(-1)].add(yb.reshape(-1, d)))(ys, idx)


def setup_inputs(seed: int = 0) -> dict:
    key = jax.random.key(seed)
    ks = jax.random.split(key, 24)
    f32 = jnp.float32

    def normal(k, shape, scale):
        return jax.random.normal(k, shape, f32) * scale

    hk, hv = RET_HEADS * RET_DK, RET_HEADS * RET_DV
    ret_cols = 2 * hk + 3 * hv
    swa_cols = (SWA_Q_HEADS + 2 * SWA_KV_HEADS) * SWA_HEAD_DIM
    gamma = 1.0 - 2.0 ** (-5.0 - jnp.arange(RET_HEADS, dtype=f32))
    decay_logit = jnp.log(gamma) - jnp.log1p(-gamma)
    return {
        "x": normal(ks[0], (BATCH, SEQ, D_MODEL), 1.0),
        "c": normal(ks[1], (BATCH, D_MODEL), 1.0),
        "ctx": normal(ks[2], (BATCH, CTX_LEN, D_MODEL), 1.0),
        "c_ctx": normal(ks[3], (D_MODEL,), 1.0),
        "ada_w": normal(ks[4], (DEPTH, D_MODEL, 6 * D_MODEL), D_MODEL ** -0.5),
        "ada_b": normal(ks[5], (DEPTH, 6 * D_MODEL), 0.01),
        "norm1_g": 1.0 + normal(ks[6], (DEPTH, D_MODEL), 0.02),
        "norm2_g": 1.0 + normal(ks[7], (DEPTH, D_MODEL), 0.02),
        "final_g": 1.0 + normal(ks[8], (D_MODEL,), 0.02),
        "ret_w_in": normal(ks[9], (N_RET_LAYERS, D_MODEL, ret_cols), D_MODEL ** -0.5),
        "ret_decay_f": decay_logit + normal(ks[10], (N_RET_LAYERS, RET_HEADS), 0.1),
        "ret_decay_b": decay_logit + normal(ks[11], (N_RET_LAYERS, RET_HEADS), 0.1),
        "ret_gn_f": 1.0 + normal(ks[12], (N_RET_LAYERS, hv), 0.02),
        "ret_gn_b": 1.0 + normal(ks[13], (N_RET_LAYERS, hv), 0.02),
        "ret_w_out": normal(ks[14], (N_RET_LAYERS, hv, D_MODEL), hv ** -0.5),
        "swa_w_qkv": normal(ks[15], (N_SWA_LAYERS, D_MODEL, swa_cols), D_MODEL ** -0.5),
        "swa_sink": normal(ks[16], (N_SWA_LAYERS, SWA_Q_HEADS), 0.5),
        "swa_w_out": normal(ks[17], (N_SWA_LAYERS, SWA_Q_HEADS * SWA_HEAD_DIM, D_MODEL), (SWA_Q_HEADS * SWA_HEAD_DIM) ** -0.5),
        "moe_router": normal(ks[18], (DEPTH, D_MODEL, N_EXPERTS), D_MODEL ** -0.5),
        "moe_w_gate": normal(ks[19], (DEPTH, N_EXPERTS, D_MODEL, EXPERT_FF), D_MODEL ** -0.5),
        "moe_w_up": normal(ks[20], (DEPTH, N_EXPERTS, D_MODEL, EXPERT_FF), D_MODEL ** -0.5),
        "moe_w_down": normal(ks[21], (DEPTH, N_EXPERTS, EXPERT_FF, D_MODEL), EXPERT_FF ** -0.5),
    }


def reference(x, c, ctx, c_ctx, ada_w, ada_b, norm1_g, norm2_g, final_g,
              ret_w_in, ret_decay_f, ret_decay_b, ret_gn_f, ret_gn_b, ret_w_out,
              swa_w_qkv, swa_sink, swa_w_out,
              moe_router, moe_w_gate, moe_w_up, moe_w_down):
    for i in range(DEPTH):
        last = i == DEPTH - 1
        j = i // N_MIXERS
        sh1, sc1, g1, sh2, sc2, g2 = adaln_modulation(c, ada_w[i], ada_b[i])
        csh1, csc1, cg1, csh2, csc2, cg2 = adaln_modulation(c_ctx, ada_w[i], ada_b[i])
        hx = modulate(rmsnorm(x, norm1_g[i]), sh1, sc1)
        hc = modulate(rmsnorm(ctx, norm1_g[i]), csh1, csc1)
        if i % N_MIXERS == 0:
            yx, yc = retention_mixer(hx, hc, ret_w_in[j], ret_decay_f[j], ret_decay_b[j],
                                     ret_gn_f[j], ret_gn_b[j], ret_w_out[j], not last)
        else:
            yx, yc = window_attention_mixer(hx, hc, swa_w_qkv[j], swa_sink[j], swa_w_out[j], not last)
        x = x + g1 * yx
        hx = modulate(rmsnorm(x, norm2_g[i]), sh2, sc2)
        x = x + g2 * expert_choice_moe(hx, moe_router[i], moe_w_gate[i], moe_w_up[i], moe_w_down[i])
        if not last:
            ctx = ctx + cg1 * yc
            hc = modulate(rmsnorm(ctx, norm2_g[i]), csh2, csc2)
            ctx = ctx + cg2 * expert_choice_moe(hc, moe_router[i], moe_w_gate[i], moe_w_up[i], moe_w_down[i])
    return rmsnorm(x, final_g)
```

```python
import functools

import jax
import jax.numpy as jnp
from jax import lax
from jax.experimental import pallas as pl
from jax.experimental.pallas import tpu as pltpu

F32 = jnp.float32
BF16 = jnp.bfloat16

NORM_EPS = 1e-6
GRID_W = 64
RET_HEADS = 4
RET_CHUNK = 128
SWA_HEAD_DIM = 64
SWA_KV_HEADS = 2
SWA_BLOCK = 128
ROPE_BASE = 10000.0
N_EXPERTS = 16
EC_CAPACITY = 2

LANES = 128
MIB = 1024 * 1024
COND_ROWS = 16

NT_DIMS = (((1,), (1,)), ((), ()))
TN_DIMS = (((0,), (0,)), ((), ()))


def _cparams(n_axes, vmem_mib):
    return pltpu.CompilerParams(
        dimension_semantics=("arbitrary",) * n_axes,
        vmem_limit_bytes=vmem_mib * MIB,
    )


def _silu(v):
    return v / (1.0 + jnp.exp(-v))


def _rmsnorm_mod(xv, g, shift, scale):
    h = xv * lax.rsqrt(jnp.mean(xv * xv, axis=-1, keepdims=True) + NORM_EPS) * g
    return h * (1.0 + scale) + shift


def _mod_spec(layer, chunk, d, row_of_batch, n_grid_axes, batch_axis):
    def index_map(*ids):
        return (layer, row_of_batch(ids[batch_axis]), 0, chunk)
    del n_grid_axes
    return pl.BlockSpec((None, None, 1, d), index_map)


def _adaln_body(cond_ref, w_ref, b_ref, o_ref):
    s = _silu(cond_ref[...])
    o_ref[...] = jnp.dot(s.astype(BF16), w_ref[...].astype(BF16),
                         preferred_element_type=F32) + b_ref[...]


def _adaln(cond, ada_w, ada_b):
    layers, d, n = ada_w.shape
    tn = d
    out = pl.pallas_call(
        _adaln_body,
        grid=(layers, n // tn),
        in_specs=[
            pl.BlockSpec((COND_ROWS, d), lambda l, j: (0, 0)),
            pl.BlockSpec((None, d, tn), lambda l, j: (l, 0, j)),
            pl.BlockSpec((None, 1, tn), lambda l, j: (l, 0, j)),
        ],
        out_specs=pl.BlockSpec((None, COND_ROWS, tn), lambda l, j: (l, 0, j)),
        out_shape=jax.ShapeDtypeStruct((layers, COND_ROWS, n), F32),
        compiler_params=_cparams(2, 32),
        name="adaln",
    )(cond, ada_w, ada_b.reshape(layers, 1, n))
    return out.reshape(layers, COND_ROWS, 1, n)


def _nmm_body(x_ref, g_ref, sh_ref, sc_ref, w_ref, *rest, rope_groups):
    if rope_groups:
        cos_ref, sm_ref, sp_ref, o_ref, wbf_ref = rest
    else:
        o_ref, wbf_ref = rest

    @pl.when((pl.program_id(1) == 0) & (pl.program_id(2) == 0))
    def _():
        wbf_ref[...] = w_ref[...].astype(BF16)

    h = _rmsnorm_mod(x_ref[...], g_ref[...], sh_ref[...], sc_ref[...])
    acc = jnp.dot(h.astype(BF16), wbf_ref[...], preferred_element_type=F32)
    if not rope_groups:
        o_ref[...] = acc.astype(o_ref.dtype)
        return
    c, sm, sp = cos_ref[...], sm_ref[...], sp_ref[...]
    n_groups = acc.shape[1] // LANES
    for j in range(n_groups):
        a = acc[:, j * LANES:(j + 1) * LANES]
        if j < rope_groups:
            a = (a * c + pltpu.roll(a, LANES - 16, axis=1) * sm
                 + pltpu.roll(a, 16, axis=1) * sp)
        o_ref[:, j * LANES:(j + 1) * LANES] = a.astype(o_ref.dtype)


def _norm_mod_matmul(x, g, mod, layer, row_of_batch, w, *, tm, tn, rope=None):
    b, t, d = x.shape
    n = w.shape[1]
    tm = min(tm, t)
    grid = (n // tn, b, t // tm)
    in_specs = [
        pl.BlockSpec((None, tm, d), lambda j, bi, ti: (bi, ti, 0)),
        pl.BlockSpec((1, d), lambda j, bi, ti: (0, 0)),
        _mod_spec(layer, 0, d, row_of_batch, 3, 1),
        _mod_spec(layer, 1, d, row_of_batch, 3, 1),
        pl.BlockSpec((d, tn), lambda j, bi, ti: (0, j)),
    ]
    args = [x, g.reshape(1, d), mod, mod, w]
    rope_groups = 0
    if rope is not None:
        cos, sm, sp, rope_groups = rope
        for tab in (cos, sm, sp):
            in_specs.append(pl.BlockSpec((tm, LANES), lambda j, bi, ti: (ti, 0)))
            args.append(tab)
    return pl.pallas_call(
        functools.partial(_nmm_body, rope_groups=rope_groups),
        grid=grid,
        in_specs=in_specs,
        out_specs=pl.BlockSpec((None, tm, tn), lambda j, bi, ti: (bi, ti, j)),
        out_shape=jax.ShapeDtypeStruct((b, t, n), BF16),
        scratch_shapes=[pltpu.VMEM((d, tn), BF16)],
        compiler_params=_cparams(3, 48),
        name="norm_mod_proj",
    )(*args)


def _ret_body(dec_ref, gnf_ref, gnb_ref,
              qc_ref, kc_ref, vc_ref, gfc_ref, gbc_ref,
              q_ref, k_ref, v_ref, gf_ref, gb_ref,
              y_ref, yc_ref,
              statef_ref, stateb_ref, yacc_ref, intra_ref, qd_ref, kd_ref, cd_ref):
    c = RET_CHUNK
    dk = q_ref.shape[1]
    dv = v_ref.shape[1]
    t_ctx = qc_ref.shape[0]
    t_lat = q_ref.shape[0]

    dec = dec_ref[...]
    lg = jnp.minimum(dec, 0.0) - jnp.log1p(jnp.exp(-jnp.abs(dec)))
    row = lax.broadcasted_iota(jnp.int32, (c, c), 0)
    col = lax.broadcasted_iota(jnp.int32, (c, c), 1)
    pos_v = lax.broadcasted_iota(jnp.int32, (c, dv), 0).astype(F32)
    pos_k = lax.broadcasted_iota(jnp.int32, (c, dk), 0).astype(F32)
    for d in range(2):
        lg_c, lg_v, lg_k = lg[d:d + 1, :c], lg[d:d + 1, :], lg[d:d + 1, :dk]
        diff = (row - col) if d == 0 else (col - row)
        intra_ref[d] = jnp.where(diff >= 0, jnp.exp(jnp.maximum(diff, 0).astype(F32) * lg_c), 0.0)
        if d == 0:
            qd_ref[d] = jnp.exp((pos_v + 1.0) * lg_v)
            kd_ref[d] = jnp.exp((c - 1.0 - pos_k) * lg_k)
        else:
            qd_ref[d] = jnp.exp((c - pos_v) * lg_v)
            kd_ref[d] = jnp.exp(pos_k * lg_k)
        cd_ref[d] = jnp.exp(float(c) * lg_v)

    def chunk(d, refs, r0, yoff, out_ref, first_visit):
        qr, kr, vr, gr = refs
        state_ref = statef_ref if d == 0 else stateb_ref
        r0 = pl.multiple_of(r0, c)
        qs = qr[pl.ds(r0, c), :] * jnp.asarray(dk ** -0.5, BF16)
        kc = kr[pl.ds(r0, c), :]
        vc = vr[pl.ds(r0, c), :]
        s = lax.dot_general(qs, kc, NT_DIMS, preferred_element_type=F32) * intra_ref[d]
        st = state_ref[...]
        o = (jnp.dot(s.astype(BF16), vc, preferred_element_type=F32)
             + jnp.dot(qs, st.astype(BF16), preferred_element_type=F32) * qd_ref[d])
        kk = (kc.astype(F32) * kd_ref[d]).astype(BF16)
        state_ref[...] = cd_ref[d] * st + lax.dot_general(kk, vc, TN_DIMS, preferred_element_type=F32)
        mu = jnp.mean(o, axis=-1, keepdims=True)
        oc = o - mu
        var = jnp.mean(oc * oc, axis=-1, keepdims=True)
        gn = gnf_ref[...] if d == 0 else gnb_ref[...]
        y = _silu(gr[pl.ds(r0, c), :].astype(F32)) * (oc * lax.rsqrt(var + NORM_EPS) * gn)
        yoff = pl.multiple_of(yoff, c)
        if first_visit:
            yacc_ref[pl.ds(yoff, c), :] = y
        else:
            out_ref[pl.ds(r0, c), :] = (yacc_ref[pl.ds(yoff, c), :] + y).astype(out_ref.dtype)

    statef_ref[...] = jnp.zeros_like(statef_ref)
    stateb_ref[...] = jnp.zeros_like(stateb_ref)
    segments = (
        ((qc_ref, kc_ref, vc_ref), (gfc_ref, gbc_ref), t_ctx // c, 0, yc_ref),
        ((q_ref, k_ref, v_ref), (gf_ref, gb_ref), t_lat // c, t_ctx, y_ref),
    )
    for qkv, gates, n, ybase, out_ref in segments:
        for first_visit in (True, False):

            def step(i, carry, qkv=qkv, gates=gates, n=n, ybase=ybase, out_ref=out_ref,
                     first_visit=first_visit):
                for d in range(2):
                    ci = i if d == 0 else n - 1 - i
                    chunk(d, qkv + (gates[d],), ci * c, ybase + ci * c, out_ref, first_visit)
                return carry

            lo, hi = (0, n // 2) if first_visit else (n // 2, n)
            lax.fori_loop(lo, hi, step, 0)


def _retention(p, pc, decay_f, decay_b, gn_f, gn_b):
    b, t, cols = p.shape
    t_ctx = pc.shape[1]
    assert t % (2 * RET_CHUNK) == 0 and t_ctx % (2 * RET_CHUNK) == 0
    h = RET_HEADS
    dk = cols // (8 * h)
    dv = 2 * dk
    hv = h * dv
    dec = jnp.broadcast_to(jnp.stack([decay_f, decay_b], axis=1)[:, :, None], (h, 2, dv)).astype(F32)
    k_blk, v_blk, gf_blk, gb_blk = h, (2 * h * dk) // dv, (2 * h * dk + hv) // dv, (2 * h * dk + 2 * hv) // dv

    def specs(rows):
        return [
            pl.BlockSpec((None, rows, dk), lambda bi, hi: (bi, 0, hi)),
            pl.BlockSpec((None, rows, dk), lambda bi, hi: (bi, 0, k_blk + hi)),
            pl.BlockSpec((None, rows, dv), lambda bi, hi: (bi, 0, v_blk + hi)),
            pl.BlockSpec((None, rows, dv), lambda bi, hi: (bi, 0, gf_blk + hi)),
            pl.BlockSpec((None, rows, dv), lambda bi, hi: (bi, 0, gb_blk + hi)),
        ]

    gn_spec = pl.BlockSpec((1, dv), lambda bi, hi: (0, hi))
    return pl.pallas_call(
        _ret_body,
        grid=(b, h),
        in_specs=[pl.BlockSpec((None, 2, dv), lambda bi, hi: (hi, 0, 0)), gn_spec, gn_spec]
        + specs(t_ctx) + specs(t),
        out_specs=[
            pl.BlockSpec((None, t, dv), lambda bi, hi: (bi, 0, hi)),
            pl.BlockSpec((None, t_ctx, dv), lambda bi, hi: (bi, 0, hi)),
        ],
        out_shape=[
            jax.ShapeDtypeStruct((b, t, hv), BF16),
            jax.ShapeDtypeStruct((b, t_ctx, hv), BF16),
        ],
        scratch_shapes=[
            pltpu.VMEM((dk, dv), F32),
            pltpu.VMEM((dk, dv), F32),
            pltpu.VMEM((t_ctx + t, dv), F32),
            pltpu.VMEM((2, RET_CHUNK, RET_CHUNK), F32),
            pltpu.VMEM((2, RET_CHUNK, dv), F32),
            pltpu.VMEM((2, RET_CHUNK, dk), F32),
            pltpu.VMEM((2, 1, dv), F32),
        ],
        compiler_params=_cparams(2, 48),
        name="retention",
    )(dec, gn_f.reshape(1, hv), gn_b.reshape(1, hv), pc, pc, pc, pc, pc, p, p, p, p, p)


def _outproj_body(y_ref, w_ref, x_ref, g1_ref, ng_ref, sh_ref, sc_ref, wr_ref,
                  x1_ref, hx_ref, aff_ref, wbf_ref):
    @pl.when((pl.program_id(0) == 0) & (pl.program_id(1) == 0))
    def _():
        wbf_ref[...] = w_ref[...].astype(BF16)

    x1 = x_ref[...] + g1_ref[...] * jnp.dot(y_ref[...], wbf_ref[...], preferred_element_type=F32)
    x1_ref[...] = x1
    h = _rmsnorm_mod(x1, ng_ref[...], sh_ref[...], sc_ref[...])
    hb = h.astype(BF16)
    hx_ref[...] = hb
    hlo = (h - hb.astype(F32)).astype(BF16)
    wr = wr_ref[...]
    whi = wr.astype(BF16)
    wlo = (wr - whi.astype(F32)).astype(BF16)
    logits = (lax.dot_general(whi, hb, NT_DIMS, preferred_element_type=F32)
              + lax.dot_general(whi, hlo, NT_DIMS, preferred_element_type=F32)
              + lax.dot_general(wlo, hb, NT_DIMS, preferred_element_type=F32)
              + lax.dot_general(wlo, hlo, NT_DIMS, preferred_element_type=F32))
    e = jnp.exp(logits - jnp.max(logits, axis=0, keepdims=True))
    aff_ref[...] = e / jnp.sum(e, axis=0, keepdims=True)


def _outproj_router(y, w, x, mod, layer, row_of_batch, norm_g, w_router, *, tm):
    b, t, kdim = y.shape
    d = w.shape[1]
    e = w_router.shape[1]
    tm = min(tm, t)
    row = lambda bi, ti: (0, 0)
    return pl.pallas_call(
        _outproj_body,
        grid=(b, t // tm),
        in_specs=[
            pl.BlockSpec((None, tm, kdim), lambda bi, ti: (bi, ti, 0)),
            pl.BlockSpec((kdim, d), row),
            pl.BlockSpec((None, tm, d), lambda bi, ti: (bi, ti, 0)),
            _mod_spec(layer, 2, d, row_of_batch, 2, 0),
            pl.BlockSpec((1, d), row),
            _mod_spec(layer, 3, d, row_of_batch, 2, 0),
            _mod_spec(layer, 4, d, row_of_batch, 2, 0),
            pl.BlockSpec((e, d), row),
        ],
        out_specs=[
            pl.BlockSpec((None, tm, d), lambda bi, ti: (bi, ti, 0)),
            pl.BlockSpec((None, tm, d), lambda bi, ti: (bi, ti, 0)),
            pl.BlockSpec((None, e, tm), lambda bi, ti: (bi, 0, ti)),
        ],
        out_shape=[
            jax.ShapeDtypeStruct((b, t, d), F32),
            jax.ShapeDtypeStruct((b, t, d), BF16),
            jax.ShapeDtypeStruct((b, e, t), F32),
        ],
        scratch_shapes=[pltpu.VMEM((kdim, d), BF16)],
        compiler_params=_cparams(2, 48),
        name="outproj_router",
    )(y, w, x, mod, norm_g.reshape(1, d), mod, mod, w_router.T)


def _topk_body(aff_ref, slot_ref, wsel_ref, *, cap):
    a = aff_ref[...]
    r, t = a.shape
    capf = float(cap)

    def count_ge(v):
        return jnp.sum(jnp.where(a >= v, 1.0, 0.0), axis=1, keepdims=True)

    def bisect(_, lohi):
        lo, hi = lohi
        mid = 0.5 * (lo + hi)
        ok = count_ge(mid) >= capf
        return jnp.where(ok, mid, lo), jnp.where(ok, hi, mid)

    lo, hi = lax.fori_loop(0, 32, bisect, (jnp.zeros((r, 1), F32), jnp.full((r, 1), 2.0, F32)))

    def unfinished(state):
        return jnp.min(state[2]) < 0.5

    def walk(state):
        lo, hi, done = state
        top = jnp.max(jnp.where(a < hi, a, -1.0), axis=1, keepdims=True)
        found = (count_ge(top) >= capf) & (done < 0.5)
        moved = (done < 0.5) & jnp.logical_not(found)
        return jnp.where(found, top, lo), jnp.where(moved, top, hi), jnp.where(found, 1.0, done)

    thr, _, _ = lax.while_loop(unfinished, walk, (lo, hi, jnp.zeros((r, 1), F32)))

    gt = a > thr
    eq = a == thr
    need = capf - jnp.sum(jnp.where(gt, 1.0, 0.0), axis=1, keepdims=True)
    upper = jnp.where(lax.broadcasted_iota(jnp.int32, (LANES, LANES), 0)
                      < lax.broadcasted_iota(jnp.int32, (LANES, LANES), 1), 1.0, 0.0).astype(BF16)
    carry_eq = jnp.zeros((r, 1), F32)
    carry_sel = jnp.zeros((r, 1), F32)
    for j in range(t // LANES):
        sl = slice(j * LANES, (j + 1) * LANES)
        eqf = jnp.where(eq[:, sl], 1.0, 0.0)
        pre_eq = jnp.dot(eqf.astype(BF16), upper, preferred_element_type=F32) + carry_eq
        self_ = jnp.where(gt[:, sl], 1.0, jnp.where(pre_eq < need, eqf, 0.0))
        pre_sel = jnp.dot(self_.astype(BF16), upper, preferred_element_type=F32) + carry_sel
        sel = self_ > 0.5
        slot_ref[:, sl] = jnp.where(sel, pre_sel, -1.0).astype(jnp.int32)
        wsel_ref[:, sl] = jnp.where(sel, a[:, sl], 0.0)
        carry_eq = carry_eq + jnp.sum(eqf, axis=1, keepdims=True)
        carry_sel = carry_sel + jnp.sum(self_, axis=1, keepdims=True)


def _topk_select(aff):
    b, e, t = aff.shape
    cap = EC_CAPACITY * t // N_EXPERTS
    r = b * e
    full = pl.BlockSpec((r, t), lambda i: (0, 0))
    slot, wsel = pl.pallas_call(
        functools.partial(_topk_body, cap=cap),
        grid=(1,),
        in_specs=[full],
        out_specs=[full, full],
        out_shape=[jax.ShapeDtypeStruct((r, t), jnp.int32), jax.ShapeDtypeStruct((r, t), F32)],
        compiler_params=_cparams(1, 32),
        name="topk_select",
    )(aff.reshape(r, t))
    return slot.reshape(b, e, t), wsel.reshape(b, e, t)


def _gather_body(slot_ref, h_ref, o_ref):
    cap = o_ref.shape[0]
    t = h_ref.shape[0]
    rank = lax.broadcasted_iota(jnp.int32, (cap, t), 0)
    onehot = jnp.where(slot_ref[...] == rank, 1.0, 0.0).astype(BF16)
    o_ref[...] = jnp.dot(onehot, h_ref[...], preferred_element_type=F32).astype(o_ref.dtype)


def _gather(slot, hx):
    b, e, t = slot.shape
    d = hx.shape[2]
    cap = EC_CAPACITY * t // N_EXPERTS
    return pl.pallas_call(
        _gather_body,
        grid=(b, e),
        in_specs=[
            pl.BlockSpec((None, None, 1, t), lambda bi, ei: (bi, ei, 0, 0)),
            pl.BlockSpec((None, t, d), lambda bi, ei: (bi, 0, 0)),
        ],
        out_specs=pl.BlockSpec((None, cap, d), lambda bi, ei: (ei, bi, 0)),
        out_shape=jax.ShapeDtypeStruct((e, b * cap, d), BF16),
        compiler_params=_cparams(2, 32),
        name="moe_gather",
    )(slot.reshape(b, e, 1, t), hx)


def _ffn_body(*refs, n_streams, row_chunk):
    xs_refs = refs[:n_streams]
    wg_ref, wu_ref, wd_ref = refs[n_streams:n_streams + 3]
    ys_refs = refs[n_streams + 3:2 * n_streams + 3]
    acc_refs = refs[2 * n_streams + 3:3 * n_streams + 3]
    wgb_ref, wub_ref, wdb_ref = refs[3 * n_streams + 3:]
    f = pl.program_id(1)
    wgb_ref[...] = wg_ref[...].astype(BF16)
    wub_ref[...] = wu_ref[...].astype(BF16)
    wdb_ref[...] = wd_ref[...].astype(BF16)
    for xs_ref, ys_ref, acc_ref in zip(xs_refs, ys_refs, acc_refs):
        rc = min(row_chunk, xs_ref.shape[0])

        @pl.when(f == 0)
        def _(acc_ref=acc_ref):
            acc_ref[...] = jnp.zeros_like(acc_ref)

        def rows(i, carry, xs_ref=xs_ref, acc_ref=acc_ref, rc=rc):
            r0 = pl.multiple_of(i * rc, rc)
            xb = xs_ref[pl.ds(r0, rc), :]
            g = jnp.dot(xb, wgb_ref[...], preferred_element_type=F32)
            u = jnp.dot(xb, wub_ref[...], preferred_element_type=F32)
            hid = (_silu(g) * u).astype(BF16)
            acc_ref[pl.ds(r0, rc), :] += jnp.dot(hid, wdb_ref[...], preferred_element_type=F32)
            return carry

        lax.fori_loop(0, xs_ref.shape[0] // rc, rows, 0)

        @pl.when(f == pl.num_programs(1) - 1)
        def _(ys_ref=ys_ref, acc_ref=acc_ref):
            ys_ref[...] = acc_ref[...].astype(ys_ref.dtype)


def _expert_ffn(xs_list, w_gate, w_up, w_down, layer, *, tf=512, row_chunk=256):
    _, e, d, ff = w_gate.shape
    n = len(xs_list)
    xs_specs = [pl.BlockSpec((None, xs.shape[1], d), lambda ei, fi: (ei, 0, 0)) for xs in xs_list]
    return pl.pallas_call(
        functools.partial(_ffn_body, n_streams=n, row_chunk=row_chunk),
        grid=(e, ff // tf),
        in_specs=xs_specs + [
            pl.BlockSpec((None, None, d, tf), lambda ei, fi: (layer, ei, 0, fi)),
            pl.BlockSpec((None, None, d, tf), lambda ei, fi: (layer, ei, 0, fi)),
            pl.BlockSpec((None, None, tf, d), lambda ei, fi: (layer, ei, fi, 0)),
        ],
        out_specs=xs_specs,
        out_shape=[jax.ShapeDtypeStruct(xs.shape, BF16) for xs in xs_list],
        scratch_shapes=[pltpu.VMEM(xs.shape[1:], F32) for xs in xs_list]
        + [pltpu.VMEM((d, tf), BF16), pltpu.VMEM((d, tf), BF16), pltpu.VMEM((tf, d), BF16)],
        compiler_params=_cparams(2, 56),
        name="expert_ffn",
    )(*xs_list, w_gate, w_up, w_down)


def _combine_body(ys_ref, slot_ref, wsel_ref, x_ref, g2_ref, *rest, final_norm):
    if final_norm:
        fg_ref, o_ref = rest
    else:
        (o_ref,) = rest
    n_exp, cap, _ = ys_ref.shape
    tm = x_ref.shape[0]
    rank = lax.broadcasted_iota(jnp.int32, (tm, cap), 1)
    slots = slot_ref[...]
    wsel = wsel_ref[...]
    o_ref[...] = jnp.zeros_like(o_ref)
    for e in range(n_exp):
        onehot = jnp.where(slots[:, e:e + 1] == rank, 1.0, 0.0).astype(BF16)
        o_ref[...] += wsel[:, e:e + 1] * jnp.dot(onehot, ys_ref[e], preferred_element_type=F32)
    out = x_ref[...] + g2_ref[...] * o_ref[...]
    if final_norm:
        out = out * lax.rsqrt(jnp.mean(out * out, axis=-1, keepdims=True) + NORM_EPS) * fg_ref[...]
    o_ref[...] = out


def _combine(ys, slot_t, wsel_t, x, mod, layer, row_of_batch, final_g=None, *, tm=512):
    b, t, d = x.shape
    e = ys.shape[0]
    cap = ys.shape[1] // b
    tm = min(tm, t)
    in_specs = [
        pl.BlockSpec((e, cap, d), lambda bi, ti: (0, bi, 0)),
        pl.BlockSpec((None, tm, e), lambda bi, ti: (bi, ti, 0)),
        pl.BlockSpec((None, tm, e), lambda bi, ti: (bi, ti, 0)),
        pl.BlockSpec((None, tm, d), lambda bi, ti: (bi, ti, 0)),
        _mod_spec(layer, 5, d, row_of_batch, 2, 0),
    ]
    args = [ys, slot_t, wsel_t, x, mod]
    if final_g is not None:
        in_specs.append(pl.BlockSpec((1, d), lambda bi, ti: (0, 0)))
        args.append(final_g.reshape(1, d))
    return pl.pallas_call(
        functools.partial(_combine_body, final_norm=final_g is not None),
        grid=(b, t // tm),
        in_specs=in_specs,
        out_specs=pl.BlockSpec((None, tm, d), lambda bi, ti: (bi, ti, 0)),
        out_shape=jax.ShapeDtypeStruct((b, t, d), F32),
        compiler_params=_cparams(2, 48),
        name="moe_combine",
    )(*args)


def _attn_body(sink_ref, q_ref, k_ref, v_ref, kc_ref, vc_ref, o_ref, kband_ref, vband_ref):
    blk = SWA_BLOCK
    hd = SWA_HEAD_DIM
    t_ctx = kc_ref.shape[0]
    n_heads = q_ref.shape[1] // hd
    group = n_heads // SWA_KV_HEADS
    n = pl.program_id(1)
    nb = pl.num_programs(1)
    starts = (jnp.maximum(n - 1, 0), n, jnp.minimum(n + 1, nb - 1))
    kband_ref[0:t_ctx, :] = kc_ref[...]
    vband_ref[0:t_ctx, :] = vc_ref[...]
    for i, st in enumerate(starts):
        r0 = pl.multiple_of(st * blk, blk)
        kband_ref[t_ctx + i * blk:t_ctx + (i + 1) * blk, :] = k_ref[pl.ds(r0, blk), :]
        vband_ref[t_ctx + i * blk:t_ctx + (i + 1) * blk, :] = v_ref[pl.ds(r0, blk), :]
    qi = lax.broadcasted_iota(jnp.int32, (blk, blk), 0)
    kj = lax.broadcasted_iota(jnp.int32, (blk, blk), 1)
    neg = -jnp.inf
    bias_prev = jnp.where(kj >= qi, 0.0, neg) + jnp.where(n > 0, 0.0, neg)
    bias_next = jnp.where(kj <= qi, 0.0, neg) + jnp.where(n < nb - 1, 0.0, neg)
    bias = jnp.concatenate(
        [jnp.zeros((blk, t_ctx), F32), bias_prev, jnp.zeros((blk, blk), F32), bias_next], axis=1)
    for h in range(n_heads):
        kv = h // group
        qh = q_ref[:, h * hd:(h + 1) * hd] * jnp.asarray(hd ** -0.5, BF16)
        kh = kband_ref[:, kv * hd:(kv + 1) * hd]
        vh = vband_ref[:, kv * hd:(kv + 1) * hd]
        s = lax.dot_general(qh, kh, NT_DIMS, preferred_element_type=F32) + bias
        sink = sink_ref[h]
        m = jnp.maximum(jnp.max(s, axis=1, keepdims=True), sink)
        p = jnp.exp(s - m)
        den = jnp.sum(p, axis=1, keepdims=True) + jnp.exp(sink - m)
        o = jnp.dot(p.astype(BF16), vh, preferred_element_type=F32) / den
        o_ref[:, h * hd:(h + 1) * hd] = o.astype(o_ref.dtype)


def _window_attention(p, pc, sink):
    b, t, cols = p.shape
    t_ctx = pc.shape[1]
    kvw = SWA_KV_HEADS * SWA_HEAD_DIM
    qw = cols - 2 * kvw
    band = t_ctx + 3 * SWA_BLOCK
    return pl.pallas_call(
        _attn_body,
        grid=(b, t // SWA_BLOCK),
        in_specs=[
            pl.BlockSpec(memory_space=pltpu.SMEM),
            pl.BlockSpec((None, SWA_BLOCK, qw), lambda bi, ni: (bi, ni, 0)),
            pl.BlockSpec((None, t, kvw), lambda bi, ni: (bi, 0, qw // kvw)),
            pl.BlockSpec((None, t, kvw), lambda bi, ni: (bi, 0, qw // kvw + 1)),
            pl.BlockSpec((None, t_ctx, kvw), lambda bi, ni: (bi, 0, 0)),
            pl.BlockSpec((None, t_ctx, kvw), lambda bi, ni: (bi, 0, 1)),
        ],
        out_specs=pl.BlockSpec((None, SWA_BLOCK, qw), lambda bi, ni: (bi, ni, 0)),
        out_shape=jax.ShapeDtypeStruct((b, t, qw), BF16),
        scratch_shapes=[pltpu.VMEM((band, kvw), BF16), pltpu.VMEM((band, kvw), BF16)],
        compiler_params=_cparams(2, 32),
        name="window_attention",
    )(sink.astype(F32), p, p, p, pc, pc)


def _rope_tables(t):
    n_freq = SWA_HEAD_DIM // 4
    pos = jnp.arange(t)
    row = (pos // GRID_W).astype(F32)
    col = (pos % GRID_W).astype(F32)
    inv_freq = ROPE_BASE ** (-jnp.arange(n_freq, dtype=F32) / n_freq)
    ang_r, ang_c = row[:, None] * inv_freq, col[:, None] * inv_freq
    zero = jnp.zeros_like(ang_r)
    cos = jnp.concatenate([jnp.cos(ang_r)] * 2 + [jnp.cos(ang_c)] * 2, axis=1)
    sin_minus = jnp.concatenate([-jnp.sin(ang_r), zero, -jnp.sin(ang_c), zero], axis=1)
    sin_plus = jnp.concatenate([zero, jnp.sin(ang_r), zero, jnp.sin(ang_c)], axis=1)
    reps = LANES // SWA_HEAD_DIM
    return tuple(jnp.tile(tab, (1, reps)) for tab in (cos, sin_minus, sin_plus))


def _moe(streams, w_gate, w_up, w_down, layer):
    routed = []
    for hx, aff in streams:
        slot, wsel = _topk_select(aff)
        routed.append((_gather(slot, hx), jnp.swapaxes(slot, 1, 2), jnp.swapaxes(wsel, 1, 2)))
    ys_list = _expert_ffn([r[0] for r in routed], w_gate, w_up, w_down, layer)
    return [(ys, r[1], r[2]) for ys, r in zip(ys_list, routed)]


def kernel(x, c, ctx, c_ctx, ada_w, ada_b, norm1_g, norm2_g, final_g, ret_w_in, ret_decay_f, ret_decay_b, ret_gn_f, ret_gn_b, ret_w_out, swa_w_qkv, swa_sink, swa_w_out, moe_router, moe_w_gate, moe_w_up, moe_w_down):
    b, t, d = x.shape
    lat_row = lambda bi: bi
    ctx_row = lambda bi: b
    cond = jnp.zeros((COND_ROWS, d), F32).at[:b].set(c).at[b].set(c_ctx)
    mod = _adaln(cond, ada_w, ada_b)

    p = _norm_mod_matmul(x, norm1_g[0], mod, 0, lat_row, ret_w_in[0], tm=512, tn=2048)
    pc = _norm_mod_matmul(ctx, norm1_g[0], mod, 0, ctx_row, ret_w_in[0], tm=512, tn=2048)
    y, yc = _retention(p, pc, ret_decay_f[0], ret_decay_b[0], ret_gn_f[0], ret_gn_b[0])
    x1, hx, aff = _outproj_router(y, ret_w_out[0], x, mod, 0, lat_row, norm2_g[0], moe_router[0], tm=512)
    c1, hc, affc = _outproj_router(yc, ret_w_out[0], ctx, mod, 0, ctx_row, norm2_g[0], moe_router[0], tm=512)
    (ys, slot_t, wsel_t), (ysc, slotc_t, wselc_t) = _moe(
        [(hx, aff), (hc, affc)], moe_w_gate, moe_w_up, moe_w_down, 0)
    x2 = _combine(ys, slot_t, wsel_t, x1, mod, 0, lat_row)
    c2 = _combine(ysc, slotc_t, wselc_t, c1, mod, 0, ctx_row)

    qw = SWA_HEAD_DIM * (swa_w_qkv.shape[2] // SWA_HEAD_DIM - 2 * SWA_KV_HEADS)
    cos, sin_minus, sin_plus = _rope_tables(t)
    rope_groups = (qw + SWA_KV_HEADS * SWA_HEAD_DIM) // LANES
    p = _norm_mod_matmul(x2, norm1_g[1], mod, 1, lat_row, swa_w_qkv[0], tm=512,
                         tn=swa_w_qkv.shape[2], rope=(cos, sin_minus, sin_plus, rope_groups))
    w_kv = swa_w_qkv[0][:, qw:]
    pc = _norm_mod_matmul(c2, norm1_g[1], mod, 1, ctx_row, w_kv, tm=512, tn=w_kv.shape[1])
    o = _window_attention(p, pc, swa_sink[0])
    x3, hx, aff = _outproj_router(o, swa_w_out[0], x2, mod, 1, lat_row, norm2_g[1], moe_router[1], tm=512)
    ((ys, slot_t, wsel_t),) = _moe([(hx, aff)], moe_w_gate, moe_w_up, moe_w_down, 1)
    return _combine(ys, slot_t, wsel_t, x3, mod, 1, lat_row, final_g)
```

```python
import functools

import jax
import jax.numpy as jnp
from jax import lax
from jax.experimental import pallas as pl
from jax.experimental.pallas import tpu as pltpu

F32 = jnp.float32
BF16 = jnp.bfloat16

NORM_EPS = 1e-6
GRID_W = 64
RET_HEADS = 4
RET_CHUNK = 128
SWA_HEAD_DIM = 64
SWA_KV_HEADS = 2
SWA_BLOCK = 128
ROPE_BASE = 10000.0
N_EXPERTS = 16
EC_CAPACITY = 2

LANES = 128
MIB = 1024 * 1024
COND_ROWS = 16

NT_DIMS = (((1,), (1,)), ((), ()))
TN_DIMS = (((0,), (0,)), ((), ()))


def _cparams(n_axes, vmem_mib):
    return pltpu.CompilerParams(
        dimension_semantics=("arbitrary",) * n_axes,
        vmem_limit_bytes=vmem_mib * MIB,
    )


def _silu(v):
    return v / (1.0 + jnp.exp(-v))


def _rmsnorm_mod(xv, g, shift, scale):
    h = xv * lax.rsqrt(jnp.mean(xv * xv, axis=-1, keepdims=True) + NORM_EPS) * g
    return h * (1.0 + scale) + shift


def _mod_spec(layer, chunk, d, row_of_batch, n_grid_axes, batch_axis):
    def index_map(*ids):
        return (layer, row_of_batch(ids[batch_axis]), 0, chunk)
    del n_grid_axes
    return pl.BlockSpec((None, None, 1, d), index_map)


def _adaln_body(cond_ref, w_ref, b_ref, o_ref):
    s = _silu(cond_ref[...])
    o_ref[...] = jnp.dot(s.astype(BF16), w_ref[...].astype(BF16),
                         preferred_element_type=F32) + b_ref[...]


def _adaln(cond, ada_w, ada_b):
    layers, d, n = ada_w.shape
    tn = d
    out = pl.pallas_call(
        _adaln_body,
        grid=(layers, n // tn),
        in_specs=[
            pl.BlockSpec((COND_ROWS, d), lambda l, j: (0, 0)),
            pl.BlockSpec((None, d, tn), lambda l, j: (l, 0, j)),
            pl.BlockSpec((None, 1, tn), lambda l, j: (l, 0, j)),
        ],
        out_specs=pl.BlockSpec((None, COND_ROWS, tn), lambda l, j: (l, 0, j)),
        out_shape=jax.ShapeDtypeStruct((layers, COND_ROWS, n), F32),
        compiler_params=_cparams(2, 32),
        name="adaln",
    )(cond, ada_w, ada_b.reshape(layers, 1, n))
    return out.reshape(layers, COND_ROWS, 1, n)


def _nmm_body(x_ref, g_ref, sh_ref, sc_ref, w_ref, *rest, rope_groups):
    if rope_groups:
        cos_ref, sm_ref, sp_ref, o_ref, wbf_ref = rest
    else:
        o_ref, wbf_ref = rest

    @pl.when((pl.program_id(1) == 0) & (pl.program_id(2) == 0))
    def _():
        wbf_ref[...] = w_ref[...].astype(BF16)

    h = _rmsnorm_mod(x_ref[...], g_ref[...], sh_ref[...], sc_ref[...])
    acc = jnp.dot(h.astype(BF16), wbf_ref[...], preferred_element_type=F32)
    if not rope_groups:
        o_ref[...] = acc.astype(o_ref.dtype)
        return
    c, sm, sp = cos_ref[...], sm_ref[...], sp_ref[...]
    n_groups = acc.shape[1] // LANES
    for j in range(n_groups):
        a = acc[:, j * LANES:(j + 1) * LANES]
        if j < rope_groups:
            a = (a * c + pltpu.roll(a, LANES - 16, axis=1) * sm
                 + pltpu.roll(a, 16, axis=1) * sp)
        o_ref[:, j * LANES:(j + 1) * LANES] = a.astype(o_ref.dtype)


def _norm_mod_matmul(x, g, mod, layer, row_of_batch, w, *, tm, tn, rope=None):
    b, t, d = x.shape
    n = w.shape[1]
    tm = min(tm, t)
    grid = (n // tn, b, t // tm)
    in_specs = [
        pl.BlockSpec((None, tm, d), lambda j, bi, ti: (bi, ti, 0)),
        pl.BlockSpec((1, d), lambda j, bi, ti: (0, 0)),
        _mod_spec(layer, 0, d, row_of_batch, 3, 1),
        _mod_spec(layer, 1, d, row_of_batch, 3, 1),
        pl.BlockSpec((d, tn), lambda j, bi, ti: (0, j)),
    ]
    args = [x, g.reshape(1, d), mod, mod, w]
    rope_groups = 0
    if rope is not None:
        cos, sm, sp, rope_groups = rope
        for tab in (cos, sm, sp):
            in_specs.append(pl.BlockSpec((tm, LANES), lambda j, bi, ti: (ti, 0)))
            args.append(tab)
    return pl.pallas_call(
        functools.partial(_nmm_body, rope_groups=rope_groups),
        grid=grid,
        in_specs=in_specs,
        out_specs=pl.BlockSpec((None, tm, tn), lambda j, bi, ti: (bi, ti, j)),
        out_shape=jax.ShapeDtypeStruct((b, t, n), BF16),
        scratch_shapes=[pltpu.VMEM((d, tn), BF16)],
        compiler_params=_cparams(3, 48),
        name="norm_mod_proj",
    )(*args)


def _ret_body(dec_ref, gnf_ref, gnb_ref,
              qc_ref, kc_ref, vc_ref, gfc_ref, gbc_ref,
              q_ref, k_ref, v_ref, gf_ref, gb_ref,
              y_ref, yc_ref,
              statef_ref, stateb_ref, yacc_ref, intra_ref, qd_ref, kd_ref, cd_ref):
    c = RET_CHUNK
    dk = q_ref.shape[1]
    dv = v_ref.shape[1]
    t_ctx = qc_ref.shape[0]
    t_lat = q_ref.shape[0]

    dec = dec_ref[...]
    lg = jnp.minimum(dec, 0.0) - jnp.log1p(jnp.exp(-jnp.abs(dec)))
    row = lax.broadcasted_iota(jnp.int32, (c, c), 0)
    col = lax.broadcasted_iota(jnp.int32, (c, c), 1)
    pos_v = lax.broadcasted_iota(jnp.int32, (c, dv), 0).astype(F32)
    pos_k = lax.broadcasted_iota(jnp.int32, (c, dk), 0).astype(F32)
    for d in range(2):
        lg_c, lg_v, lg_k = lg[d:d + 1, :c], lg[d:d + 1, :], lg[d:d + 1, :dk]
        diff = (row - col) if d == 0 else (col - row)
        intra_ref[d] = jnp.where(diff >= 0, jnp.exp(jnp.maximum(diff, 0).astype(F32) * lg_c), 0.0)
        if d == 0:
            qd_ref[d] = jnp.exp((pos_v + 1.0) * lg_v)
            kd_ref[d] = jnp.exp((c - 1.0 - pos_k) * lg_k)
        else:
            qd_ref[d] = jnp.exp((c - pos_v) * lg_v)
            kd_ref[d] = jnp.exp(pos_k * lg_k)
        cd_ref[d] = jnp.exp(float(c) * lg_v)

    def chunk(d, refs, r0, yoff, out_ref, first_visit):
        qr, kr, vr, gr = refs
        state_ref = statef_ref if d == 0 else stateb_ref
        r0 = pl.multiple_of(r0, c)
        qs = qr[pl.ds(r0, c), :] * jnp.asarray(dk ** -0.5, BF16)
        kc = kr[pl.ds(r0, c), :]
        vc = vr[pl.ds(r0, c), :]
        s = lax.dot_general(qs, kc, NT_DIMS, preferred_element_type=F32) * intra_ref[d]
        st = state_ref[...]
        o = (jnp.dot(s.astype(BF16), vc, preferred_element_type=F32)
             + jnp.dot(qs, st.astype(BF16), preferred_element_type=F32) * qd_ref[d])
        kk = (kc.astype(F32) * kd_ref[d]).astype(BF16)
        state_ref[...] = cd_ref[d] * st + lax.dot_general(kk, vc, TN_DIMS, preferred_element_type=F32)
        mu = jnp.mean(o, axis=-1, keepdims=True)
        oc = o - mu
        var = jnp.mean(oc * oc, axis=-1, keepdims=True)
        gn = gnf_ref[...] if d == 0 else gnb_ref[...]
        y = _silu(gr[pl.ds(r0, c), :].astype(F32)) * (oc * lax.rsqrt(var + NORM_EPS) * gn)
        yoff = pl.multiple_of(yoff, c)
        if first_visit:
            yacc_ref[pl.ds(yoff, c), :] = y
        else:
            out_ref[pl.ds(r0, c), :] = (yacc_ref[pl.ds(yoff, c), :] + y).astype(out_ref.dtype)

    statef_ref[...] = jnp.zeros_like(statef_ref)
    stateb_ref[...] = jnp.zeros_like(stateb_ref)
    segments = (
        ((qc_ref, kc_ref, vc_ref), (gfc_ref, gbc_ref), t_ctx // c, 0, yc_ref),
        ((q_ref, k_ref, v_ref), (gf_ref, gb_ref), t_lat // c, t_ctx, y_ref),
    )
    for qkv, gates, n, ybase, out_ref in segments:
        for first_visit in (True, False):

            def step(i, carry, qkv=qkv, gates=gates, n=n, ybase=ybase, out_ref=out_ref,
                     first_visit=first_visit):
                for d in range(2):
                    ci = i if d == 0 else n - 1 - i
                    chunk(d, qkv + (gates[d],), ci * c, ybase + ci * c, out_ref, first_visit)
                return carry

            lo, hi = (0, n // 2) if first_visit else (n // 2, n)
            lax.fori_loop(lo, hi, step, 0, unroll=4 if hi - lo >= 4 else 1)


def _retention(p, pc, decay_f, decay_b, gn_f, gn_b):
    b, t, cols = p.shape
    t_ctx = pc.shape[1]
    assert t % (2 * RET_CHUNK) == 0 and t_ctx % (2 * RET_CHUNK) == 0
    h = RET_HEADS
    dk = cols // (8 * h)
    dv = 2 * dk
    hv = h * dv
    dec = jnp.broadcast_to(jnp.stack([decay_f, decay_b], axis=1)[:, :, None], (h, 2, dv)).astype(F32)
    k_blk, v_blk, gf_blk, gb_blk = h, (2 * h * dk) // dv, (2 * h * dk + hv) // dv, (2 * h * dk + 2 * hv) // dv

    def specs(rows):
        return [
            pl.BlockSpec((None, rows, dk), lambda bi, hi: (bi, 0, hi)),
            pl.BlockSpec((None, rows, dk), lambda bi, hi: (bi, 0, k_blk + hi)),
            pl.BlockSpec((None, rows, dv), lambda bi, hi: (bi, 0, v_blk + hi)),
            pl.BlockSpec((None, rows, dv), lambda bi, hi: (bi, 0, gf_blk + hi)),
            pl.BlockSpec((None, rows, dv), lambda bi, hi: (bi, 0, gb_blk + hi)),
        ]

    gn_spec = pl.BlockSpec((1, dv), lambda bi, hi: (0, hi))
    return pl.pallas_call(
        _ret_body,
        grid=(b, h),
        in_specs=[pl.BlockSpec((None, 2, dv), lambda bi, hi: (hi, 0, 0)), gn_spec, gn_spec]
        + specs(t_ctx) + specs(t),
        out_specs=[
            pl.BlockSpec((None, t, dv), lambda bi, hi: (bi, 0, hi)),
            pl.BlockSpec((None, t_ctx, dv), lambda bi, hi: (bi, 0, hi)),
        ],
        out_shape=[
            jax.ShapeDtypeStruct((b, t, hv), BF16),
            jax.ShapeDtypeStruct((b, t_ctx, hv), BF16),
        ],
        scratch_shapes=[
            pltpu.VMEM((dk, dv), F32),
            pltpu.VMEM((dk, dv), F32),
            pltpu.VMEM((t_ctx + t, dv), F32),
            pltpu.VMEM((2, RET_CHUNK, RET_CHUNK), F32),
            pltpu.VMEM((2, RET_CHUNK, dv), F32),
            pltpu.VMEM((2, RET_CHUNK, dk), F32),
            pltpu.VMEM((2, 1, dv), F32),
        ],
        compiler_params=_cparams(2, 48),
        name="retention",
    )(dec, gn_f.reshape(1, hv), gn_b.reshape(1, hv), pc, pc, pc, pc, pc, p, p, p, p, p)


def _outproj_body(y_ref, w_ref, x_ref, g1_ref, ng_ref, sh_ref, sc_ref, wr_ref,
                  x1_ref, hx_ref, aff_ref, wbf_ref):
    @pl.when((pl.program_id(0) == 0) & (pl.program_id(1) == 0))
    def _():
        wbf_ref[...] = w_ref[...].astype(BF16)

    x1 = x_ref[...] + g1_ref[...] * jnp.dot(y_ref[...], wbf_ref[...], preferred_element_type=F32)
    x1_ref[...] = x1
    h = _rmsnorm_mod(x1, ng_ref[...], sh_ref[...], sc_ref[...])
    hb = h.astype(BF16)
    hx_ref[...] = hb
    hlo = (h - hb.astype(F32)).astype(BF16)
    wr = wr_ref[...]
    whi = wr.astype(BF16)
    wlo = (wr - whi.astype(F32)).astype(BF16)
    logits = (lax.dot_general(whi, hb, NT_DIMS, preferred_element_type=F32)
              + lax.dot_general(whi, hlo, NT_DIMS, preferred_element_type=F32)
              + lax.dot_general(wlo, hb, NT_DIMS, preferred_element_type=F32)
              + lax.dot_general(wlo, hlo, NT_DIMS, preferred_element_type=F32))
    e = jnp.exp(logits - jnp.max(logits, axis=0, keepdims=True))
    aff_ref[...] = e / jnp.sum(e, axis=0, keepdims=True)


def _outproj_router(y, w, x, mod, layer, row_of_batch, norm_g, w_router, *, tm):
    b, t, kdim = y.shape
    d = w.shape[1]
    e = w_router.shape[1]
    tm = min(tm, t)
    row = lambda bi, ti: (0, 0)
    return pl.pallas_call(
        _outproj_body,
        grid=(b, t // tm),
        in_specs=[
            pl.BlockSpec((None, tm, kdim), lambda bi, ti: (bi, ti, 0)),
            pl.BlockSpec((kdim, d), row),
            pl.BlockSpec((None, tm, d), lambda bi, ti: (bi, ti, 0)),
            _mod_spec(layer, 2, d, row_of_batch, 2, 0),
            pl.BlockSpec((1, d), row),
            _mod_spec(layer, 3, d, row_of_batch, 2, 0),
            _mod_spec(layer, 4, d, row_of_batch, 2, 0),
            pl.BlockSpec((e, d), row),
        ],
        out_specs=[
            pl.BlockSpec((None, tm, d), lambda bi, ti: (bi, ti, 0)),
            pl.BlockSpec((None, tm, d), lambda bi, ti: (bi, ti, 0)),
            pl.BlockSpec((None, e, tm), lambda bi, ti: (bi, 0, ti)),
        ],
        out_shape=[
            jax.ShapeDtypeStruct((b, t, d), F32),
            jax.ShapeDtypeStruct((b, t, d), BF16),
            jax.ShapeDtypeStruct((b, e, t), F32),
        ],
        scratch_shapes=[pltpu.VMEM((kdim, d), BF16)],
        compiler_params=_cparams(2, 48),
        name="outproj_router",
    )(y, w, x, mod, norm_g.reshape(1, d), mod, mod, w_router.T)


def _topk_body(aff_ref, slot_ref, wsel_ref, *, cap):
    a = aff_ref[...]
    r, t = a.shape
    capf = float(cap)

    def count_ge(v):
        return jnp.sum(jnp.where(a >= v, 1.0, 0.0), axis=1, keepdims=True)

    def bisect(_, lohi):
        lo, hi = lohi
        mid = 0.5 * (lo + hi)
        ok = count_ge(mid) >= capf
        return jnp.where(ok, mid, lo), jnp.where(ok, hi, mid)

    lo, hi = lax.fori_loop(0, 32, bisect, (jnp.zeros((r, 1), F32), jnp.full((r, 1), 2.0, F32)))

    def unfinished(state):
        return jnp.min(state[2]) < 0.5

    def walk(state):
        lo, hi, done = state
        top = jnp.max(jnp.where(a < hi, a, -1.0), axis=1, keepdims=True)
        found = (count_ge(top) >= capf) & (done < 0.5)
        moved = (done < 0.5) & jnp.logical_not(found)
        return jnp.where(found, top, lo), jnp.where(moved, top, hi), jnp.where(found, 1.0, done)

    thr, _, _ = lax.while_loop(unfinished, walk, (lo, hi, jnp.zeros((r, 1), F32)))

    gt = a > thr
    eq = a == thr
    need = capf - jnp.sum(jnp.where(gt, 1.0, 0.0), axis=1, keepdims=True)
    upper = jnp.where(lax.broadcasted_iota(jnp.int32, (LANES, LANES), 0)
                      < lax.broadcasted_iota(jnp.int32, (LANES, LANES), 1), 1.0, 0.0).astype(BF16)
    carry_eq = jnp.zeros((r, 1), F32)
    carry_sel = jnp.zeros((r, 1), F32)
    for j in range(t // LANES):
        sl = slice(j * LANES, (j + 1) * LANES)
        eqf = jnp.where(eq[:, sl], 1.0, 0.0)
        pre_eq = jnp.dot(eqf.astype(BF16), upper, preferred_element_type=F32) + carry_eq
        self_ = jnp.where(gt[:, sl], 1.0, jnp.where(pre_eq < need, eqf, 0.0))
        pre_sel = jnp.dot(self_.astype(BF16), upper, preferred_element_type=F32) + carry_sel
        sel = self_ > 0.5
        slot_ref[:, sl] = jnp.where(sel, pre_sel, -1.0).astype(jnp.int32)
        wsel_ref[:, sl] = jnp.where(sel, a[:, sl], 0.0)
        carry_eq = carry_eq + jnp.sum(eqf, axis=1, keepdims=True)
        carry_sel = carry_sel + jnp.sum(self_, axis=1, keepdims=True)


def _topk_select(aff):
    b, e, t = aff.shape
    cap = EC_CAPACITY * t // N_EXPERTS
    r = b * e
    full = pl.BlockSpec((r, t), lambda i: (0, 0))
    slot, wsel = pl.pallas_call(
        functools.partial(_topk_body, cap=cap),
        grid=(1,),
        in_specs=[full],
        out_specs=[full, full],
        out_shape=[jax.ShapeDtypeStruct((r, t), jnp.int32), jax.ShapeDtypeStruct((r, t), F32)],
        compiler_params=_cparams(1, 32),
        name="topk_select",
    )(aff.reshape(r, t))
    return slot.reshape(b, e, t), wsel.reshape(b, e, t)


def _gather_body(slot_ref, h_ref, o_ref, onehot_ref):
    n_exp, cap, d = o_ref.shape
    t = h_ref.shape[0]
    rank = lax.broadcasted_iota(jnp.int32, (cap, t), 0)
    for e in range(n_exp):
        onehot_ref[e * cap:(e + 1) * cap, :] = jnp.where(
            slot_ref[e:e + 1, :] == rank, 1.0, 0.0).astype(BF16)
    xs = jnp.dot(onehot_ref[...], h_ref[...], preferred_element_type=F32)
    o_ref[...] = xs.reshape(n_exp, cap, d).astype(o_ref.dtype)


def _gather(slot, hx, *, experts_per_step=8):
    b, e, t = slot.shape
    d = hx.shape[2]
    cap = EC_CAPACITY * t // N_EXPERTS
    eg = experts_per_step
    return pl.pallas_call(
        _gather_body,
        grid=(b, e // eg),
        in_specs=[
            pl.BlockSpec((None, eg, t), lambda bi, gi: (bi, gi, 0)),
            pl.BlockSpec((None, t, d), lambda bi, gi: (bi, 0, 0)),
        ],
        out_specs=pl.BlockSpec((eg, cap, d), lambda bi, gi: (gi, bi, 0)),
        out_shape=jax.ShapeDtypeStruct((e, b * cap, d), BF16),
        scratch_shapes=[pltpu.VMEM((eg * cap, t), BF16)],
        compiler_params=_cparams(2, 48),
        name="moe_gather",
    )(slot, hx)


def _ffn_body(*refs, n_streams, row_chunk):
    xs_refs = refs[:n_streams]
    wg_ref, wu_ref, wd_ref = refs[n_streams:n_streams + 3]
    ys_refs = refs[n_streams + 3:2 * n_streams + 3]
    acc_refs = refs[2 * n_streams + 3:3 * n_streams + 3]
    wgb_ref, wub_ref, wdb_ref = refs[3 * n_streams + 3:]
    f = pl.program_id(1)
    wgb_ref[...] = wg_ref[...].astype(BF16)
    wub_ref[...] = wu_ref[...].astype(BF16)
    wdb_ref[...] = wd_ref[...].astype(BF16)
    for xs_ref, ys_ref, acc_ref in zip(xs_refs, ys_refs, acc_refs):
        rc = min(row_chunk, xs_ref.shape[0])

        @pl.when(f == 0)
        def _(acc_ref=acc_ref):
            acc_ref[...] = jnp.zeros_like(acc_ref)

        def rows(i, carry, xs_ref=xs_ref, acc_ref=acc_ref, rc=rc):
            r0 = pl.multiple_of(i * rc, rc)
            xb = xs_ref[pl.ds(r0, rc), :]
            g = jnp.dot(xb, wgb_ref[...], preferred_element_type=F32)
            u = jnp.dot(xb, wub_ref[...], preferred_element_type=F32)
            hid = (_silu(g) * u).astype(BF16)
            acc_ref[pl.ds(r0, rc), :] += jnp.dot(hid, wdb_ref[...], preferred_element_type=F32)
            return carry

        lax.fori_loop(0, xs_ref.shape[0] // rc, rows, 0)

        @pl.when(f == pl.num_programs(1) - 1)
        def _(ys_ref=ys_ref, acc_ref=acc_ref):
            ys_ref[...] = acc_ref[...].astype(ys_ref.dtype)


def _expert_ffn(xs_list, w_gate, w_up, w_down, layer, *, tf=512, row_chunk=1024):
    _, e, d, ff = w_gate.shape
    n = len(xs_list)
    xs_specs = [pl.BlockSpec((None, xs.shape[1], d), lambda ei, fi: (ei, 0, 0)) for xs in xs_list]
    return pl.pallas_call(
        functools.partial(_ffn_body, n_streams=n, row_chunk=row_chunk),
        grid=(e, ff // tf),
        in_specs=xs_specs + [
            pl.BlockSpec((None, None, d, tf), lambda ei, fi: (layer, ei, 0, fi)),
            pl.BlockSpec((None, None, d, tf), lambda ei, fi: (layer, ei, 0, fi)),
            pl.BlockSpec((None, None, tf, d), lambda ei, fi: (layer, ei, fi, 0)),
        ],
        out_specs=xs_specs,
        out_shape=[jax.ShapeDtypeStruct(xs.shape, BF16) for xs in xs_list],
        scratch_shapes=[pltpu.VMEM(xs.shape[1:], F32) for xs in xs_list]
        + [pltpu.VMEM((d, tf), BF16), pltpu.VMEM((d, tf), BF16), pltpu.VMEM((tf, d), BF16)],
        compiler_params=_cparams(2, 56),
        name="expert_ffn",
    )(*xs_list, w_gate, w_up, w_down)


def _combine_body(ys_ref, slot_ref, wsel_ref, x_ref, g2_ref, *rest, final_norm):
    if final_norm:
        fg_ref, o_ref = rest
    else:
        (o_ref,) = rest
    n_exp, cap, _ = ys_ref.shape
    tm = x_ref.shape[0]
    rank = lax.broadcasted_iota(jnp.int32, (tm, cap), 1)
    slots = slot_ref[...]
    wsel = wsel_ref[...]
    o_ref[...] = jnp.zeros_like(o_ref)
    for e in range(n_exp):
        onehot = jnp.where(slots[:, e:e + 1] == rank, 1.0, 0.0).astype(BF16)
        o_ref[...] += wsel[:, e:e + 1] * jnp.dot(onehot, ys_ref[e], preferred_element_type=F32)
    out = x_ref[...] + g2_ref[...] * o_ref[...]
    if final_norm:
        out = out * lax.rsqrt(jnp.mean(out * out, axis=-1, keepdims=True) + NORM_EPS) * fg_ref[...]
    o_ref[...] = out


def _combine(ys, slot_t, wsel_t, x, mod, layer, row_of_batch, final_g=None, *, tm=512):
    b, t, d = x.shape
    e = ys.shape[0]
    cap = ys.shape[1] // b
    tm = min(tm, t)
    in_specs = [
        pl.BlockSpec((e, cap, d), lambda bi, ti: (0, bi, 0)),
        pl.BlockSpec((None, tm, e), lambda bi, ti: (bi, ti, 0)),
        pl.BlockSpec((None, tm, e), lambda bi, ti: (bi, ti, 0)),
        pl.BlockSpec((None, tm, d), lambda bi, ti: (bi, ti, 0)),
        _mod_spec(layer, 5, d, row_of_batch, 2, 0),
    ]
    args = [ys, slot_t, wsel_t, x, mod]
    if final_g is not None:
        in_specs.append(pl.BlockSpec((1, d), lambda bi, ti: (0, 0)))
        args.append(final_g.reshape(1, d))
    return pl.pallas_call(
        functools.partial(_combine_body, final_norm=final_g is not None),
        grid=(b, t // tm),
        in_specs=in_specs,
        out_specs=pl.BlockSpec((None, tm, d), lambda bi, ti: (bi, ti, 0)),
        out_shape=jax.ShapeDtypeStruct((b, t, d), F32),
        compiler_params=_cparams(2, 48),
        name="moe_combine",
    )(*args)


def _attn_body(sink_ref, q_ref, k_ref, v_ref, kc_ref, vc_ref, o_ref, kband_ref, vband_ref):
    blk = SWA_BLOCK
    hd = SWA_HEAD_DIM
    t_ctx = kc_ref.shape[0]
    n_heads = q_ref.shape[1] // hd
    group = n_heads // SWA_KV_HEADS
    n = pl.program_id(1)
    nb = pl.num_programs(1)
    starts = (jnp.maximum(n - 1, 0), n, jnp.minimum(n + 1, nb - 1))
    kband_ref[0:t_ctx, :] = kc_ref[...]
    vband_ref[0:t_ctx, :] = vc_ref[...]
    for i, st in enumerate(starts):
        r0 = pl.multiple_of(st * blk, blk)
        kband_ref[t_ctx + i * blk:t_ctx + (i + 1) * blk, :] = k_ref[pl.ds(r0, blk), :]
        vband_ref[t_ctx + i * blk:t_ctx + (i + 1) * blk, :] = v_ref[pl.ds(r0, blk), :]
    qi = lax.broadcasted_iota(jnp.int32, (blk, blk), 0)
    kj = lax.broadcasted_iota(jnp.int32, (blk, blk), 1)
    neg = -jnp.inf
    bias_prev = jnp.where(kj >= qi, 0.0, neg) + jnp.where(n > 0, 0.0, neg)
    bias_next = jnp.where(kj <= qi, 0.0, neg) + jnp.where(n < nb - 1, 0.0, neg)
    bias = jnp.concatenate(
        [jnp.zeros((blk, t_ctx), F32), bias_prev, jnp.zeros((blk, blk), F32), bias_next], axis=1)
    band = bias.shape[1]
    head_idx = lax.broadcasted_iota(jnp.int32, (group, 1, 1), 0)
    for kv in range(SWA_KV_HEADS):
        heads = range(kv * group, (kv + 1) * group)
        qg = jnp.concatenate([q_ref[:, h * hd:(h + 1) * hd] for h in heads], axis=0)
        qg = qg * jnp.asarray(hd ** -0.5, BF16)
        kh = kband_ref[:, kv * hd:(kv + 1) * hd]
        vh = vband_ref[:, kv * hd:(kv + 1) * hd]
        s = lax.dot_general(qg, kh, NT_DIMS, preferred_element_type=F32)
        s = s.reshape(group, blk, band) + bias[None]
        sink = jnp.zeros((group, 1, 1), F32)
        for i, h in enumerate(heads):
            sink = jnp.where(head_idx == i, sink_ref[h], sink)
        m = jnp.maximum(jnp.max(s, axis=-1, keepdims=True), sink)
        p = jnp.exp(s - m)
        den = jnp.sum(p, axis=-1, keepdims=True) + jnp.exp(sink - m)
        o = jnp.dot(p.reshape(group * blk, band).astype(BF16), vh, preferred_element_type=F32)
        o = o.reshape(group, blk, hd) / den
        for i, h in enumerate(heads):
            o_ref[:, h * hd:(h + 1) * hd] = o[i].astype(o_ref.dtype)


def _window_attention(p, pc, sink):
    b, t, cols = p.shape
    t_ctx = pc.shape[1]
    kvw = SWA_KV_HEADS * SWA_HEAD_DIM
    qw = cols - 2 * kvw
    band = t_ctx + 3 * SWA_BLOCK
    return pl.pallas_call(
        _attn_body,
        grid=(b, t // SWA_BLOCK),
        in_specs=[
            pl.BlockSpec(memory_space=pltpu.SMEM),
            pl.BlockSpec((None, SWA_BLOCK, qw), lambda bi, ni: (bi, ni, 0)),
            pl.BlockSpec((None, t, kvw), lambda bi, ni: (bi, 0, qw // kvw)),
            pl.BlockSpec((None, t, kvw), lambda bi, ni: (bi, 0, qw // kvw + 1)),
            pl.BlockSpec((None, t_ctx, kvw), lambda bi, ni: (bi, 0, 0)),
            pl.BlockSpec((None, t_ctx, kvw), lambda bi, ni: (bi, 0, 1)),
        ],
        out_specs=pl.BlockSpec((None, SWA_BLOCK, qw), lambda bi, ni: (bi, ni, 0)),
        out_shape=jax.ShapeDtypeStruct((b, t, qw), BF16),
        scratch_shapes=[pltpu.VMEM((band, kvw), BF16), pltpu.VMEM((band, kvw), BF16)],
        compiler_params=_cparams(2, 32),
        name="window_attention",
    )(sink.astype(F32), p, p, p, pc, pc)


def _rope_tables(t):
    n_freq = SWA_HEAD_DIM // 4
    pos = jnp.arange(t)
    row = (pos // GRID_W).astype(F32)
    col = (pos % GRID_W).astype(F32)
    inv_freq = ROPE_BASE ** (-jnp.arange(n_freq, dtype=F32) / n_freq)
    ang_r, ang_c = row[:, None] * inv_freq, col[:, None] * inv_freq
    zero = jnp.zeros_like(ang_r)
    cos = jnp.concatenate([jnp.cos(ang_r)] * 2 + [jnp.cos(ang_c)] * 2, axis=1)
    sin_minus = jnp.concatenate([-jnp.sin(ang_r), zero, -jnp.sin(ang_c), zero], axis=1)
    sin_plus = jnp.concatenate([zero, jnp.sin(ang_r), zero, jnp.sin(ang_c)], axis=1)
    reps = LANES // SWA_HEAD_DIM
    return tuple(jnp.tile(tab, (1, reps)) for tab in (cos, sin_minus, sin_plus))


def _moe(streams, w_gate, w_up, w_down, layer):
    routed = []
    for hx, aff in streams:
        slot, wsel = _topk_select(aff)
        routed.append((_gather(slot, hx), jnp.swapaxes(slot, 1, 2), jnp.swapaxes(wsel, 1, 2)))
    ys_list = _expert_ffn([r[0] for r in routed], w_gate, w_up, w_down, layer)
    return [(ys, r[1], r[2]) for ys, r in zip(ys_list, routed)]


def kernel(x, c, ctx, c_ctx, ada_w, ada_b, norm1_g, norm2_g, final_g, ret_w_in, ret_decay_f, ret_decay_b, ret_gn_f, ret_gn_b, ret_w_out, swa_w_qkv, swa_sink, swa_w_out, moe_router, moe_w_gate, moe_w_up, moe_w_down):
    b, t, d = x.shape
    lat_row = lambda bi: bi
    ctx_row = lambda bi: b
    cond = jnp.zeros((COND_ROWS, d), F32).at[:b].set(c).at[b].set(c_ctx)
    mod = _adaln(cond, ada_w, ada_b)

    p = _norm_mod_matmul(x, norm1_g[0], mod, 0, lat_row, ret_w_in[0], tm=512, tn=2048)
    pc = _norm_mod_matmul(ctx, norm1_g[0], mod, 0, ctx_row, ret_w_in[0], tm=512, tn=2048)
    y, yc = _retention(p, pc, ret_decay_f[0], ret_decay_b[0], ret_gn_f[0], ret_gn_b[0])
    x1, hx, aff = _outproj_router(y, ret_w_out[0], x, mod, 0, lat_row, norm2_g[0], moe_router[0], tm=512)
    c1, hc, affc = _outproj_router(yc, ret_w_out[0], ctx, mod, 0, ctx_row, norm2_g[0], moe_router[0], tm=512)
    (ys, slot_t, wsel_t), (ysc, slotc_t, wselc_t) = _moe(
        [(hx, aff), (hc, affc)], moe_w_gate, moe_w_up, moe_w_down, 0)
    x2 = _combine(ys, slot_t, wsel_t, x1, mod, 0, lat_row)
    c2 = _combine(ysc, slotc_t, wselc_t, c1, mod, 0, ctx_row)

    qw = SWA_HEAD_DIM * (swa_w_qkv.shape[2] // SWA_HEAD_DIM - 2 * SWA_KV_HEADS)
    cos, sin_minus, sin_plus = _rope_tables(t)
    rope_groups = (qw + SWA_KV_HEADS * SWA_HEAD_DIM) // LANES
    p = _norm_mod_matmul(x2, norm1_g[1], mod, 1, lat_row, swa_w_qkv[0], tm=512,
                         tn=swa_w_qkv.shape[2], rope=(cos, sin_minus, sin_plus, rope_groups))
    w_kv = swa_w_qkv[0][:, qw:]
    pc = _norm_mod_matmul(c2, norm1_g[1], mod, 1, ctx_row, w_kv, tm=512, tn=w_kv.shape[1])
    o = _window_attention(p, pc, swa_sink[0])
    x3, hx, aff = _outproj_router(o, swa_w_out[0], x2, mod, 1, lat_row, norm2_g[1], moe_router[1], tm=512)
    ((ys, slot_t, wsel_t),) = _moe([(hx, aff)], moe_w_gate, moe_w_up, moe_w_down, 1)
    return _combine(ys, slot_t, wsel_t, x3, mod, 1, lat_row, final_g)
```

```python
import functools

import jax
import jax.numpy as jnp
from jax import lax
from jax.experimental import pallas as pl
from jax.experimental.pallas import tpu as pltpu

F32 = jnp.float32
BF16 = jnp.bfloat16

NORM_EPS = 1e-6
GRID_W = 64
RET_HEADS = 4
RET_CHUNK = 256
SWA_HEAD_DIM = 64
SWA_KV_HEADS = 2
SWA_BLOCK = 128
ROPE_BASE = 10000.0
N_EXPERTS = 16
EC_CAPACITY = 2

LANES = 128
MIB = 1024 * 1024
COND_ROWS = 16

NT_DIMS = (((1,), (1,)), ((), ()))
TN_DIMS = (((0,), (0,)), ((), ()))


def _cparams(n_axes, vmem_mib):
    return pltpu.CompilerParams(
        dimension_semantics=("arbitrary",) * n_axes,
        vmem_limit_bytes=vmem_mib * MIB,
    )


def _silu(v):
    return v / (1.0 + jnp.exp(-v))


def _rmsnorm_mod(xv, g, shift, scale):
    h = xv * lax.rsqrt(jnp.mean(xv * xv, axis=-1, keepdims=True) + NORM_EPS) * g
    return h * (1.0 + scale) + shift


def _mod_spec(layer, chunk, d, row_of_batch, n_grid_axes, batch_axis):
    def index_map(*ids):
        return (layer, row_of_batch(ids[batch_axis]), 0, chunk)
    del n_grid_axes
    return pl.BlockSpec((None, None, 1, d), index_map)


def _adaln_body(cond_ref, w_ref, b_ref, o_ref):
    s = _silu(cond_ref[...])
    o_ref[...] = jnp.dot(s.astype(BF16), w_ref[...].astype(BF16),
                         preferred_element_type=F32) + b_ref[...]


def _adaln(cond, ada_w, ada_b):
    layers, d, n = ada_w.shape
    tn = d
    out = pl.pallas_call(
        _adaln_body,
        grid=(layers, n // tn),
        in_specs=[
            pl.BlockSpec((COND_ROWS, d), lambda l, j: (0, 0)),
            pl.BlockSpec((None, d, tn), lambda l, j: (l, 0, j)),
            pl.BlockSpec((None, 1, tn), lambda l, j: (l, 0, j)),
        ],
        out_specs=pl.BlockSpec((None, COND_ROWS, tn), lambda l, j: (l, 0, j)),
        out_shape=jax.ShapeDtypeStruct((layers, COND_ROWS, n), F32),
        compiler_params=_cparams(2, 32),
        name="adaln",
    )(cond, ada_w, ada_b.reshape(layers, 1, n))
    return out.reshape(layers, COND_ROWS, 1, n)


def _nmm_body(x_ref, g_ref, sh_ref, sc_ref, w_ref, *rest, rope_groups, silu_from_tile):
    if rope_groups:
        cos_ref, sm_ref, sp_ref, o_ref, wbf_ref = rest
    else:
        o_ref, wbf_ref = rest

    @pl.when((pl.program_id(1) == 0) & (pl.program_id(2) == 0))
    def _():
        wbf_ref[...] = w_ref[...].astype(BF16)

    def project():
        h = _rmsnorm_mod(x_ref[...], g_ref[...], sh_ref[...], sc_ref[...])
        return jnp.dot(h.astype(BF16), wbf_ref[...], preferred_element_type=F32)

    if silu_from_tile is not None:
        @pl.when(pl.program_id(0) < silu_from_tile)
        def _():
            o_ref[...] = project().astype(o_ref.dtype)

        @pl.when(pl.program_id(0) >= silu_from_tile)
        def _():
            o_ref[...] = _silu(project()).astype(o_ref.dtype)

        return
    acc = project()
    if not rope_groups:
        o_ref[...] = acc.astype(o_ref.dtype)
        return
    c, sm, sp = cos_ref[...], sm_ref[...], sp_ref[...]
    n_groups = acc.shape[1] // LANES
    for j in range(n_groups):
        a = acc[:, j * LANES:(j + 1) * LANES]
        if j < rope_groups:
            a = (a * c + pltpu.roll(a, LANES - 16, axis=1) * sm
                 + pltpu.roll(a, 16, axis=1) * sp)
        o_ref[:, j * LANES:(j + 1) * LANES] = a.astype(o_ref.dtype)


def _norm_mod_matmul(x, g, mod, layer, row_of_batch, w, *, tm, tn, rope=None, silu_from_tile=None):
    b, t, d = x.shape
    n = w.shape[1]
    tm = min(tm, t)
    grid = (n // tn, b, t // tm)
    in_specs = [
        pl.BlockSpec((None, tm, d), lambda j, bi, ti: (bi, ti, 0)),
        pl.BlockSpec((1, d), lambda j, bi, ti: (0, 0)),
        _mod_spec(layer, 0, d, row_of_batch, 3, 1),
        _mod_spec(layer, 1, d, row_of_batch, 3, 1),
        pl.BlockSpec((d, tn), lambda j, bi, ti: (0, j)),
    ]
    args = [x, g.reshape(1, d), mod, mod, w]
    rope_groups = 0
    if rope is not None:
        cos, sm, sp, rope_groups = rope
        for tab in (cos, sm, sp):
            in_specs.append(pl.BlockSpec((tm, LANES), lambda j, bi, ti: (ti, 0)))
            args.append(tab)
    return pl.pallas_call(
        functools.partial(_nmm_body, rope_groups=rope_groups, silu_from_tile=silu_from_tile),
        grid=grid,
        in_specs=in_specs,
        out_specs=pl.BlockSpec((None, tm, tn), lambda j, bi, ti: (bi, ti, j)),
        out_shape=jax.ShapeDtypeStruct((b, t, n), BF16),
        scratch_shapes=[pltpu.VMEM((d, tn), BF16)],
        compiler_params=_cparams(3, 48),
        name="norm_mod_proj",
    )(*args)


def _ret_body(dec_ref, gnf_ref, gnb_ref,
              qc_ref, kc_ref, vc_ref, gfc_ref, gbc_ref,
              q_ref, k_ref, v_ref, gf_ref, gb_ref,
              y_ref, yc_ref,
              statef_ref, stateb_ref, yacc_ref, intra_ref, qd_ref, kd_ref, cd_ref):
    c = RET_CHUNK
    dk = q_ref.shape[1]
    dv = v_ref.shape[1]
    t_ctx = qc_ref.shape[0]
    t_lat = q_ref.shape[0]

    dec = dec_ref[...]
    lg = jnp.minimum(dec, 0.0) - jnp.log1p(jnp.exp(-jnp.abs(dec)))
    row = lax.broadcasted_iota(jnp.int32, (c, c), 0)
    col = lax.broadcasted_iota(jnp.int32, (c, c), 1)
    pos_v = lax.broadcasted_iota(jnp.int32, (c, dv), 0).astype(F32)
    pos_k = lax.broadcasted_iota(jnp.int32, (c, dk), 0).astype(F32)
    for d in range(2):
        lg_c, lg_v, lg_k = lg[d:d + 1, :c], lg[d:d + 1, :], lg[d:d + 1, :dk]
        diff = (row - col) if d == 0 else (col - row)
        intra_ref[d] = jnp.where(diff >= 0, jnp.exp(jnp.maximum(diff, 0).astype(F32) * lg_c), 0.0)
        if d == 0:
            qd_ref[d] = jnp.exp((pos_v + 1.0) * lg_v)
            kd_ref[d] = jnp.exp((c - 1.0 - pos_k) * lg_k)
        else:
            qd_ref[d] = jnp.exp((c - pos_v) * lg_v)
            kd_ref[d] = jnp.exp(pos_k * lg_k)
        cd_ref[d] = jnp.exp(float(c) * lg_v)

    def chunk(d, refs, r0, yoff, out_ref, first_visit):
        qr, kr, vr, gr = refs
        state_ref = statef_ref if d == 0 else stateb_ref
        if not isinstance(r0, int):
            r0, yoff = pl.multiple_of(r0, c), pl.multiple_of(yoff, c)
        qs = qr[pl.ds(r0, c), :] * jnp.asarray(dk ** -0.5, BF16)
        kc = kr[pl.ds(r0, c), :]
        vc = vr[pl.ds(r0, c), :]
        s = lax.dot_general(qs, kc, NT_DIMS, preferred_element_type=F32) * intra_ref[d]
        st = state_ref[...]
        o = (jnp.dot(s.astype(BF16), vc, preferred_element_type=F32)
             + jnp.dot(qs, st.astype(BF16), preferred_element_type=F32) * qd_ref[d])
        kk = (kc.astype(F32) * kd_ref[d]).astype(BF16)
        state_ref[...] = cd_ref[d] * st + lax.dot_general(kk, vc, TN_DIMS, preferred_element_type=F32)
        mu = jnp.mean(o, axis=-1, keepdims=True)
        oc = o - mu
        var = jnp.mean(oc * oc, axis=-1, keepdims=True)
        gn = gnf_ref[...] if d == 0 else gnb_ref[...]
        y = gr[pl.ds(r0, c), :].astype(F32) * (oc * lax.rsqrt(var + NORM_EPS) * gn)
        if first_visit:
            yacc_ref[pl.ds(yoff, c), :] = y
        else:
            out_ref[pl.ds(r0, c), :] = (yacc_ref[pl.ds(yoff, c), :] + y).astype(out_ref.dtype)

    statef_ref[...] = jnp.zeros_like(statef_ref)
    stateb_ref[...] = jnp.zeros_like(stateb_ref)
    segments = (
        ((qc_ref, kc_ref, vc_ref), (gfc_ref, gbc_ref), t_ctx // c, 0, yc_ref),
        ((q_ref, k_ref, v_ref), (gf_ref, gb_ref), t_lat // c, t_ctx, y_ref),
    )
    for qkv, gates, n, ybase, out_ref in segments:
        half = n // 2

        def step(i, carry, first_visit, qkv=qkv, gates=gates, n=n, ybase=ybase, out_ref=out_ref):
            for d in range(2):
                ci = i if d == 0 else n - 1 - i
                chunk(d, qkv + (gates[d],), ci * c, ybase + ci * c, out_ref, first_visit)
            return carry

        unroll = 4 if half >= 4 else 1
        if half:
            lax.fori_loop(0, half, functools.partial(step, first_visit=True), 0, unroll=unroll)
        if n % 2:
            for d in range(2):
                chunk(d, qkv + (gates[d],), half * c, ybase + half * c, out_ref, d == 0)
        if half:
            lax.fori_loop(n - half, n, functools.partial(step, first_visit=False), 0, unroll=unroll)


def _retention(p, pc, decay_f, decay_b, gn_f, gn_b):
    b, t, cols = p.shape
    t_ctx = pc.shape[1]
    assert t % RET_CHUNK == 0 and t_ctx % RET_CHUNK == 0
    h = RET_HEADS
    dk = cols // (8 * h)
    dv = 2 * dk
    hv = h * dv
    dec = jnp.broadcast_to(jnp.stack([decay_f, decay_b], axis=1)[:, :, None], (h, 2, dv)).astype(F32)
    k_blk, v_blk, gf_blk, gb_blk = h, (2 * h * dk) // dv, (2 * h * dk + hv) // dv, (2 * h * dk + 2 * hv) // dv

    def specs(rows):
        return [
            pl.BlockSpec((None, rows, dk), lambda bi, hi: (bi, 0, hi)),
            pl.BlockSpec((None, rows, dk), lambda bi, hi: (bi, 0, k_blk + hi)),
            pl.BlockSpec((None, rows, dv), lambda bi, hi: (bi, 0, v_blk + hi)),
            pl.BlockSpec((None, rows, dv), lambda bi, hi: (bi, 0, gf_blk + hi)),
            pl.BlockSpec((None, rows, dv), lambda bi, hi: (bi, 0, gb_blk + hi)),
        ]

    gn_spec = pl.BlockSpec((1, dv), lambda bi, hi: (0, hi))
    return pl.pallas_call(
        _ret_body,
        grid=(b, h),
        in_specs=[pl.BlockSpec((None, 2, dv), lambda bi, hi: (hi, 0, 0)), gn_spec, gn_spec]
        + specs(t_ctx) + specs(t),
        out_specs=[
            pl.BlockSpec((None, t, dv), lambda bi, hi: (bi, 0, hi)),
            pl.BlockSpec((None, t_ctx, dv), lambda bi, hi: (bi, 0, hi)),
        ],
        out_shape=[
            jax.ShapeDtypeStruct((b, t, hv), BF16),
            jax.ShapeDtypeStruct((b, t_ctx, hv), BF16),
        ],
        scratch_shapes=[
            pltpu.VMEM((dk, dv), F32),
            pltpu.VMEM((dk, dv), F32),
            pltpu.VMEM((t_ctx + t, dv), F32),
            pltpu.VMEM((2, RET_CHUNK, RET_CHUNK), F32),
            pltpu.VMEM((2, RET_CHUNK, dv), F32),
            pltpu.VMEM((2, RET_CHUNK, dk), F32),
            pltpu.VMEM((2, 1, dv), F32),
        ],
        compiler_params=_cparams(2, 48),
        name="retention",
    )(dec, gn_f.reshape(1, hv), gn_b.reshape(1, hv), pc, pc, pc, pc, pc, p, p, p, p, p)


def _outproj_body(y_ref, w_ref, x_ref, g1_ref, ng_ref, sh_ref, sc_ref, wr_ref,
                  x1_ref, hx_ref, aff_ref, wbf_ref):
    @pl.when((pl.program_id(0) == 0) & (pl.program_id(1) == 0))
    def _():
        wbf_ref[...] = w_ref[...].astype(BF16)

    x1 = x_ref[...] + g1_ref[...] * jnp.dot(y_ref[...], wbf_ref[...], preferred_element_type=F32)
    x1_ref[...] = x1
    h = _rmsnorm_mod(x1, ng_ref[...], sh_ref[...], sc_ref[...])
    hb = h.astype(BF16)
    hx_ref[...] = hb
    hlo = (h - hb.astype(F32)).astype(BF16)
    wr = wr_ref[...]
    whi = wr.astype(BF16)
    wlo = (wr - whi.astype(F32)).astype(BF16)
    logits = (lax.dot_general(whi, hb, NT_DIMS, preferred_element_type=F32)
              + lax.dot_general(whi, hlo, NT_DIMS, preferred_element_type=F32)
              + lax.dot_general(wlo, hb, NT_DIMS, preferred_element_type=F32)
              + lax.dot_general(wlo, hlo, NT_DIMS, preferred_element_type=F32))
    e = jnp.exp(logits - jnp.max(logits, axis=0, keepdims=True))
    aff_ref[...] = e / jnp.sum(e, axis=0, keepdims=True)


def _outproj_router(y, w, x, mod, layer, row_of_batch, norm_g, w_router, *, tm):
    b, t, kdim = y.shape
    d = w.shape[1]
    e = w_router.shape[1]
    tm = min(tm, t)
    row = lambda bi, ti: (0, 0)
    return pl.pallas_call(
        _outproj_body,
        grid=(b, t // tm),
        in_specs=[
            pl.BlockSpec((None, tm, kdim), lambda bi, ti: (bi, ti, 0)),
            pl.BlockSpec((kdim, d), row),
            pl.BlockSpec((None, tm, d), lambda bi, ti: (bi, ti, 0)),
            _mod_spec(layer, 2, d, row_of_batch, 2, 0),
            pl.BlockSpec((1, d), row),
            _mod_spec(layer, 3, d, row_of_batch, 2, 0),
            _mod_spec(layer, 4, d, row_of_batch, 2, 0),
            pl.BlockSpec((e, d), row),
        ],
        out_specs=[
            pl.BlockSpec((None, tm, d), lambda bi, ti: (bi, ti, 0)),
            pl.BlockSpec((None, tm, d), lambda bi, ti: (bi, ti, 0)),
            pl.BlockSpec((None, e, tm), lambda bi, ti: (bi, 0, ti)),
        ],
        out_shape=[
            jax.ShapeDtypeStruct((b, t, d), F32),
            jax.ShapeDtypeStruct((b, t, d), BF16),
            jax.ShapeDtypeStruct((b, e, t), F32),
        ],
        scratch_shapes=[pltpu.VMEM((kdim, d), BF16)],
        compiler_params=_cparams(2, 48),
        name="outproj_router",
    )(y, w, x, mod, norm_g.reshape(1, d), mod, mod, w_router.T)


def _topk_body(aff_ref, slot_ref, wsel_ref, *, cap):
    a = aff_ref[...]
    r, t = a.shape
    capf = float(cap)

    def count_ge(v):
        return jnp.sum(jnp.where(a >= v, 1.0, 0.0), axis=1, keepdims=True)

    def bisect(_, lohi):
        lo, hi = lohi
        mid = 0.5 * (lo + hi)
        ok = count_ge(mid) >= capf
        return jnp.where(ok, mid, lo), jnp.where(ok, hi, mid)

    lo, hi = lax.fori_loop(0, 32, bisect, (jnp.zeros((r, 1), F32), jnp.full((r, 1), 2.0, F32)))

    def unfinished(state):
        return jnp.min(state[2]) < 0.5

    def walk(state):
        lo, hi, done = state
        top = jnp.max(jnp.where(a < hi, a, -1.0), axis=1, keepdims=True)
        found = (count_ge(top) >= capf) & (done < 0.5)
        moved = (done < 0.5) & jnp.logical_not(found)
        return jnp.where(found, top, lo), jnp.where(moved, top, hi), jnp.where(found, 1.0, done)

    thr, _, _ = lax.while_loop(unfinished, walk, (lo, hi, jnp.zeros((r, 1), F32)))

    gt = a > thr
    eq = a == thr
    need = capf - jnp.sum(jnp.where(gt, 1.0, 0.0), axis=1, keepdims=True)
    upper = jnp.where(lax.broadcasted_iota(jnp.int32, (LANES, LANES), 0)
                      < lax.broadcasted_iota(jnp.int32, (LANES, LANES), 1), 1.0, 0.0).astype(BF16)
    carry_eq = jnp.zeros((r, 1), F32)
    carry_sel = jnp.zeros((r, 1), F32)
    for j in range(t // LANES):
        sl = slice(j * LANES, (j + 1) * LANES)
        eqf = jnp.where(eq[:, sl], 1.0, 0.0)
        pre_eq = jnp.dot(eqf.astype(BF16), upper, preferred_element_type=F32) + carry_eq
        self_ = jnp.where(gt[:, sl], 1.0, jnp.where(pre_eq < need, eqf, 0.0))
        pre_sel = jnp.dot(self_.astype(BF16), upper, preferred_element_type=F32) + carry_sel
        sel = self_ > 0.5
        slot_ref[:, sl] = jnp.where(sel, pre_sel, -1.0).astype(jnp.int32)
        wsel_ref[:, sl] = jnp.where(sel, a[:, sl], 0.0)
        carry_eq = carry_eq + jnp.sum(eqf, axis=1, keepdims=True)
        carry_sel = carry_sel + jnp.sum(self_, axis=1, keepdims=True)


def _topk_select(aff):
    b, e, t = aff.shape
    cap = EC_CAPACITY * t // N_EXPERTS
    r = b * e
    full = pl.BlockSpec((r, t), lambda i: (0, 0))
    slot, wsel = pl.pallas_call(
        functools.partial(_topk_body, cap=cap),
        grid=(1,),
        in_specs=[full],
        out_specs=[full, full],
        out_shape=[jax.ShapeDtypeStruct((r, t), jnp.int32), jax.ShapeDtypeStruct((r, t), F32)],
        compiler_params=_cparams(1, 32),
        name="topk_select",
    )(aff.reshape(r, t))
    return slot.reshape(b, e, t), wsel.reshape(b, e, t)


def _gather_body(slot_ref, h_ref, o_ref, onehot_ref):
    n_exp, cap, d = o_ref.shape
    t = h_ref.shape[0]
    rank = lax.broadcasted_iota(jnp.int32, (cap, t), 0)
    for e in range(n_exp):
        onehot_ref[e * cap:(e + 1) * cap, :] = jnp.where(
            slot_ref[e:e + 1, :] == rank, 1.0, 0.0).astype(BF16)
    xs = jnp.dot(onehot_ref[...], h_ref[...], preferred_element_type=F32)
    o_ref[...] = xs.reshape(n_exp, cap, d).astype(o_ref.dtype)


def _gather(slot, hx, *, experts_per_step=8):
    b, e, t = slot.shape
    d = hx.shape[2]
    cap = EC_CAPACITY * t // N_EXPERTS
    eg = experts_per_step
    return pl.pallas_call(
        _gather_body,
        grid=(b, e // eg),
        in_specs=[
            pl.BlockSpec((None, eg, t), lambda bi, gi: (bi, gi, 0)),
            pl.BlockSpec((None, t, d), lambda bi, gi: (bi, 0, 0)),
        ],
        out_specs=pl.BlockSpec((eg, cap, d), lambda bi, gi: (gi, bi, 0)),
        out_shape=jax.ShapeDtypeStruct((e, b * cap, d), BF16),
        scratch_shapes=[pltpu.VMEM((eg * cap, t), BF16)],
        compiler_params=_cparams(2, 48),
        name="moe_gather",
    )(slot, hx)


def _ffn_body(*refs, n_streams, row_chunk):
    xs_refs = refs[:n_streams]
    wg_ref, wu_ref, wd_ref = refs[n_streams:n_streams + 3]
    ys_refs = refs[n_streams + 3:2 * n_streams + 3]
    acc_refs = refs[2 * n_streams + 3:3 * n_streams + 3]
    wgb_ref, wub_ref, wdb_ref = refs[3 * n_streams + 3:]
    f = pl.program_id(1)
    wgb_ref[...] = wg_ref[...].astype(BF16)
    wub_ref[...] = wu_ref[...].astype(BF16)
    wdb_ref[...] = wd_ref[...].astype(BF16)
    for xs_ref, ys_ref, acc_ref in zip(xs_refs, ys_refs, acc_refs):
        rc = min(row_chunk, xs_ref.shape[0])

        @pl.when(f == 0)
        def _(acc_ref=acc_ref):
            acc_ref[...] = jnp.zeros_like(acc_ref)

        def rows(i, carry, xs_ref=xs_ref, acc_ref=acc_ref, rc=rc):
            r0 = pl.multiple_of(i * rc, rc)
            xb = xs_ref[pl.ds(r0, rc), :]
            g = jnp.dot(xb, wgb_ref[...], preferred_element_type=F32)
            u = jnp.dot(xb, wub_ref[...], preferred_element_type=F32)
            hid = (_silu(g) * u).astype(BF16)
            acc_ref[pl.ds(r0, rc), :] += jnp.dot(hid, wdb_ref[...], preferred_element_type=F32)
            return carry

        lax.fori_loop(0, xs_ref.shape[0] // rc, rows, 0)

        @pl.when(f == pl.num_programs(1) - 1)
        def _(ys_ref=ys_ref, acc_ref=acc_ref):
            ys_ref[...] = acc_ref[...].astype(ys_ref.dtype)


def _expert_ffn(xs_list, w_gate, w_up, w_down, layer, *, tf=512, row_chunk=1024):
    _, e, d, ff = w_gate.shape
    n = len(xs_list)
    xs_specs = [pl.BlockSpec((None, xs.shape[1], d), lambda ei, fi: (ei, 0, 0)) for xs in xs_list]
    return pl.pallas_call(
        functools.partial(_ffn_body, n_streams=n, row_chunk=row_chunk),
        grid=(e, ff // tf),
        in_specs=xs_specs + [
            pl.BlockSpec((None, None, d, tf), lambda ei, fi: (layer, ei, 0, fi)),
            pl.BlockSpec((None, None, d, tf), lambda ei, fi: (layer, ei, 0, fi)),
            pl.BlockSpec((None, None, tf, d), lambda ei, fi: (layer, ei, fi, 0)),
        ],
        out_specs=xs_specs,
        out_shape=[jax.ShapeDtypeStruct(xs.shape, BF16) for xs in xs_list],
        scratch_shapes=[pltpu.VMEM(xs.shape[1:], F32) for xs in xs_list]
        + [pltpu.VMEM((d, tf), BF16), pltpu.VMEM((d, tf), BF16), pltpu.VMEM((tf, d), BF16)],
        compiler_params=_cparams(2, 56),
        name="expert_ffn",
    )(*xs_list, w_gate, w_up, w_down)


def _combine_body(ys_ref, slot_ref, wsel_ref, x_ref, g2_ref, *rest, final_norm):
    if final_norm:
        fg_ref, o_ref = rest
    else:
        (o_ref,) = rest
    n_exp, cap, _ = ys_ref.shape
    tm = x_ref.shape[0]
    rank = lax.broadcasted_iota(jnp.int32, (tm, cap), 1)
    slots = slot_ref[...]
    wsel = wsel_ref[...]
    o_ref[...] = jnp.zeros_like(o_ref)
    for e in range(n_exp):
        onehot = jnp.where(slots[:, e:e + 1] == rank, 1.0, 0.0).astype(BF16)
        o_ref[...] += wsel[:, e:e + 1] * jnp.dot(onehot, ys_ref[e], preferred_element_type=F32)
    out = x_ref[...] + g2_ref[...] * o_ref[...]
    if final_norm:
        out = out * lax.rsqrt(jnp.mean(out * out, axis=-1, keepdims=True) + NORM_EPS) * fg_ref[...]
    o_ref[...] = out


def _combine(ys, slot_t, wsel_t, x, mod, layer, row_of_batch, final_g=None, *, tm=512):
    b, t, d = x.shape
    e = ys.shape[0]
    cap = ys.shape[1] // b
    tm = min(tm, t)
    in_specs = [
        pl.BlockSpec((e, cap, d), lambda bi, ti: (0, bi, 0)),
        pl.BlockSpec((None, tm, e), lambda bi, ti: (bi, ti, 0)),
        pl.BlockSpec((None, tm, e), lambda bi, ti: (bi, ti, 0)),
        pl.BlockSpec((None, tm, d), lambda bi, ti: (bi, ti, 0)),
        _mod_spec(layer, 5, d, row_of_batch, 2, 0),
    ]
    args = [ys, slot_t, wsel_t, x, mod]
    if final_g is not None:
        in_specs.append(pl.BlockSpec((1, d), lambda bi, ti: (0, 0)))
        args.append(final_g.reshape(1, d))
    return pl.pallas_call(
        functools.partial(_combine_body, final_norm=final_g is not None),
        grid=(b, t // tm),
        in_specs=in_specs,
        out_specs=pl.BlockSpec((None, tm, d), lambda bi, ti: (bi, ti, 0)),
        out_shape=jax.ShapeDtypeStruct((b, t, d), F32),
        compiler_params=_cparams(2, 48),
        name="moe_combine",
    )(*args)


def _attn_body(sink_ref, q_ref, k_ref, v_ref, kc_ref, vc_ref, o_ref, kband_ref, vband_ref):
    blk = SWA_BLOCK
    hd = SWA_HEAD_DIM
    t_ctx = kc_ref.shape[0]
    n_heads = q_ref.shape[1] // hd
    group = n_heads // SWA_KV_HEADS
    n = pl.program_id(1)
    nb = pl.num_programs(1)
    starts = (jnp.maximum(n - 1, 0), n, jnp.minimum(n + 1, nb - 1))
    kband_ref[0:t_ctx, :] = kc_ref[...]
    vband_ref[0:t_ctx, :] = vc_ref[...]
    for i, st in enumerate(starts):
        r0 = pl.multiple_of(st * blk, blk)
        kband_ref[t_ctx + i * blk:t_ctx + (i + 1) * blk, :] = k_ref[pl.ds(r0, blk), :]
        vband_ref[t_ctx + i * blk:t_ctx + (i + 1) * blk, :] = v_ref[pl.ds(r0, blk), :]
    qi = lax.broadcasted_iota(jnp.int32, (blk, blk), 0)
    kj = lax.broadcasted_iota(jnp.int32, (blk, blk), 1)
    neg = -jnp.inf
    bias_prev = jnp.where(kj >= qi, 0.0, neg) + jnp.where(n > 0, 0.0, neg)
    bias_next = jnp.where(kj <= qi, 0.0, neg) + jnp.where(n < nb - 1, 0.0, neg)
    bias = jnp.concatenate(
        [jnp.zeros((blk, t_ctx), F32), bias_prev, jnp.zeros((blk, blk), F32), bias_next], axis=1)
    band = bias.shape[1]
    head_idx = lax.broadcasted_iota(jnp.int32, (group, 1, 1), 0)
    for kv in range(SWA_KV_HEADS):
        heads = range(kv * group, (kv + 1) * group)
        qg = jnp.concatenate([q_ref[:, h * hd:(h + 1) * hd] for h in heads], axis=0)
        qg = qg * jnp.asarray(hd ** -0.5, BF16)
        kh = kband_ref[:, kv * hd:(kv + 1) * hd]
        vh = vband_ref[:, kv * hd:(kv + 1) * hd]
        s = lax.dot_general(qg, kh, NT_DIMS, preferred_element_type=F32)
        s = s.reshape(group, blk, band) + bias[None]
        sink = jnp.zeros((group, 1, 1), F32)
        for i, h in enumerate(heads):
            sink = jnp.where(head_idx == i, sink_ref[h], sink)
        m = jnp.maximum(jnp.max(s, axis=-1, keepdims=True), sink)
        p = jnp.exp(s - m)
        den = jnp.sum(p, axis=-1, keepdims=True) + jnp.exp(sink - m)
        o = jnp.dot(p.reshape(group * blk, band).astype(BF16), vh, preferred_element_type=F32)
        o = o.reshape(group, blk, hd) / den
        for i, h in enumerate(heads):
            o_ref[:, h * hd:(h + 1) * hd] = o[i].astype(o_ref.dtype)


def _window_attention(p, pc, sink):
    b, t, cols = p.shape
    t_ctx = pc.shape[1]
    kvw = SWA_KV_HEADS * SWA_HEAD_DIM
    qw = cols - 2 * kvw
    band = t_ctx + 3 * SWA_BLOCK
    return pl.pallas_call(
        _attn_body,
        grid=(b, t // SWA_BLOCK),
        in_specs=[
            pl.BlockSpec(memory_space=pltpu.SMEM),
            pl.BlockSpec((None, SWA_BLOCK, qw), lambda bi, ni: (bi, ni, 0)),
            pl.BlockSpec((None, t, kvw), lambda bi, ni: (bi, 0, qw // kvw)),
            pl.BlockSpec((None, t, kvw), lambda bi, ni: (bi, 0, qw // kvw + 1)),
            pl.BlockSpec((None, t_ctx, kvw), lambda bi, ni: (bi, 0, 0)),
            pl.BlockSpec((None, t_ctx, kvw), lambda bi, ni: (bi, 0, 1)),
        ],
        out_specs=pl.BlockSpec((None, SWA_BLOCK, qw), lambda bi, ni: (bi, ni, 0)),
        out_shape=jax.ShapeDtypeStruct((b, t, qw), BF16),
        scratch_shapes=[pltpu.VMEM((band, kvw), BF16), pltpu.VMEM((band, kvw), BF16)],
        compiler_params=_cparams(2, 32),
        name="window_attention",
    )(sink.astype(F32), p, p, p, pc, pc)


def _rope_tables(t):
    n_freq = SWA_HEAD_DIM // 4
    pos = jnp.arange(t)
    row = (pos // GRID_W).astype(F32)
    col = (pos % GRID_W).astype(F32)
    inv_freq = ROPE_BASE ** (-jnp.arange(n_freq, dtype=F32) / n_freq)
    ang_r, ang_c = row[:, None] * inv_freq, col[:, None] * inv_freq
    zero = jnp.zeros_like(ang_r)
    cos = jnp.concatenate([jnp.cos(ang_r)] * 2 + [jnp.cos(ang_c)] * 2, axis=1)
    sin_minus = jnp.concatenate([-jnp.sin(ang_r), zero, -jnp.sin(ang_c), zero], axis=1)
    sin_plus = jnp.concatenate([zero, jnp.sin(ang_r), zero, jnp.sin(ang_c)], axis=1)
    reps = LANES // SWA_HEAD_DIM
    return tuple(jnp.tile(tab, (1, reps)) for tab in (cos, sin_minus, sin_plus))


def _moe(streams, w_gate, w_up, w_down, layer):
    routed = []
    for hx, aff in streams:
        slot, wsel = _topk_select(aff)
        routed.append((_gather(slot, hx), jnp.swapaxes(slot, 1, 2), jnp.swapaxes(wsel, 1, 2)))
    ys_list = _expert_ffn([r[0] for r in routed], w_gate, w_up, w_down, layer)
    return [(ys, r[1], r[2]) for ys, r in zip(ys_list, routed)]


def kernel(x, c, ctx, c_ctx, ada_w, ada_b, norm1_g, norm2_g, final_g, ret_w_in, ret_decay_f, ret_decay_b, ret_gn_f, ret_gn_b, ret_w_out, swa_w_qkv, swa_sink, swa_w_out, moe_router, moe_w_gate, moe_w_up, moe_w_down):
    b, t, d = x.shape
    lat_row = lambda bi: bi
    ctx_row = lambda bi: b
    cond = jnp.zeros((COND_ROWS, d), F32).at[:b].set(c).at[b].set(c_ctx)
    mod = _adaln(cond, ada_w, ada_b)

    ret_tn = 2048
    gate_col0 = (ret_w_in.shape[2] // 8) * 4
    assert gate_col0 % ret_tn == 0
    ret_proj = dict(tm=512, tn=ret_tn, silu_from_tile=gate_col0 // ret_tn)
    p = _norm_mod_matmul(x, norm1_g[0], mod, 0, lat_row, ret_w_in[0], **ret_proj)
    pc = _norm_mod_matmul(ctx, norm1_g[0], mod, 0, ctx_row, ret_w_in[0], **ret_proj)
    y, yc = _retention(p, pc, ret_decay_f[0], ret_decay_b[0], ret_gn_f[0], ret_gn_b[0])
    x1, hx, aff = _outproj_router(y, ret_w_out[0], x, mod, 0, lat_row, norm2_g[0], moe_router[0], tm=512)
    c1, hc, affc = _outproj_router(yc, ret_w_out[0], ctx, mod, 0, ctx_row, norm2_g[0], moe_router[0], tm=512)
    (ys, slot_t, wsel_t), (ysc, slotc_t, wselc_t) = _moe(
        [(hx, aff), (hc, affc)], moe_w_gate, moe_w_up, moe_w_down, 0)
    x2 = _combine(ys, slot_t, wsel_t, x1, mod, 0, lat_row)
    c2 = _combine(ysc, slotc_t, wselc_t, c1, mod, 0, ctx_row)

    qw = SWA_HEAD_DIM * (swa_w_qkv.shape[2] // SWA_HEAD_DIM - 2 * SWA_KV_HEADS)
    cos, sin_minus, sin_plus = _rope_tables(t)
    rope_groups = (qw + SWA_KV_HEADS * SWA_HEAD_DIM) // LANES
    p = _norm_mod_matmul(x2, norm1_g[1], mod, 1, lat_row, swa_w_qkv[0], tm=512,
                         tn=swa_w_qkv.shape[2], rope=(cos, sin_minus, sin_plus, rope_groups))
    w_kv = swa_w_qkv[0][:, qw:]
    pc = _norm_mod_matmul(c2, norm1_g[1], mod, 1, ctx_row, w_kv, tm=512, tn=w_kv.shape[1])
    o = _window_attention(p, pc, swa_sink[0])
    x3, hx, aff = _outproj_router(o, swa_w_out[0], x2, mod, 1, lat_row, norm2_g[1], moe_router[1], tm=512)
    ((ys, slot_t, wsel_t),) = _moe([(hx, aff)], moe_w_gate, moe_w_up, moe_w_down, 1)
    return _combine(ys, slot_t, wsel_t, x3, mod, 1, lat_row, final_g)
```

```python
import functools

import jax
import jax.numpy as jnp
from jax import lax
from jax.experimental import pallas as pl
from jax.experimental.pallas import tpu as pltpu

F32 = jnp.float32
BF16 = jnp.bfloat16

NORM_EPS = 1e-6
GRID_W = 64
RET_HEADS = 4
RET_CHUNK = 256
SWA_HEAD_DIM = 64
SWA_KV_HEADS = 2
SWA_BLOCK = 128
ROPE_BASE = 10000.0
N_EXPERTS = 16
EC_CAPACITY = 2

MOE_TILE = 256
MOE_WINDOW = 64

LANES = 128
BF16_ROWS = 16
MIB = 1024 * 1024
COND_ROWS = 16

NT_DIMS = (((1,), (1,)), ((), ()))
TN_DIMS = (((0,), (0,)), ((), ()))


def _cparams(n_axes, vmem_mib):
    return pltpu.CompilerParams(
        dimension_semantics=("arbitrary",) * n_axes,
        vmem_limit_bytes=vmem_mib * MIB,
    )


def _silu(v):
    return v / (1.0 + jnp.exp(-v))


def _rmsnorm_mod(xv, g, shift, scale):
    h = xv * lax.rsqrt(jnp.mean(xv * xv, axis=-1, keepdims=True) + NORM_EPS) * g
    return h * (1.0 + scale) + shift


def _mod_spec(layer, chunk, d, row_of_batch, n_grid_axes, batch_axis):
    def index_map(*ids):
        return (layer, row_of_batch(ids[batch_axis]), 0, chunk)
    del n_grid_axes
    return pl.BlockSpec((None, None, 1, d), index_map)


def _adaln_body(cond_ref, w_ref, b_ref, o_ref):
    s = _silu(cond_ref[...])
    o_ref[...] = jnp.dot(s.astype(BF16), w_ref[...].astype(BF16),
                         preferred_element_type=F32) + b_ref[...]


def _adaln(cond, ada_w, ada_b):
    layers, d, n = ada_w.shape
    tn = d
    out = pl.pallas_call(
        _adaln_body,
        grid=(layers, n // tn),
        in_specs=[
            pl.BlockSpec((COND_ROWS, d), lambda l, j: (0, 0)),
            pl.BlockSpec((None, d, tn), lambda l, j: (l, 0, j)),
            pl.BlockSpec((None, 1, tn), lambda l, j: (l, 0, j)),
        ],
        out_specs=pl.BlockSpec((None, COND_ROWS, tn), lambda l, j: (l, 0, j)),
        out_shape=jax.ShapeDtypeStruct((layers, COND_ROWS, n), F32),
        compiler_params=_cparams(2, 32),
        name="adaln",
    )(cond, ada_w, ada_b.reshape(layers, 1, n))
    return out.reshape(layers, COND_ROWS, 1, n)


def _nmm_body(x_ref, g_ref, sh_ref, sc_ref, w_ref, *rest, rope_groups, silu_from_tile):
    if rope_groups:
        cos_ref, sm_ref, sp_ref, o_ref, wbf_ref = rest
    else:
        o_ref, wbf_ref = rest

    @pl.when((pl.program_id(1) == 0) & (pl.program_id(2) == 0))
    def _():
        wbf_ref[...] = w_ref[...].astype(BF16)

    def project():
        h = _rmsnorm_mod(x_ref[...], g_ref[...], sh_ref[...], sc_ref[...])
        return jnp.dot(h.astype(BF16), wbf_ref[...], preferred_element_type=F32)

    if silu_from_tile is not None:
        @pl.when(pl.program_id(0) < silu_from_tile)
        def _():
            o_ref[...] = project().astype(o_ref.dtype)

        @pl.when(pl.program_id(0) >= silu_from_tile)
        def _():
            o_ref[...] = _silu(project()).astype(o_ref.dtype)

        return
    acc = project()
    if not rope_groups:
        o_ref[...] = acc.astype(o_ref.dtype)
        return
    c, sm, sp = cos_ref[...], sm_ref[...], sp_ref[...]
    n_groups = acc.shape[1] // LANES
    for j in range(n_groups):
        a = acc[:, j * LANES:(j + 1) * LANES]
        if j < rope_groups:
            a = (a * c + pltpu.roll(a, LANES - 16, axis=1) * sm
                 + pltpu.roll(a, 16, axis=1) * sp)
        o_ref[:, j * LANES:(j + 1) * LANES] = a.astype(o_ref.dtype)


def _norm_mod_matmul(x, g, mod, layer, row_of_batch, w, *, tm, tn, rope=None, silu_from_tile=None):
    b, t, d = x.shape
    n = w.shape[1]
    tm = min(tm, t)
    grid = (n // tn, b, t // tm)
    in_specs = [
        pl.BlockSpec((None, tm, d), lambda j, bi, ti: (bi, ti, 0)),
        pl.BlockSpec((1, d), lambda j, bi, ti: (0, 0)),
        _mod_spec(layer, 0, d, row_of_batch, 3, 1),
        _mod_spec(layer, 1, d, row_of_batch, 3, 1),
        pl.BlockSpec((d, tn), lambda j, bi, ti: (0, j)),
    ]
    args = [x, g.reshape(1, d), mod, mod, w]
    rope_groups = 0
    if rope is not None:
        cos, sm, sp, rope_groups = rope
        for tab in (cos, sm, sp):
            in_specs.append(pl.BlockSpec((tm, LANES), lambda j, bi, ti: (ti, 0)))
            args.append(tab)
    return pl.pallas_call(
        functools.partial(_nmm_body, rope_groups=rope_groups, silu_from_tile=silu_from_tile),
        grid=grid,
        in_specs=in_specs,
        out_specs=pl.BlockSpec((None, tm, tn), lambda j, bi, ti: (bi, ti, j)),
        out_shape=jax.ShapeDtypeStruct((b, t, n), BF16),
        scratch_shapes=[pltpu.VMEM((d, tn), BF16)],
        compiler_params=_cparams(3, 48),
        name="norm_mod_proj",
    )(*args)


def _ret_body(dec_ref, gnf_ref, gnb_ref,
              qc_ref, kc_ref, vc_ref, gfc_ref, gbc_ref,
              q_ref, k_ref, v_ref, gf_ref, gb_ref,
              y_ref, yc_ref,
              statef_ref, stateb_ref, yacc_ref, intra_ref, qd_ref, kd_ref, cd_ref):
    c = RET_CHUNK
    dk = q_ref.shape[1]
    dv = v_ref.shape[1]
    t_ctx = qc_ref.shape[0]
    t_lat = q_ref.shape[0]

    dec = dec_ref[...]
    lg = jnp.minimum(dec, 0.0) - jnp.log1p(jnp.exp(-jnp.abs(dec)))
    row = lax.broadcasted_iota(jnp.int32, (c, c), 0)
    col = lax.broadcasted_iota(jnp.int32, (c, c), 1)
    pos_v = lax.broadcasted_iota(jnp.int32, (c, dv), 0).astype(F32)
    pos_k = lax.broadcasted_iota(jnp.int32, (c, dk), 0).astype(F32)
    for d in range(2):
        lg_c, lg_v, lg_k = lg[d:d + 1, :c], lg[d:d + 1, :], lg[d:d + 1, :dk]
        diff = (row - col) if d == 0 else (col - row)
        intra_ref[d] = jnp.where(diff >= 0, jnp.exp(jnp.maximum(diff, 0).astype(F32) * lg_c), 0.0)
        if d == 0:
            qd_ref[d] = jnp.exp((pos_v + 1.0) * lg_v)
            kd_ref[d] = jnp.exp((c - 1.0 - pos_k) * lg_k)
        else:
            qd_ref[d] = jnp.exp((c - pos_v) * lg_v)
            kd_ref[d] = jnp.exp(pos_k * lg_k)
        cd_ref[d] = jnp.exp(float(c) * lg_v)

    def chunk(d, refs, r0, yoff, out_ref, first_visit):
        qr, kr, vr, gr = refs
        state_ref = statef_ref if d == 0 else stateb_ref
        if not isinstance(r0, int):
            r0, yoff = pl.multiple_of(r0, c), pl.multiple_of(yoff, c)
        qs = qr[pl.ds(r0, c), :] * jnp.asarray(dk ** -0.5, BF16)
        kc = kr[pl.ds(r0, c), :]
        vc = vr[pl.ds(r0, c), :]
        s = lax.dot_general(qs, kc, NT_DIMS, preferred_element_type=F32) * intra_ref[d]
        st = state_ref[...]
        o = (jnp.dot(s.astype(BF16), vc, preferred_element_type=F32)
             + jnp.dot(qs, st.astype(BF16), preferred_element_type=F32) * qd_ref[d])
        kk = (kc.astype(F32) * kd_ref[d]).astype(BF16)
        state_ref[...] = cd_ref[d] * st + lax.dot_general(kk, vc, TN_DIMS, preferred_element_type=F32)
        mu = jnp.mean(o, axis=-1, keepdims=True)
        oc = o - mu
        var = jnp.mean(oc * oc, axis=-1, keepdims=True)
        gn = gnf_ref[...] if d == 0 else gnb_ref[...]
        y = gr[pl.ds(r0, c), :].astype(F32) * (oc * lax.rsqrt(var + NORM_EPS) * gn)
        if first_visit:
            yacc_ref[pl.ds(yoff, c), :] = y
        else:
            out_ref[pl.ds(r0, c), :] = (yacc_ref[pl.ds(yoff, c), :] + y).astype(out_ref.dtype)

    statef_ref[...] = jnp.zeros_like(statef_ref)
    stateb_ref[...] = jnp.zeros_like(stateb_ref)
    segments = (
        ((qc_ref, kc_ref, vc_ref), (gfc_ref, gbc_ref), t_ctx // c, 0, yc_ref),
        ((q_ref, k_ref, v_ref), (gf_ref, gb_ref), t_lat // c, t_ctx, y_ref),
    )
    for qkv, gates, n, ybase, out_ref in segments:
        half = n // 2

        def step(i, carry, first_visit, qkv=qkv, gates=gates, n=n, ybase=ybase, out_ref=out_ref):
            for d in range(2):
                ci = i if d == 0 else n - 1 - i
                chunk(d, qkv + (gates[d],), ci * c, ybase + ci * c, out_ref, first_visit)
            return carry

        unroll = 4 if half >= 4 else 1
        if half:
            lax.fori_loop(0, half, functools.partial(step, first_visit=True), 0, unroll=unroll)
        if n % 2:
            for d in range(2):
                chunk(d, qkv + (gates[d],), half * c, ybase + half * c, out_ref, d == 0)
        if half:
            lax.fori_loop(n - half, n, functools.partial(step, first_visit=False), 0, unroll=unroll)


def _retention(p, pc, decay_f, decay_b, gn_f, gn_b):
    b, t, cols = p.shape
    t_ctx = pc.shape[1]
    assert t % RET_CHUNK == 0 and t_ctx % RET_CHUNK == 0
    h = RET_HEADS
    dk = cols // (8 * h)
    dv = 2 * dk
    hv = h * dv
    dec = jnp.broadcast_to(jnp.stack([decay_f, decay_b], axis=1)[:, :, None], (h, 2, dv)).astype(F32)
    k_blk, v_blk, gf_blk, gb_blk = h, (2 * h * dk) // dv, (2 * h * dk + hv) // dv, (2 * h * dk + 2 * hv) // dv

    def specs(rows):
        return [
            pl.BlockSpec((None, rows, dk), lambda bi, hi: (bi, 0, hi)),
            pl.BlockSpec((None, rows, dk), lambda bi, hi: (bi, 0, k_blk + hi)),
            pl.BlockSpec((None, rows, dv), lambda bi, hi: (bi, 0, v_blk + hi)),
            pl.BlockSpec((None, rows, dv), lambda bi, hi: (bi, 0, gf_blk + hi)),
            pl.BlockSpec((None, rows, dv), lambda bi, hi: (bi, 0, gb_blk + hi)),
        ]

    gn_spec = pl.BlockSpec((1, dv), lambda bi, hi: (0, hi))
    return pl.pallas_call(
        _ret_body,
        grid=(b, h),
        in_specs=[pl.BlockSpec((None, 2, dv), lambda bi, hi: (hi, 0, 0)), gn_spec, gn_spec]
        + specs(t_ctx) + specs(t),
        out_specs=[
            pl.BlockSpec((None, t, dv), lambda bi, hi: (bi, 0, hi)),
            pl.BlockSpec((None, t_ctx, dv), lambda bi, hi: (bi, 0, hi)),
        ],
        out_shape=[
            jax.ShapeDtypeStruct((b, t, hv), BF16),
            jax.ShapeDtypeStruct((b, t_ctx, hv), BF16),
        ],
        scratch_shapes=[
            pltpu.VMEM((dk, dv), F32),
            pltpu.VMEM((dk, dv), F32),
            pltpu.VMEM((t_ctx + t, dv), F32),
            pltpu.VMEM((2, RET_CHUNK, RET_CHUNK), F32),
            pltpu.VMEM((2, RET_CHUNK, dv), F32),
            pltpu.VMEM((2, RET_CHUNK, dk), F32),
            pltpu.VMEM((2, 1, dv), F32),
        ],
        compiler_params=_cparams(2, 48),
        name="retention",
    )(dec, gn_f.reshape(1, hv), gn_b.reshape(1, hv), pc, pc, pc, pc, pc, p, p, p, p, p)


def _outproj_body(y_ref, w_ref, x_ref, g1_ref, ng_ref, sh_ref, sc_ref, wr_ref,
                  x1_ref, hx_ref, aff_ref, wbf_ref):
    @pl.when((pl.program_id(0) == 0) & (pl.program_id(1) == 0))
    def _():
        wbf_ref[...] = w_ref[...].astype(BF16)

    x1 = x_ref[...] + g1_ref[...] * jnp.dot(y_ref[...], wbf_ref[...], preferred_element_type=F32)
    x1_ref[...] = x1
    h = _rmsnorm_mod(x1, ng_ref[...], sh_ref[...], sc_ref[...])
    hb = h.astype(BF16)
    hx_ref[...] = hb
    hlo = (h - hb.astype(F32)).astype(BF16)
    wr = wr_ref[...]
    whi = wr.astype(BF16)
    wlo = (wr - whi.astype(F32)).astype(BF16)
    logits = (lax.dot_general(whi, hb, NT_DIMS, preferred_element_type=F32)
              + lax.dot_general(whi, hlo, NT_DIMS, preferred_element_type=F32)
              + lax.dot_general(wlo, hb, NT_DIMS, preferred_element_type=F32)
              + lax.dot_general(wlo, hlo, NT_DIMS, preferred_element_type=F32))
    e = jnp.exp(logits - jnp.max(logits, axis=0, keepdims=True))
    aff_ref[...] = e / jnp.sum(e, axis=0, keepdims=True)


def _outproj_router(y, w, x, mod, layer, row_of_batch, norm_g, w_router, *, tm):
    b, t, kdim = y.shape
    d = w.shape[1]
    e = w_router.shape[1]
    tm = min(tm, t)
    row = lambda bi, ti: (0, 0)
    return pl.pallas_call(
        _outproj_body,
        grid=(b, t // tm),
        in_specs=[
            pl.BlockSpec((None, tm, kdim), lambda bi, ti: (bi, ti, 0)),
            pl.BlockSpec((kdim, d), row),
            pl.BlockSpec((None, tm, d), lambda bi, ti: (bi, ti, 0)),
            _mod_spec(layer, 2, d, row_of_batch, 2, 0),
            pl.BlockSpec((1, d), row),
            _mod_spec(layer, 3, d, row_of_batch, 2, 0),
            _mod_spec(layer, 4, d, row_of_batch, 2, 0),
            pl.BlockSpec((e, d), row),
        ],
        out_specs=[
            pl.BlockSpec((None, tm, d), lambda bi, ti: (bi, ti, 0)),
            pl.BlockSpec((None, tm, d), lambda bi, ti: (bi, ti, 0)),
            pl.BlockSpec((None, e, tm), lambda bi, ti: (bi, 0, ti)),
        ],
        out_shape=[
            jax.ShapeDtypeStruct((b, t, d), F32),
            jax.ShapeDtypeStruct((b, t, d), BF16),
            jax.ShapeDtypeStruct((b, e, t), F32),
        ],
        scratch_shapes=[pltpu.VMEM((kdim, d), BF16)],
        compiler_params=_cparams(2, 48),
        name="outproj_router",
    )(y, w, x, mod, norm_g.reshape(1, d), mod, mod, w_router.T)


def _topk_body(aff_ref, slot_ref, wsel_ref, lo_ref, *, cap):
    a = aff_ref[...]
    r, t = a.shape
    capf = float(cap)

    def count_ge(v):
        return jnp.sum(jnp.where(a >= v, 1.0, 0.0), axis=1, keepdims=True)

    def bisect(_, lohi):
        lo, hi = lohi
        mid = 0.5 * (lo + hi)
        ok = count_ge(mid) >= capf
        return jnp.where(ok, mid, lo), jnp.where(ok, hi, mid)

    lo, hi = lax.fori_loop(0, 32, bisect, (jnp.zeros((r, 1), F32), jnp.full((r, 1), 2.0, F32)))

    def unfinished(state):
        return jnp.min(state[2]) < 0.5

    def walk(state):
        lo, hi, done = state
        top = jnp.max(jnp.where(a < hi, a, -1.0), axis=1, keepdims=True)
        found = (count_ge(top) >= capf) & (done < 0.5)
        moved = (done < 0.5) & jnp.logical_not(found)
        return jnp.where(found, top, lo), jnp.where(moved, top, hi), jnp.where(found, 1.0, done)

    thr, _, _ = lax.while_loop(unfinished, walk, (lo, hi, jnp.zeros((r, 1), F32)))

    gt = a > thr
    eq = a == thr
    need = capf - jnp.sum(jnp.where(gt, 1.0, 0.0), axis=1, keepdims=True)
    upper = jnp.where(lax.broadcasted_iota(jnp.int32, (LANES, LANES), 0)
                      < lax.broadcasted_iota(jnp.int32, (LANES, LANES), 1), 1.0, 0.0).astype(BF16)
    carry_eq = jnp.zeros((r, 1), F32)
    carry_sel = jnp.zeros((r, 1), F32)
    lane = lax.broadcasted_iota(jnp.int32, (r, LANES), 1)
    tile_lo = jnp.zeros((r, LANES), F32)
    for j in range(t // LANES):
        sl = slice(j * LANES, (j + 1) * LANES)
        if (j * LANES) % MOE_TILE == 0:
            tile_lo = jnp.where(lane == (j * LANES) // MOE_TILE, carry_sel, tile_lo)
        eqf = jnp.where(eq[:, sl], 1.0, 0.0)
        pre_eq = jnp.dot(eqf.astype(BF16), upper, preferred_element_type=F32) + carry_eq
        self_ = jnp.where(gt[:, sl], 1.0, jnp.where(pre_eq < need, eqf, 0.0))
        pre_sel = jnp.dot(self_.astype(BF16), upper, preferred_element_type=F32) + carry_sel
        sel = self_ > 0.5
        slot_ref[:, sl] = jnp.where(sel, pre_sel, -1.0).astype(jnp.int32)
        wsel_ref[:, sl] = jnp.where(sel, a[:, sl], 0.0)
        carry_eq = carry_eq + jnp.sum(eqf, axis=1, keepdims=True)
        carry_sel = carry_sel + jnp.sum(self_, axis=1, keepdims=True)
    tile_lo = jnp.where(lane == t // MOE_TILE, carry_sel, tile_lo)
    lo_ref[...] = tile_lo.astype(jnp.int32)


def _topk_select(aff):
    b, e, t = aff.shape
    cap = EC_CAPACITY * t // N_EXPERTS
    r = b * e
    n_tiles = t // MOE_TILE
    assert t % MOE_TILE == 0 and n_tiles < LANES
    full = pl.BlockSpec((r, t), lambda i: (0, 0))
    slot, wsel, lo = pl.pallas_call(
        functools.partial(_topk_body, cap=cap),
        grid=(1,),
        in_specs=[full],
        out_specs=[full, full, pl.BlockSpec((r, LANES), lambda i: (0, 0))],
        out_shape=[jax.ShapeDtypeStruct((r, t), jnp.int32), jax.ShapeDtypeStruct((r, t), F32),
                   jax.ShapeDtypeStruct((r, LANES), jnp.int32)],
        compiler_params=_cparams(1, 32),
        name="topk_select",
    )(aff.reshape(r, t))
    tile_lo = jnp.swapaxes(lo.reshape(b, e, LANES)[:, :, :n_tiles + 1], 1, 2).reshape(-1)
    return slot.reshape(b, e, t), wsel.reshape(b, e, t), tile_lo


def _gather_body(slot_ref, h_ref, o_ref, onehot_ref):
    n_exp, cap, d = o_ref.shape
    t = h_ref.shape[0]
    rank = lax.broadcasted_iota(jnp.int32, (cap, t), 0)
    for e in range(n_exp):
        onehot_ref[e * cap:(e + 1) * cap, :] = jnp.where(
            slot_ref[e:e + 1, :] == rank, 1.0, 0.0).astype(BF16)
    xs = jnp.dot(onehot_ref[...], h_ref[...], preferred_element_type=F32)
    o_ref[...] = xs.reshape(n_exp, cap, d).astype(o_ref.dtype)


def _gather(slot, hx, *, experts_per_step=8):
    b, e, t = slot.shape
    d = hx.shape[2]
    cap = EC_CAPACITY * t // N_EXPERTS
    eg = experts_per_step
    return pl.pallas_call(
        _gather_body,
        grid=(b, e // eg),
        in_specs=[
            pl.BlockSpec((None, eg, t), lambda bi, gi: (bi, gi, 0)),
            pl.BlockSpec((None, t, d), lambda bi, gi: (bi, 0, 0)),
        ],
        out_specs=pl.BlockSpec((eg, cap, d), lambda bi, gi: (gi, bi, 0)),
        out_shape=jax.ShapeDtypeStruct((e, b * cap, d), BF16),
        scratch_shapes=[pltpu.VMEM((eg * cap, t), BF16)],
        compiler_params=_cparams(2, 48),
        name="moe_gather",
    )(slot, hx)


def _window_starts(lo_ref, base, n_exp, cap):
    starts = []
    fits = None
    for e in range(n_exp):
        lo = lo_ref[base + e]
        hi = lo_ref[base + n_exp + e]
        start = jnp.minimum((lo // BF16_ROWS) * BF16_ROWS, cap - MOE_WINDOW)
        starts.append(pl.multiple_of(start, BF16_ROWS))
        ok = hi - start <= MOE_WINDOW
        fits = ok if fits is None else fits & ok
    return starts, fits


def _gather_win_body(lo_ref, slot_ref, h_ref, o_ref, onehot_ref):
    n_exp, cap, d = o_ref.shape
    tt = h_ref.shape[0]
    w = MOE_WINDOW
    i = pl.program_id(1)

    @pl.when(i == 0)
    def _():
        o_ref[...] = jnp.zeros_like(o_ref)

    base = (pl.program_id(0) * (pl.num_programs(1) + 1) + i) * n_exp
    starts, fits = _window_starts(lo_ref, base, n_exp, cap)

    @pl.when(fits)
    def _():
        rank = lax.broadcasted_iota(jnp.int32, (w, tt), 0)
        for e in range(n_exp):
            onehot_ref[e * w:(e + 1) * w, :] = jnp.where(
                slot_ref[e:e + 1, :] - starts[e] == rank, 1.0, 0.0).astype(BF16)
        rows = jnp.dot(onehot_ref[...], h_ref[...], preferred_element_type=F32)
        for e in range(n_exp):
            o_ref[e, pl.ds(starts[e], w), :] += rows[e * w:(e + 1) * w].astype(o_ref.dtype)

    @pl.when(jnp.logical_not(fits))
    def _():
        rank = lax.broadcasted_iota(jnp.int32, (cap, tt), 0)
        for e in range(n_exp):
            onehot = jnp.where(slot_ref[e:e + 1, :] == rank, 1.0, 0.0).astype(BF16)
            o_ref[e] += jnp.dot(onehot, h_ref[...], preferred_element_type=F32).astype(o_ref.dtype)


def _gather_windowed(slot, tile_lo, hx):
    b, e, t = slot.shape
    d = hx.shape[2]
    cap = EC_CAPACITY * t // N_EXPERTS
    return pl.pallas_call(
        _gather_win_body,
        grid_spec=pltpu.PrefetchScalarGridSpec(
            num_scalar_prefetch=1,
            grid=(b, t // MOE_TILE),
            in_specs=[
                pl.BlockSpec((None, e, MOE_TILE), lambda bi, ti, lo: (bi, 0, ti)),
                pl.BlockSpec((None, MOE_TILE, d), lambda bi, ti, lo: (bi, ti, 0)),
            ],
            out_specs=pl.BlockSpec((e, cap, d), lambda bi, ti, lo: (0, bi, 0)),
            scratch_shapes=[pltpu.VMEM((e * MOE_WINDOW, MOE_TILE), BF16)],
        ),
        out_shape=jax.ShapeDtypeStruct((e, b * cap, d), BF16),
        compiler_params=_cparams(2, 48),
        name="moe_gather_windowed",
    )(tile_lo, slot, hx)


def _ffn_body(*refs, n_streams, row_chunk):
    xs_refs = refs[:n_streams]
    wg_ref, wu_ref, wd_ref = refs[n_streams:n_streams + 3]
    ys_refs = refs[n_streams + 3:2 * n_streams + 3]
    acc_refs = refs[2 * n_streams + 3:3 * n_streams + 3]
    wgb_ref, wub_ref, wdb_ref = refs[3 * n_streams + 3:]
    f = pl.program_id(1)
    wgb_ref[...] = wg_ref[...].astype(BF16)
    wub_ref[...] = wu_ref[...].astype(BF16)
    wdb_ref[...] = wd_ref[...].astype(BF16)
    for xs_ref, ys_ref, acc_ref in zip(xs_refs, ys_refs, acc_refs):
        rc = min(row_chunk, xs_ref.shape[0])

        @pl.when(f == 0)
        def _(acc_ref=acc_ref):
            acc_ref[...] = jnp.zeros_like(acc_ref)

        def rows(i, carry, xs_ref=xs_ref, acc_ref=acc_ref, rc=rc):
            r0 = pl.multiple_of(i * rc, rc)
            xb = xs_ref[pl.ds(r0, rc), :]
            g = jnp.dot(xb, wgb_ref[...], preferred_element_type=F32)
            u = jnp.dot(xb, wub_ref[...], preferred_element_type=F32)
            hid = (_silu(g) * u).astype(BF16)
            acc_ref[pl.ds(r0, rc), :] += jnp.dot(hid, wdb_ref[...], preferred_element_type=F32)
            return carry

        lax.fori_loop(0, xs_ref.shape[0] // rc, rows, 0)

        @pl.when(f == pl.num_programs(1) - 1)
        def _(ys_ref=ys_ref, acc_ref=acc_ref):
            ys_ref[...] = acc_ref[...].astype(ys_ref.dtype)


def _expert_ffn(xs_list, w_gate, w_up, w_down, layer, *, tf=512, row_chunk=1024):
    _, e, d, ff = w_gate.shape
    n = len(xs_list)
    xs_specs = [pl.BlockSpec((None, xs.shape[1], d), lambda ei, fi: (ei, 0, 0)) for xs in xs_list]
    return pl.pallas_call(
        functools.partial(_ffn_body, n_streams=n, row_chunk=row_chunk),
        grid=(e, ff // tf),
        in_specs=xs_specs + [
            pl.BlockSpec((None, None, d, tf), lambda ei, fi: (layer, ei, 0, fi)),
            pl.BlockSpec((None, None, d, tf), lambda ei, fi: (layer, ei, 0, fi)),
            pl.BlockSpec((None, None, tf, d), lambda ei, fi: (layer, ei, fi, 0)),
        ],
        out_specs=xs_specs,
        out_shape=[jax.ShapeDtypeStruct(xs.shape, BF16) for xs in xs_list],
        scratch_shapes=[pltpu.VMEM(xs.shape[1:], F32) for xs in xs_list]
        + [pltpu.VMEM((d, tf), BF16), pltpu.VMEM((d, tf), BF16), pltpu.VMEM((tf, d), BF16)],
        compiler_params=_cparams(2, 56),
        name="expert_ffn",
    )(*xs_list, w_gate, w_up, w_down)


def _combine_body(ys_ref, slot_ref, wsel_ref, x_ref, g2_ref, *rest, final_norm):
    if final_norm:
        fg_ref, o_ref = rest
    else:
        (o_ref,) = rest
    n_exp, cap, _ = ys_ref.shape
    tm = x_ref.shape[0]
    rank = lax.broadcasted_iota(jnp.int32, (tm, cap), 1)
    slots = slot_ref[...]
    wsel = wsel_ref[...]
    o_ref[...] = jnp.zeros_like(o_ref)
    for e in range(n_exp):
        onehot = jnp.where(slots[:, e:e + 1] == rank, 1.0, 0.0).astype(BF16)
        o_ref[...] += wsel[:, e:e + 1] * jnp.dot(onehot, ys_ref[e], preferred_element_type=F32)
    out = x_ref[...] + g2_ref[...] * o_ref[...]
    if final_norm:
        out = out * lax.rsqrt(jnp.mean(out * out, axis=-1, keepdims=True) + NORM_EPS) * fg_ref[...]
    o_ref[...] = out


def _combine(ys, slot_t, wsel_t, x, mod, layer, row_of_batch, final_g=None, *, tm=512):
    b, t, d = x.shape
    e = ys.shape[0]
    cap = ys.shape[1] // b
    tm = min(tm, t)
    in_specs = [
        pl.BlockSpec((e, cap, d), lambda bi, ti: (0, bi, 0)),
        pl.BlockSpec((None, tm, e), lambda bi, ti: (bi, ti, 0)),
        pl.BlockSpec((None, tm, e), lambda bi, ti: (bi, ti, 0)),
        pl.BlockSpec((None, tm, d), lambda bi, ti: (bi, ti, 0)),
        _mod_spec(layer, 5, d, row_of_batch, 2, 0),
    ]
    args = [ys, slot_t, wsel_t, x, mod]
    if final_g is not None:
        in_specs.append(pl.BlockSpec((1, d), lambda bi, ti: (0, 0)))
        args.append(final_g.reshape(1, d))
    return pl.pallas_call(
        functools.partial(_combine_body, final_norm=final_g is not None),
        grid=(b, t // tm),
        in_specs=in_specs,
        out_specs=pl.BlockSpec((None, tm, d), lambda bi, ti: (bi, ti, 0)),
        out_shape=jax.ShapeDtypeStruct((b, t, d), F32),
        compiler_params=_cparams(2, 48),
        name="moe_combine",
    )(*args)


def _combine_win_body(lo_ref, ys_ref, slot_ref, wsel_ref, x_ref, g2_ref, *rest, final_norm):
    if final_norm:
        fg_ref, o_ref, ywin_ref = rest
    else:
        o_ref, ywin_ref = rest
    n_exp, cap, _ = ys_ref.shape
    tt = x_ref.shape[0]
    w = MOE_WINDOW
    i = pl.program_id(1)
    base = (pl.program_id(0) * (pl.num_programs(1) + 1) + i) * n_exp
    starts, fits = _window_starts(lo_ref, base, n_exp, cap)
    slots = slot_ref[...]
    wsel = wsel_ref[...]

    def finish(acc):
        out = x_ref[...] + g2_ref[...] * acc
        if final_norm:
            out = out * lax.rsqrt(jnp.mean(out * out, axis=-1, keepdims=True) + NORM_EPS) * fg_ref[...]
        o_ref[...] = out

    @pl.when(fits)
    def _():
        lane = lax.broadcasted_iota(jnp.int32, (tt, LANES), 1)
        pieces = []
        for e in range(n_exp):
            ywin_ref[e * w:(e + 1) * w, :] = ys_ref[e, pl.ds(starts[e], w), :]
        for e0 in range(0, n_exp, 2):
            e1 = e0 + 1
            s0, s1 = slots[:, e0:e0 + 1], slots[:, e1:e1 + 1]
            r0 = s0 - starts[e0]
            r1 = jnp.where(s1 >= 0, s1 - starts[e1] + w, -1)
            pieces.append(jnp.where(lane == r0, wsel[:, e0:e0 + 1],
                                    jnp.where(lane == r1, wsel[:, e1:e1 + 1], 0.0)).astype(BF16))
        scatter = jnp.concatenate(pieces, axis=1)
        finish(jnp.dot(scatter, ywin_ref[...], preferred_element_type=F32))

    @pl.when(jnp.logical_not(fits))
    def _():
        rank = lax.broadcasted_iota(jnp.int32, (tt, cap), 1)
        acc = jnp.zeros(x_ref.shape, F32)
        for e in range(n_exp):
            scatter = jnp.where(slots[:, e:e + 1] == rank, wsel[:, e:e + 1], 0.0).astype(BF16)
            acc = acc + jnp.dot(scatter, ys_ref[e], preferred_element_type=F32)
        finish(acc)


def _combine_windowed(ys, slot_t, wsel_t, tile_lo, x, mod, layer, row_of_batch, final_g=None):
    b, t, d = x.shape
    e = ys.shape[0]
    cap = ys.shape[1] // b
    tile = lambda bi, ti, lo: (bi, ti, 0)
    in_specs = [
        pl.BlockSpec((e, cap, d), lambda bi, ti, lo: (0, bi, 0)),
        pl.BlockSpec((None, MOE_TILE, e), tile),
        pl.BlockSpec((None, MOE_TILE, e), tile),
        pl.BlockSpec((None, MOE_TILE, d), tile),
        _mod_spec(layer, 5, d, row_of_batch, 2, 0),
    ]
    args = [ys, slot_t, wsel_t, x, mod]
    if final_g is not None:
        in_specs.append(pl.BlockSpec((1, d), lambda bi, ti, lo: (0, 0)))
        args.append(final_g.reshape(1, d))
    return pl.pallas_call(
        functools.partial(_combine_win_body, final_norm=final_g is not None),
        grid_spec=pltpu.PrefetchScalarGridSpec(
            num_scalar_prefetch=1,
            grid=(b, t // MOE_TILE),
            in_specs=in_specs,
            out_specs=pl.BlockSpec((None, MOE_TILE, d), tile),
            scratch_shapes=[pltpu.VMEM((e * MOE_WINDOW, d), BF16)],
        ),
        out_shape=jax.ShapeDtypeStruct((b, t, d), F32),
        compiler_params=_cparams(2, 48),
        name="moe_combine_windowed",
    )(tile_lo, *args)


def _attn_body(sink_ref, q_ref, k_ref, v_ref, kc_ref, vc_ref, o_ref, kband_ref, vband_ref):
    blk = SWA_BLOCK
    hd = SWA_HEAD_DIM
    t_ctx = kc_ref.shape[0]
    n_heads = q_ref.shape[1] // hd
    group = n_heads // SWA_KV_HEADS
    n = pl.program_id(1)
    nb = pl.num_programs(1)
    starts = (jnp.maximum(n - 1, 0), n, jnp.minimum(n + 1, nb - 1))
    kband_ref[0:t_ctx, :] = kc_ref[...]
    vband_ref[0:t_ctx, :] = vc_ref[...]
    for i, st in enumerate(starts):
        r0 = pl.multiple_of(st * blk, blk)
        kband_ref[t_ctx + i * blk:t_ctx + (i + 1) * blk, :] = k_ref[pl.ds(r0, blk), :]
        vband_ref[t_ctx + i * blk:t_ctx + (i + 1) * blk, :] = v_ref[pl.ds(r0, blk), :]
    qi = lax.broadcasted_iota(jnp.int32, (blk, blk), 0)
    kj = lax.broadcasted_iota(jnp.int32, (blk, blk), 1)
    neg = -jnp.inf
    bias_prev = jnp.where(kj >= qi, 0.0, neg) + jnp.where(n > 0, 0.0, neg)
    bias_next = jnp.where(kj <= qi, 0.0, neg) + jnp.where(n < nb - 1, 0.0, neg)
    bias = jnp.concatenate(
        [jnp.zeros((blk, t_ctx), F32), bias_prev, jnp.zeros((blk, blk), F32), bias_next], axis=1)
    band = bias.shape[1]
    head_idx = lax.broadcasted_iota(jnp.int32, (group, 1, 1), 0)
    for kv in range(SWA_KV_HEADS):
        heads = range(kv * group, (kv + 1) * group)
        qg = jnp.concatenate([q_ref[:, h * hd:(h + 1) * hd] for h in heads], axis=0)
        qg = qg * jnp.asarray(hd ** -0.5, BF16)
        kh = kband_ref[:, kv * hd:(kv + 1) * hd]
        vh = vband_ref[:, kv * hd:(kv + 1) * hd]
        s = lax.dot_general(qg, kh, NT_DIMS, preferred_element_type=F32)
        s = s.reshape(group, blk, band) + bias[None]
        sink = jnp.zeros((group, 1, 1), F32)
        for i, h in enumerate(heads):
            sink = jnp.where(head_idx == i, sink_ref[h], sink)
        m = jnp.maximum(jnp.max(s, axis=-1, keepdims=True), sink)
        p = jnp.exp(s - m)
        den = jnp.sum(p, axis=-1, keepdims=True) + jnp.exp(sink - m)
        o = jnp.dot(p.reshape(group * blk, band).astype(BF16), vh, preferred_element_type=F32)
        o = o.reshape(group, blk, hd) / den
        for i, h in enumerate(heads):
            o_ref[:, h * hd:(h + 1) * hd] = o[i].astype(o_ref.dtype)


def _window_attention(p, pc, sink):
    b, t, cols = p.shape
    t_ctx = pc.shape[1]
    kvw = SWA_KV_HEADS * SWA_HEAD_DIM
    qw = cols - 2 * kvw
    band = t_ctx + 3 * SWA_BLOCK
    return pl.pallas_call(
        _attn_body,
        grid=(b, t // SWA_BLOCK),
        in_specs=[
            pl.BlockSpec(memory_space=pltpu.SMEM),
            pl.BlockSpec((None, SWA_BLOCK, qw), lambda bi, ni: (bi, ni, 0)),
            pl.BlockSpec((None, t, kvw), lambda bi, ni: (bi, 0, qw // kvw)),
            pl.BlockSpec((None, t, kvw), lambda bi, ni: (bi, 0, qw // kvw + 1)),
            pl.BlockSpec((None, t_ctx, kvw), lambda bi, ni: (bi, 0, 0)),
            pl.BlockSpec((None, t_ctx, kvw), lambda bi, ni: (bi, 0, 1)),
        ],
        out_specs=pl.BlockSpec((None, SWA_BLOCK, qw), lambda bi, ni: (bi, ni, 0)),
        out_shape=jax.ShapeDtypeStruct((b, t, qw), BF16),
        scratch_shapes=[pltpu.VMEM((band, kvw), BF16), pltpu.VMEM((band, kvw), BF16)],
        compiler_params=_cparams(2, 32),
        name="window_attention",
    )(sink.astype(F32), p, p, p, pc, pc)


def _rope_tables(t):
    n_freq = SWA_HEAD_DIM // 4
    pos = jnp.arange(t)
    row = (pos // GRID_W).astype(F32)
    col = (pos % GRID_W).astype(F32)
    inv_freq = ROPE_BASE ** (-jnp.arange(n_freq, dtype=F32) / n_freq)
    ang_r, ang_c = row[:, None] * inv_freq, col[:, None] * inv_freq
    zero = jnp.zeros_like(ang_r)
    cos = jnp.concatenate([jnp.cos(ang_r)] * 2 + [jnp.cos(ang_c)] * 2, axis=1)
    sin_minus = jnp.concatenate([-jnp.sin(ang_r), zero, -jnp.sin(ang_c), zero], axis=1)
    sin_plus = jnp.concatenate([zero, jnp.sin(ang_r), zero, jnp.sin(ang_c)], axis=1)
    reps = LANES // SWA_HEAD_DIM
    return tuple(jnp.tile(tab, (1, reps)) for tab in (cos, sin_minus, sin_plus))


def _use_windows(t):
    cap = EC_CAPACITY * t // N_EXPERTS
    assert 2 * MOE_WINDOW == LANES
    return t > MOE_TILE and cap >= MOE_WINDOW and (cap - MOE_WINDOW) % BF16_ROWS == 0


def _moe(streams, w_gate, w_up, w_down, layer):
    routed = []
    for hx, aff in streams:
        slot, wsel, tile_lo = _topk_select(aff)
        windowed = _use_windows(hx.shape[1])
        xs = _gather_windowed(slot, tile_lo, hx) if windowed else _gather(slot, hx)
        routed.append((xs, jnp.swapaxes(slot, 1, 2), jnp.swapaxes(wsel, 1, 2), tile_lo, windowed))
    ys_list = _expert_ffn([r[0] for r in routed], w_gate, w_up, w_down, layer)

    def make_combine(ys, slot_t, wsel_t, tile_lo, windowed):
        def combine(x, mod, row_of_batch, final_g=None):
            if windowed:
                return _combine_windowed(ys, slot_t, wsel_t, tile_lo, x, mod, layer, row_of_batch, final_g)
            return _combine(ys, slot_t, wsel_t, x, mod, layer, row_of_batch, final_g)
        return combine

    return [make_combine(ys, *r[1:]) for ys, r in zip(ys_list, routed)]


def kernel(x, c, ctx, c_ctx, ada_w, ada_b, norm1_g, norm2_g, final_g, ret_w_in, ret_decay_f, ret_decay_b, ret_gn_f, ret_gn_b, ret_w_out, swa_w_qkv, swa_sink, swa_w_out, moe_router, moe_w_gate, moe_w_up, moe_w_down):
    b, t, d = x.shape
    lat_row = lambda bi: bi
    ctx_row = lambda bi: b
    cond = jnp.zeros((COND_ROWS, d), F32).at[:b].set(c).at[b].set(c_ctx)
    mod = _adaln(cond, ada_w, ada_b)

    ret_tn = 2048
    gate_col0 = (ret_w_in.shape[2] // 8) * 4
    assert gate_col0 % ret_tn == 0
    ret_proj = dict(tm=512, tn=ret_tn, silu_from_tile=gate_col0 // ret_tn)
    p = _norm_mod_matmul(x, norm1_g[0], mod, 0, lat_row, ret_w_in[0], **ret_proj)
    pc = _norm_mod_matmul(ctx, norm1_g[0], mod, 0, ctx_row, ret_w_in[0], **ret_proj)
    y, yc = _retention(p, pc, ret_decay_f[0], ret_decay_b[0], ret_gn_f[0], ret_gn_b[0])
    x1, hx, aff = _outproj_router(y, ret_w_out[0], x, mod, 0, lat_row, norm2_g[0], moe_router[0], tm=512)
    c1, hc, affc = _outproj_router(yc, ret_w_out[0], ctx, mod, 0, ctx_row, norm2_g[0], moe_router[0], tm=512)
    combine_lat, combine_ctx = _moe([(hx, aff), (hc, affc)], moe_w_gate, moe_w_up, moe_w_down, 0)
    x2 = combine_lat(x1, mod, lat_row)
    c2 = combine_ctx(c1, mod, ctx_row)

    qw = SWA_HEAD_DIM * (swa_w_qkv.shape[2] // SWA_HEAD_DIM - 2 * SWA_KV_HEADS)
    cos, sin_minus, sin_plus = _rope_tables(t)
    rope_groups = (qw + SWA_KV_HEADS * SWA_HEAD_DIM) // LANES
    p = _norm_mod_matmul(x2, norm1_g[1], mod, 1, lat_row, swa_w_qkv[0], tm=512,
                         tn=swa_w_qkv.shape[2], rope=(cos, sin_minus, sin_plus, rope_groups))
    w_kv = swa_w_qkv[0][:, qw:]
    pc = _norm_mod_matmul(c2, norm1_g[1], mod, 1, ctx_row, w_kv, tm=512, tn=w_kv.shape[1])
    o = _window_attention(p, pc, swa_sink[0])
    x3, hx, aff = _outproj_router(o, swa_w_out[0], x2, mod, 1, lat_row, norm2_g[1], moe_router[1], tm=512)
    (combine_lat,) = _moe([(hx, aff)], moe_w_gate, moe_w_up, moe_w_down, 1)
    return combine_lat(x3, mod, lat_row, final_g)
```

```python
import functools

import jax
import jax.numpy as jnp
from jax import lax
from jax.experimental import pallas as pl
from jax.experimental.pallas import tpu as pltpu

F32 = jnp.float32
BF16 = jnp.bfloat16

NORM_EPS = 1e-6
GRID_W = 64
RET_HEADS = 4
RET_CHUNK = 256
SWA_HEAD_DIM = 64
SWA_KV_HEADS = 2
SWA_BLOCK = 128
ROPE_BASE = 10000.0
N_EXPERTS = 16
EC_CAPACITY = 2

ROW_TILE = 1024
MOE_TILE = 256
MOE_WINDOW = 64
MOE_TILES_PER_STEP = 4

LANES = 128
BF16_ROWS = 16
MIB = 1024 * 1024
COND_ROWS = 16

NT_DIMS = (((1,), (1,)), ((), ()))
TN_DIMS = (((0,), (0,)), ((), ()))


def _cparams(n_axes, vmem_mib):
    return pltpu.CompilerParams(
        dimension_semantics=("arbitrary",) * n_axes,
        vmem_limit_bytes=vmem_mib * MIB,
    )


def _silu(v):
    return v / (1.0 + jnp.exp(-v))


def _rmsnorm_mod(xv, g, shift, scale):
    h = xv * lax.rsqrt(jnp.mean(xv * xv, axis=-1, keepdims=True) + NORM_EPS) * g
    return h * (1.0 + scale) + shift


def _mod_spec(layer, chunk, d, row_of_batch, n_grid_axes, batch_axis):
    def index_map(*ids):
        return (layer, row_of_batch(ids[batch_axis]), 0, chunk)
    del n_grid_axes
    return pl.BlockSpec((None, None, 1, d), index_map)


def _adaln_body(cond_ref, w_ref, b_ref, o_ref):
    s = _silu(cond_ref[...])
    o_ref[...] = jnp.dot(s.astype(BF16), w_ref[...].astype(BF16),
                         preferred_element_type=F32) + b_ref[...]


def _adaln(cond, ada_w, ada_b):
    layers, d, n = ada_w.shape
    tn = d
    out = pl.pallas_call(
        _adaln_body,
        grid=(layers, n // tn),
        in_specs=[
            pl.BlockSpec((COND_ROWS, d), lambda l, j: (0, 0)),
            pl.BlockSpec((None, d, tn), lambda l, j: (l, 0, j)),
            pl.BlockSpec((None, 1, tn), lambda l, j: (l, 0, j)),
        ],
        out_specs=pl.BlockSpec((None, COND_ROWS, tn), lambda l, j: (l, 0, j)),
        out_shape=jax.ShapeDtypeStruct((layers, COND_ROWS, n), F32),
        compiler_params=_cparams(2, 32),
        name="adaln",
    )(cond, ada_w, ada_b.reshape(layers, 1, n))
    return out.reshape(layers, COND_ROWS, 1, n)


def _nmm_body(x_ref, g_ref, sh_ref, sc_ref, w_ref, *rest, rope_groups, silu_from_tile):
    if rope_groups:
        cos_ref, sm_ref, sp_ref, o_ref, wbf_ref = rest
    else:
        o_ref, wbf_ref = rest

    @pl.when((pl.program_id(1) == 0) & (pl.program_id(2) == 0))
    def _():
        wbf_ref[...] = w_ref[...].astype(BF16)

    def project():
        h = _rmsnorm_mod(x_ref[...], g_ref[...], sh_ref[...], sc_ref[...])
        return jnp.dot(h.astype(BF16), wbf_ref[...], preferred_element_type=F32)

    if silu_from_tile is not None:
        @pl.when(pl.program_id(0) < silu_from_tile)
        def _():
            o_ref[...] = project().astype(o_ref.dtype)

        @pl.when(pl.program_id(0) >= silu_from_tile)
        def _():
            o_ref[...] = _silu(project()).astype(o_ref.dtype)

        return
    acc = project()
    if not rope_groups:
        o_ref[...] = acc.astype(o_ref.dtype)
        return
    c, sm, sp = cos_ref[...], sm_ref[...], sp_ref[...]
    n_groups = acc.shape[1] // LANES
    for j in range(n_groups):
        a = acc[:, j * LANES:(j + 1) * LANES]
        if j < rope_groups:
            a = (a * c + pltpu.roll(a, LANES - 16, axis=1) * sm
                 + pltpu.roll(a, 16, axis=1) * sp)
        o_ref[:, j * LANES:(j + 1) * LANES] = a.astype(o_ref.dtype)


def _norm_mod_matmul(x, g, mod, layer, row_of_batch, w, *, tm, tn, rope=None, silu_from_tile=None):
    b, t, d = x.shape
    n = w.shape[1]
    tm = min(tm, t)
    grid = (n // tn, b, t // tm)
    in_specs = [
        pl.BlockSpec((None, tm, d), lambda j, bi, ti: (bi, ti, 0)),
        pl.BlockSpec((1, d), lambda j, bi, ti: (0, 0)),
        _mod_spec(layer, 0, d, row_of_batch, 3, 1),
        _mod_spec(layer, 1, d, row_of_batch, 3, 1),
        pl.BlockSpec((d, tn), lambda j, bi, ti: (0, j), pipeline_mode=pl.Buffered(1)),
    ]
    args = [x, g.reshape(1, d), mod, mod, w]
    rope_groups = 0
    if rope is not None:
        cos, sm, sp, rope_groups = rope
        for tab in (cos, sm, sp):
            in_specs.append(pl.BlockSpec((tm, LANES), lambda j, bi, ti: (ti, 0)))
            args.append(tab)
    return pl.pallas_call(
        functools.partial(_nmm_body, rope_groups=rope_groups, silu_from_tile=silu_from_tile),
        grid=grid,
        in_specs=in_specs,
        out_specs=pl.BlockSpec((None, tm, tn), lambda j, bi, ti: (bi, ti, j)),
        out_shape=jax.ShapeDtypeStruct((b, t, n), BF16),
        scratch_shapes=[pltpu.VMEM((d, tn), BF16)],
        compiler_params=_cparams(3, 48),
        name="norm_mod_proj",
    )(*args)


def _ret_body(dec_ref, gnf_ref, gnb_ref,
              qc_ref, kc_ref, vc_ref, gfc_ref, gbc_ref,
              q_ref, k_ref, v_ref, gf_ref, gb_ref,
              y_ref, yc_ref,
              statef_ref, stateb_ref, yacc_ref, intra_ref, qd_ref, kd_ref, cd_ref):
    c = RET_CHUNK
    dk = q_ref.shape[1]
    dv = v_ref.shape[1]
    t_ctx = qc_ref.shape[0]
    t_lat = q_ref.shape[0]

    dec = dec_ref[...]
    lg = jnp.minimum(dec, 0.0) - jnp.log1p(jnp.exp(-jnp.abs(dec)))
    row = lax.broadcasted_iota(jnp.int32, (c, c), 0)
    col = lax.broadcasted_iota(jnp.int32, (c, c), 1)
    pos_v = lax.broadcasted_iota(jnp.int32, (c, dv), 0).astype(F32)
    pos_k = lax.broadcasted_iota(jnp.int32, (c, dk), 0).astype(F32)
    for d in range(2):
        lg_c, lg_v, lg_k = lg[d:d + 1, :c], lg[d:d + 1, :], lg[d:d + 1, :dk]
        diff = (row - col) if d == 0 else (col - row)
        intra_ref[d] = jnp.where(diff >= 0, jnp.exp(jnp.maximum(diff, 0).astype(F32) * lg_c), 0.0)
        if d == 0:
            qd_ref[d] = jnp.exp((pos_v + 1.0) * lg_v)
            kd_ref[d] = jnp.exp((c - 1.0 - pos_k) * lg_k)
        else:
            qd_ref[d] = jnp.exp((c - pos_v) * lg_v)
            kd_ref[d] = jnp.exp(pos_k * lg_k)
        cd_ref[d] = jnp.exp(float(c) * lg_v)

    def chunk(d, refs, r0, yoff, out_ref, first_visit):
        qr, kr, vr, gr = refs
        state_ref = statef_ref if d == 0 else stateb_ref
        if not isinstance(r0, int):
            r0, yoff = pl.multiple_of(r0, c), pl.multiple_of(yoff, c)
        qs = qr[pl.ds(r0, c), :] * jnp.asarray(dk ** -0.5, BF16)
        kc = kr[pl.ds(r0, c), :]
        vc = vr[pl.ds(r0, c), :]
        s = lax.dot_general(qs, kc, NT_DIMS, preferred_element_type=F32) * intra_ref[d]
        st = state_ref[...]
        o = (jnp.dot(s.astype(BF16), vc, preferred_element_type=F32)
             + jnp.dot(qs, st.astype(BF16), preferred_element_type=F32) * qd_ref[d])
        kk = (kc.astype(F32) * kd_ref[d]).astype(BF16)
        state_ref[...] = cd_ref[d] * st + lax.dot_general(kk, vc, TN_DIMS, preferred_element_type=F32)
        mu = jnp.mean(o, axis=-1, keepdims=True)
        oc = o - mu
        var = jnp.mean(oc * oc, axis=-1, keepdims=True)
        gn = gnf_ref[...] if d == 0 else gnb_ref[...]
        y = gr[pl.ds(r0, c), :].astype(F32) * (oc * lax.rsqrt(var + NORM_EPS) * gn)
        if first_visit:
            yacc_ref[pl.ds(yoff, c), :] = y
        else:
            out_ref[pl.ds(r0, c), :] = (yacc_ref[pl.ds(yoff, c), :] + y).astype(out_ref.dtype)

    statef_ref[...] = jnp.zeros_like(statef_ref)
    stateb_ref[...] = jnp.zeros_like(stateb_ref)
    segments = (
        ((qc_ref, kc_ref, vc_ref), (gfc_ref, gbc_ref), t_ctx // c, 0, yc_ref),
        ((q_ref, k_ref, v_ref), (gf_ref, gb_ref), t_lat // c, t_ctx, y_ref),
    )
    for qkv, gates, n, ybase, out_ref in segments:
        half = n // 2

        def step(i, carry, first_visit, qkv=qkv, gates=gates, n=n, ybase=ybase, out_ref=out_ref):
            for d in range(2):
                ci = i if d == 0 else n - 1 - i
                chunk(d, qkv + (gates[d],), ci * c, ybase + ci * c, out_ref, first_visit)
            return carry

        unroll = 4 if half >= 4 else 1
        if half:
            lax.fori_loop(0, half, functools.partial(step, first_visit=True), 0, unroll=unroll)
        if n % 2:
            for d in range(2):
                chunk(d, qkv + (gates[d],), half * c, ybase + half * c, out_ref, d == 0)
        if half:
            lax.fori_loop(n - half, n, functools.partial(step, first_visit=False), 0, unroll=unroll)


def _retention(p, pc, decay_f, decay_b, gn_f, gn_b):
    b, t, cols = p.shape
    t_ctx = pc.shape[1]
    assert t % RET_CHUNK == 0 and t_ctx % RET_CHUNK == 0
    h = RET_HEADS
    dk = cols // (8 * h)
    dv = 2 * dk
    hv = h * dv
    dec = jnp.broadcast_to(jnp.stack([decay_f, decay_b], axis=1)[:, :, None], (h, 2, dv)).astype(F32)
    k_blk, v_blk, gf_blk, gb_blk = h, (2 * h * dk) // dv, (2 * h * dk + hv) // dv, (2 * h * dk + 2 * hv) // dv

    def specs(rows):
        return [
            pl.BlockSpec((None, rows, dk), lambda bi, hi: (bi, 0, hi)),
            pl.BlockSpec((None, rows, dk), lambda bi, hi: (bi, 0, k_blk + hi)),
            pl.BlockSpec((None, rows, dv), lambda bi, hi: (bi, 0, v_blk + hi)),
            pl.BlockSpec((None, rows, dv), lambda bi, hi: (bi, 0, gf_blk + hi)),
            pl.BlockSpec((None, rows, dv), lambda bi, hi: (bi, 0, gb_blk + hi)),
        ]

    gn_spec = pl.BlockSpec((1, dv), lambda bi, hi: (0, hi))
    return pl.pallas_call(
        _ret_body,
        grid=(b, h),
        in_specs=[pl.BlockSpec((None, 2, dv), lambda bi, hi: (hi, 0, 0)), gn_spec, gn_spec]
        + specs(t_ctx) + specs(t),
        out_specs=[
            pl.BlockSpec((None, t, dv), lambda bi, hi: (bi, 0, hi)),
            pl.BlockSpec((None, t_ctx, dv), lambda bi, hi: (bi, 0, hi)),
        ],
        out_shape=[
            jax.ShapeDtypeStruct((b, t, hv), BF16),
            jax.ShapeDtypeStruct((b, t_ctx, hv), BF16),
        ],
        scratch_shapes=[
            pltpu.VMEM((dk, dv), F32),
            pltpu.VMEM((dk, dv), F32),
            pltpu.VMEM((t_ctx + t, dv), F32),
            pltpu.VMEM((2, RET_CHUNK, RET_CHUNK), F32),
            pltpu.VMEM((2, RET_CHUNK, dv), F32),
            pltpu.VMEM((2, RET_CHUNK, dk), F32),
            pltpu.VMEM((2, 1, dv), F32),
        ],
        compiler_params=_cparams(2, 48),
        name="retention",
    )(dec, gn_f.reshape(1, hv), gn_b.reshape(1, hv), pc, pc, pc, pc, pc, p, p, p, p, p)


def _outproj_body(y_ref, w_ref, x_ref, g1_ref, ng_ref, sh_ref, sc_ref, wr_ref,
                  x1_ref, hx_ref, aff_ref, wbf_ref):
    @pl.when((pl.program_id(0) == 0) & (pl.program_id(1) == 0))
    def _():
        wbf_ref[...] = w_ref[...].astype(BF16)

    x1 = x_ref[...] + g1_ref[...] * jnp.dot(y_ref[...], wbf_ref[...], preferred_element_type=F32)
    x1_ref[...] = x1
    h = _rmsnorm_mod(x1, ng_ref[...], sh_ref[...], sc_ref[...])
    hb = h.astype(BF16)
    hx_ref[...] = hb
    hlo = (h - hb.astype(F32)).astype(BF16)
    wr = wr_ref[...]
    whi = wr.astype(BF16)
    wlo = (wr - whi.astype(F32)).astype(BF16)
    logits = (lax.dot_general(whi, hb, NT_DIMS, preferred_element_type=F32)
              + lax.dot_general(whi, hlo, NT_DIMS, preferred_element_type=F32)
              + lax.dot_general(wlo, hb, NT_DIMS, preferred_element_type=F32)
              + lax.dot_general(wlo, hlo, NT_DIMS, preferred_element_type=F32))
    e = jnp.exp(logits - jnp.max(logits, axis=0, keepdims=True))
    aff_ref[...] = e / jnp.sum(e, axis=0, keepdims=True)


def _outproj_router(y, w, x, mod, layer, row_of_batch, norm_g, w_router, *, tm):
    b, t, kdim = y.shape
    d = w.shape[1]
    e = w_router.shape[1]
    tm = min(tm, t)
    row = lambda bi, ti: (0, 0)
    return pl.pallas_call(
        _outproj_body,
        grid=(b, t // tm),
        in_specs=[
            pl.BlockSpec((None, tm, kdim), lambda bi, ti: (bi, ti, 0)),
            pl.BlockSpec((kdim, d), row, pipeline_mode=pl.Buffered(1)),
            pl.BlockSpec((None, tm, d), lambda bi, ti: (bi, ti, 0)),
            _mod_spec(layer, 2, d, row_of_batch, 2, 0),
            pl.BlockSpec((1, d), row),
            _mod_spec(layer, 3, d, row_of_batch, 2, 0),
            _mod_spec(layer, 4, d, row_of_batch, 2, 0),
            pl.BlockSpec((e, d), row),
        ],
        out_specs=[
            pl.BlockSpec((None, tm, d), lambda bi, ti: (bi, ti, 0)),
            pl.BlockSpec((None, tm, d), lambda bi, ti: (bi, ti, 0)),
            pl.BlockSpec((None, e, tm), lambda bi, ti: (bi, 0, ti)),
        ],
        out_shape=[
            jax.ShapeDtypeStruct((b, t, d), F32),
            jax.ShapeDtypeStruct((b, t, d), BF16),
            jax.ShapeDtypeStruct((b, e, t), F32),
        ],
        scratch_shapes=[pltpu.VMEM((kdim, d), BF16)],
        compiler_params=_cparams(2, 48),
        name="outproj_router",
    )(y, w, x, mod, norm_g.reshape(1, d), mod, mod, w_router.T)


def _topk_body(aff_ref, slot_ref, wsel_ref, lo_ref, *, cap):
    a = aff_ref[...]
    r, t = a.shape
    capf = float(cap)

    def count_ge(v):
        return jnp.sum(jnp.where(a >= v, 1.0, 0.0), axis=1, keepdims=True)

    def bisect(_, lohi):
        lo, hi = lohi
        mid = 0.5 * (lo + hi)
        ok = count_ge(mid) >= capf
        return jnp.where(ok, mid, lo), jnp.where(ok, hi, mid)

    lo, hi = lax.fori_loop(0, 32, bisect, (jnp.zeros((r, 1), F32), jnp.full((r, 1), 2.0, F32)))

    def unfinished(state):
        return jnp.min(state[2]) < 0.5

    def walk(state):
        lo, hi, done = state
        top = jnp.max(jnp.where(a < hi, a, -1.0), axis=1, keepdims=True)
        found = (count_ge(top) >= capf) & (done < 0.5)
        moved = (done < 0.5) & jnp.logical_not(found)
        return jnp.where(found, top, lo), jnp.where(moved, top, hi), jnp.where(found, 1.0, done)

    thr, _, _ = lax.while_loop(unfinished, walk, (lo, hi, jnp.zeros((r, 1), F32)))

    gt = a > thr
    eq = a == thr
    need = capf - jnp.sum(jnp.where(gt, 1.0, 0.0), axis=1, keepdims=True)
    upper = jnp.where(lax.broadcasted_iota(jnp.int32, (LANES, LANES), 0)
                      < lax.broadcasted_iota(jnp.int32, (LANES, LANES), 1), 1.0, 0.0).astype(BF16)
    carry_eq = jnp.zeros((r, 1), F32)
    carry_sel = jnp.zeros((r, 1), F32)
    lane = lax.broadcasted_iota(jnp.int32, (r, LANES), 1)
    tile_lo = jnp.zeros((r, LANES), F32)
    for j in range(t // LANES):
        sl = slice(j * LANES, (j + 1) * LANES)
        if (j * LANES) % MOE_TILE == 0:
            tile_lo = jnp.where(lane == (j * LANES) // MOE_TILE, carry_sel, tile_lo)
        eqf = jnp.where(eq[:, sl], 1.0, 0.0)
        pre_eq = jnp.dot(eqf.astype(BF16), upper, preferred_element_type=F32) + carry_eq
        self_ = jnp.where(gt[:, sl], 1.0, jnp.where(pre_eq < need, eqf, 0.0))
        pre_sel = jnp.dot(self_.astype(BF16), upper, preferred_element_type=F32) + carry_sel
        sel = self_ > 0.5
        slot_ref[:, sl] = jnp.where(sel, pre_sel, -1.0).astype(jnp.int32)
        wsel_ref[:, sl] = jnp.where(sel, a[:, sl], 0.0)
        carry_eq = carry_eq + jnp.sum(eqf, axis=1, keepdims=True)
        carry_sel = carry_sel + jnp.sum(self_, axis=1, keepdims=True)
    tile_lo = jnp.where(lane == t // MOE_TILE, carry_sel, tile_lo)
    lo_ref[...] = tile_lo.astype(jnp.int32)


def _topk_select(aff):
    b, e, t = aff.shape
    cap = EC_CAPACITY * t // N_EXPERTS
    r = b * e
    n_tiles = t // MOE_TILE
    assert t % MOE_TILE == 0 and n_tiles < LANES
    full = pl.BlockSpec((r, t), lambda i: (0, 0))
    slot, wsel, lo = pl.pallas_call(
        functools.partial(_topk_body, cap=cap),
        grid=(1,),
        in_specs=[full],
        out_specs=[full, full, pl.BlockSpec((r, LANES), lambda i: (0, 0))],
        out_shape=[jax.ShapeDtypeStruct((r, t), jnp.int32), jax.ShapeDtypeStruct((r, t), F32),
                   jax.ShapeDtypeStruct((r, LANES), jnp.int32)],
        compiler_params=_cparams(1, 32),
        name="topk_select",
    )(aff.reshape(r, t))
    tile_lo = jnp.swapaxes(lo.reshape(b, e, LANES)[:, :, :n_tiles + 1], 1, 2).reshape(-1)
    return slot.reshape(b, e, t), wsel.reshape(b, e, t), tile_lo


def _gather_body(slot_ref, h_ref, o_ref, onehot_ref):
    n_exp, cap, d = o_ref.shape
    t = h_ref.shape[0]
    rank = lax.broadcasted_iota(jnp.int32, (cap, t), 0)
    for e in range(n_exp):
        onehot_ref[e * cap:(e + 1) * cap, :] = jnp.where(
            slot_ref[e:e + 1, :] == rank, 1.0, 0.0).astype(BF16)
    xs = jnp.dot(onehot_ref[...], h_ref[...], preferred_element_type=F32)
    o_ref[...] = xs.reshape(n_exp, cap, d).astype(o_ref.dtype)


def _gather(slot, hx, *, experts_per_step=8):
    b, e, t = slot.shape
    d = hx.shape[2]
    cap = EC_CAPACITY * t // N_EXPERTS
    eg = experts_per_step
    return pl.pallas_call(
        _gather_body,
        grid=(b, e // eg),
        in_specs=[
            pl.BlockSpec((None, eg, t), lambda bi, gi: (bi, gi, 0)),
            pl.BlockSpec((None, t, d), lambda bi, gi: (bi, 0, 0)),
        ],
        out_specs=pl.BlockSpec((eg, cap, d), lambda bi, gi: (gi, bi, 0)),
        out_shape=jax.ShapeDtypeStruct((e, b * cap, d), BF16),
        scratch_shapes=[pltpu.VMEM((eg * cap, t), BF16)],
        compiler_params=_cparams(2, 48),
        name="moe_gather",
    )(slot, hx)


def _window_starts(lo_ref, base, n_exp, cap):
    starts = []
    fits = None
    for e in range(n_exp):
        lo = lo_ref[base + e]
        hi = lo_ref[base + n_exp + e]
        start = jnp.minimum((lo // BF16_ROWS) * BF16_ROWS, cap - MOE_WINDOW)
        starts.append(pl.multiple_of(start, BF16_ROWS))
        ok = hi - start <= MOE_WINDOW
        fits = ok if fits is None else fits & ok
    return starts, fits


def _gather_win_body(lo_ref, slot_ref, h_ref, o_ref, onehot_ref):
    n_exp, cap, d = o_ref.shape
    tt = MOE_TILE
    tiles = h_ref.shape[0] // tt
    w = MOE_WINDOW
    i = pl.program_id(1)

    @pl.when(i == 0)
    def _():
        o_ref[...] = jnp.zeros_like(o_ref)

    for sub in range(tiles):
        tok = slice(sub * tt, (sub + 1) * tt)
        base = (pl.program_id(0) * (pl.num_programs(1) * tiles + 1) + i * tiles + sub) * n_exp
        starts, fits = _window_starts(lo_ref, base, n_exp, cap)

        @pl.when(fits)
        def _(tok=tok, starts=starts):
            rank = lax.broadcasted_iota(jnp.int32, (w, tt), 0)
            for e in range(n_exp):
                onehot_ref[e * w:(e + 1) * w, :] = jnp.where(
                    slot_ref[e:e + 1, tok] - starts[e] == rank, 1.0, 0.0).astype(BF16)
            rows = jnp.dot(onehot_ref[...], h_ref[tok, :], preferred_element_type=F32)
            for e in range(n_exp):
                o_ref[e, pl.ds(starts[e], w), :] += rows[e * w:(e + 1) * w].astype(o_ref.dtype)

        @pl.when(jnp.logical_not(fits))
        def _(tok=tok):
            rank = lax.broadcasted_iota(jnp.int32, (cap, tt), 0)
            for e in range(n_exp):
                onehot = jnp.where(slot_ref[e:e + 1, tok] == rank, 1.0, 0.0).astype(BF16)
                o_ref[e] += jnp.dot(onehot, h_ref[tok, :], preferred_element_type=F32).astype(o_ref.dtype)


def _gather_windowed(slot, tile_lo, hx):
    b, e, t = slot.shape
    d = hx.shape[2]
    cap = EC_CAPACITY * t // N_EXPERTS
    rows = MOE_TILE * MOE_TILES_PER_STEP
    return pl.pallas_call(
        _gather_win_body,
        grid_spec=pltpu.PrefetchScalarGridSpec(
            num_scalar_prefetch=1,
            grid=(b, t // rows),
            in_specs=[
                pl.BlockSpec((None, e, rows), lambda bi, ti, lo: (bi, 0, ti)),
                pl.BlockSpec((None, rows, d), lambda bi, ti, lo: (bi, ti, 0)),
            ],
            out_specs=pl.BlockSpec((e, cap, d), lambda bi, ti, lo: (0, bi, 0)),
            scratch_shapes=[pltpu.VMEM((e * MOE_WINDOW, MOE_TILE), BF16)],
        ),
        out_shape=jax.ShapeDtypeStruct((e, b * cap, d), BF16),
        compiler_params=_cparams(2, 48),
        name="moe_gather_windowed",
    )(tile_lo, slot, hx)


def _ffn_body(*refs, n_streams, row_chunk):
    xs_refs = refs[:n_streams]
    wg_ref, wu_ref, wd_ref = refs[n_streams:n_streams + 3]
    ys_refs = refs[n_streams + 3:2 * n_streams + 3]
    acc_refs = refs[2 * n_streams + 3:3 * n_streams + 3]
    wgb_ref, wub_ref, wdb_ref = refs[3 * n_streams + 3:]
    f = pl.program_id(1)
    wgb_ref[...] = wg_ref[...].astype(BF16)
    wub_ref[...] = wu_ref[...].astype(BF16)
    wdb_ref[...] = wd_ref[...].astype(BF16)
    for xs_ref, ys_ref, acc_ref in zip(xs_refs, ys_refs, acc_refs):
        rc = min(row_chunk, xs_ref.shape[0])

        @pl.when(f == 0)
        def _(acc_ref=acc_ref):
            acc_ref[...] = jnp.zeros_like(acc_ref)

        def rows(i, carry, xs_ref=xs_ref, acc_ref=acc_ref, rc=rc):
            r0 = pl.multiple_of(i * rc, rc)
            xb = xs_ref[pl.ds(r0, rc), :]
            g = jnp.dot(xb, wgb_ref[...], preferred_element_type=F32)
            u = jnp.dot(xb, wub_ref[...], preferred_element_type=F32)
            hid = (_silu(g) * u).astype(BF16)
            acc_ref[pl.ds(r0, rc), :] += jnp.dot(hid, wdb_ref[...], preferred_element_type=F32)
            return carry

        lax.fori_loop(0, xs_ref.shape[0] // rc, rows, 0)

        @pl.when(f == pl.num_programs(1) - 1)
        def _(ys_ref=ys_ref, acc_ref=acc_ref):
            ys_ref[...] = acc_ref[...].astype(ys_ref.dtype)


def _expert_ffn(xs_list, w_gate, w_up, w_down, layer, *, tf=512, row_chunk=1024):
    _, e, d, ff = w_gate.shape
    n = len(xs_list)
    xs_specs = [pl.BlockSpec((None, xs.shape[1], d), lambda ei, fi: (ei, 0, 0)) for xs in xs_list]
    return pl.pallas_call(
        functools.partial(_ffn_body, n_streams=n, row_chunk=row_chunk),
        grid=(e, ff // tf),
        in_specs=xs_specs + [
            pl.BlockSpec((None, None, d, tf), lambda ei, fi: (layer, ei, 0, fi)),
            pl.BlockSpec((None, None, d, tf), lambda ei, fi: (layer, ei, 0, fi)),
            pl.BlockSpec((None, None, tf, d), lambda ei, fi: (layer, ei, fi, 0)),
        ],
        out_specs=xs_specs,
        out_shape=[jax.ShapeDtypeStruct(xs.shape, BF16) for xs in xs_list],
        scratch_shapes=[pltpu.VMEM(xs.shape[1:], F32) for xs in xs_list]
        + [pltpu.VMEM((d, tf), BF16), pltpu.VMEM((d, tf), BF16), pltpu.VMEM((tf, d), BF16)],
        compiler_params=_cparams(2, 56),
        name="expert_ffn",
    )(*xs_list, w_gate, w_up, w_down)


def _combine_body(ys_ref, slot_ref, wsel_ref, x_ref, g2_ref, *rest, final_norm):
    if final_norm:
        fg_ref, o_ref = rest
    else:
        (o_ref,) = rest
    n_exp, cap, _ = ys_ref.shape
    tm = x_ref.shape[0]
    rank = lax.broadcasted_iota(jnp.int32, (tm, cap), 1)
    slots = slot_ref[...]
    wsel = wsel_ref[...]
    o_ref[...] = jnp.zeros_like(o_ref)
    for e in range(n_exp):
        onehot = jnp.where(slots[:, e:e + 1] == rank, 1.0, 0.0).astype(BF16)
        o_ref[...] += wsel[:, e:e + 1] * jnp.dot(onehot, ys_ref[e], preferred_element_type=F32)
    out = x_ref[...] + g2_ref[...] * o_ref[...]
    if final_norm:
        out = out * lax.rsqrt(jnp.mean(out * out, axis=-1, keepdims=True) + NORM_EPS) * fg_ref[...]
    o_ref[...] = out


def _combine(ys, slot_t, wsel_t, x, mod, layer, row_of_batch, final_g=None, *, tm=ROW_TILE):
    b, t, d = x.shape
    e = ys.shape[0]
    cap = ys.shape[1] // b
    tm = min(tm, t)
    in_specs = [
        pl.BlockSpec((e, cap, d), lambda bi, ti: (0, bi, 0)),
        pl.BlockSpec((None, tm, e), lambda bi, ti: (bi, ti, 0)),
        pl.BlockSpec((None, tm, e), lambda bi, ti: (bi, ti, 0)),
        pl.BlockSpec((None, tm, d), lambda bi, ti: (bi, ti, 0)),
        _mod_spec(layer, 5, d, row_of_batch, 2, 0),
    ]
    args = [ys, slot_t, wsel_t, x, mod]
    if final_g is not None:
        in_specs.append(pl.BlockSpec((1, d), lambda bi, ti: (0, 0)))
        args.append(final_g.reshape(1, d))
    return pl.pallas_call(
        functools.partial(_combine_body, final_norm=final_g is not None),
        grid=(b, t // tm),
        in_specs=in_specs,
        out_specs=pl.BlockSpec((None, tm, d), lambda bi, ti: (bi, ti, 0)),
        out_shape=jax.ShapeDtypeStruct((b, t, d), F32),
        compiler_params=_cparams(2, 48),
        name="moe_combine",
    )(*args)


def _combine_win_body(lo_ref, ys_ref, slot_ref, wsel_ref, x_ref, g2_ref, *rest, final_norm):
    if final_norm:
        fg_ref, o_ref, ywin_ref, scat_ref = rest
    else:
        o_ref, ywin_ref, scat_ref = rest
    n_exp, cap, d = ys_ref.shape
    tt = MOE_TILE
    tiles = x_ref.shape[0] // tt
    w = MOE_WINDOW
    i = pl.program_id(1)

    def finish(tok, acc):
        out = x_ref[tok, :] + g2_ref[...] * acc
        if final_norm:
            out = out * lax.rsqrt(jnp.mean(out * out, axis=-1, keepdims=True) + NORM_EPS) * fg_ref[...]
        o_ref[tok, :] = out

    for sub in range(tiles):
        tok = slice(sub * tt, (sub + 1) * tt)
        base = (pl.program_id(0) * (pl.num_programs(1) * tiles + 1) + i * tiles + sub) * n_exp
        starts, fits = _window_starts(lo_ref, base, n_exp, cap)

        @pl.when(fits)
        def _(tok=tok, starts=starts):
            rank = lax.broadcasted_iota(jnp.int32, (w, tt), 0)
            for e in range(n_exp):
                ywin_ref[e * w:(e + 1) * w, :] = ys_ref[e, pl.ds(starts[e], w), :]
                scat_ref[e * w:(e + 1) * w, :] = jnp.where(
                    slot_ref[e:e + 1, tok] - starts[e] == rank, wsel_ref[e:e + 1, tok], 0.0).astype(BF16)
            finish(tok, lax.dot_general(scat_ref[...], ywin_ref[...], TN_DIMS,
                                        preferred_element_type=F32))

        @pl.when(jnp.logical_not(fits))
        def _(tok=tok):
            rank = lax.broadcasted_iota(jnp.int32, (cap, tt), 0)
            acc = jnp.zeros((tt, d), F32)
            for e in range(n_exp):
                scat = jnp.where(slot_ref[e:e + 1, tok] == rank, wsel_ref[e:e + 1, tok], 0.0).astype(BF16)
                acc = acc + lax.dot_general(scat, ys_ref[e], TN_DIMS, preferred_element_type=F32)
            finish(tok, acc)


def _combine_windowed(ys, slot, wsel, tile_lo, x, mod, layer, row_of_batch, final_g=None):
    b, t, d = x.shape
    e = ys.shape[0]
    cap = ys.shape[1] // b
    rows = MOE_TILE * MOE_TILES_PER_STEP
    tile = lambda bi, ti, lo: (bi, ti, 0)
    routing = pl.BlockSpec((None, e, rows), lambda bi, ti, lo: (bi, 0, ti))
    in_specs = [
        pl.BlockSpec((e, cap, d), lambda bi, ti, lo: (0, bi, 0)),
        routing,
        routing,
        pl.BlockSpec((None, rows, d), tile),
        _mod_spec(layer, 5, d, row_of_batch, 2, 0),
    ]
    args = [ys, slot, wsel, x, mod]
    if final_g is not None:
        in_specs.append(pl.BlockSpec((1, d), lambda bi, ti, lo: (0, 0)))
        args.append(final_g.reshape(1, d))
    return pl.pallas_call(
        functools.partial(_combine_win_body, final_norm=final_g is not None),
        grid_spec=pltpu.PrefetchScalarGridSpec(
            num_scalar_prefetch=1,
            grid=(b, t // rows),
            in_specs=in_specs,
            out_specs=pl.BlockSpec((None, rows, d), tile),
            scratch_shapes=[pltpu.VMEM((e * MOE_WINDOW, d), BF16),
                            pltpu.VMEM((e * MOE_WINDOW, MOE_TILE), BF16)],
        ),
        out_shape=jax.ShapeDtypeStruct((b, t, d), F32),
        compiler_params=_cparams(2, 48),
        name="moe_combine_windowed",
    )(tile_lo, *args)


def _attn_body(sink_ref, q_ref, k_ref, v_ref, kc_ref, vc_ref, o_ref, kband_ref, vband_ref):
    blk = SWA_BLOCK
    hd = SWA_HEAD_DIM
    t_ctx = kc_ref.shape[0]
    n_heads = q_ref.shape[1] // hd
    group = n_heads // SWA_KV_HEADS
    blocks_per_step = q_ref.shape[0] // blk
    nb = pl.num_programs(1) * blocks_per_step
    band = t_ctx + 3 * blk
    kband_ref[0:t_ctx, :] = kc_ref[...]
    vband_ref[0:t_ctx, :] = vc_ref[...]
    qi = lax.broadcasted_iota(jnp.int32, (blk, blk), 0)
    kj = lax.broadcasted_iota(jnp.int32, (blk, blk), 1)
    head_idx = lax.broadcasted_iota(jnp.int32, (group, 1, 1), 0)
    neg = -jnp.inf

    def query_block(sub, carry):
        n = pl.program_id(1) * blocks_per_step + sub
        q0 = pl.multiple_of(sub * blk, blk)
        for i, st in enumerate((jnp.maximum(n - 1, 0), n, jnp.minimum(n + 1, nb - 1))):
            r0 = pl.multiple_of(st * blk, blk)
            kband_ref[t_ctx + i * blk:t_ctx + (i + 1) * blk, :] = k_ref[pl.ds(r0, blk), :]
            vband_ref[t_ctx + i * blk:t_ctx + (i + 1) * blk, :] = v_ref[pl.ds(r0, blk), :]
        bias_prev = (jnp.where(kj >= qi, 0.0, neg) + jnp.where(n > 0, 0.0, neg))[None]
        bias_next = (jnp.where(kj <= qi, 0.0, neg) + jnp.where(n < nb - 1, 0.0, neg))[None]
        for kv in range(SWA_KV_HEADS):
            heads = range(kv * group, (kv + 1) * group)
            qg = jnp.concatenate([q_ref[pl.ds(q0, blk), h * hd:(h + 1) * hd] for h in heads], axis=0)
            qg = qg * jnp.asarray(hd ** -0.5, BF16)
            kh = kband_ref[:, kv * hd:(kv + 1) * hd]
            vh = vband_ref[:, kv * hd:(kv + 1) * hd]
            s = lax.dot_general(qg, kh, NT_DIMS, preferred_element_type=F32).reshape(group, blk, band)
            s = jnp.concatenate([
                s[:, :, :t_ctx],
                s[:, :, t_ctx:t_ctx + blk] + bias_prev,
                s[:, :, t_ctx + blk:t_ctx + 2 * blk],
                s[:, :, t_ctx + 2 * blk:] + bias_next], axis=-1)
            sink = jnp.zeros((group, 1, 1), F32)
            for i, h in enumerate(heads):
                sink = jnp.where(head_idx == i, sink_ref[h], sink)
            m = jnp.maximum(jnp.max(s, axis=-1, keepdims=True), sink)
            p = jnp.exp(s - m)
            den = jnp.sum(p, axis=-1, keepdims=True) + jnp.exp(sink - m)
            o = jnp.dot(p.reshape(group * blk, band).astype(BF16), vh, preferred_element_type=F32)
            o = o.reshape(group, blk, hd) / den
            for i, h in enumerate(heads):
                o_ref[pl.ds(q0, blk), h * hd:(h + 1) * hd] = o[i].astype(o_ref.dtype)
        return carry

    lax.fori_loop(0, blocks_per_step, query_block, 0)


def _window_attention(p, pc, sink, *, blocks_per_step=4):
    b, t, cols = p.shape
    t_ctx = pc.shape[1]
    kvw = SWA_KV_HEADS * SWA_HEAD_DIM
    qw = cols - 2 * kvw
    band = t_ctx + 3 * SWA_BLOCK
    rows = blocks_per_step * SWA_BLOCK
    return pl.pallas_call(
        _attn_body,
        grid=(b, t // rows),
        in_specs=[
            pl.BlockSpec(memory_space=pltpu.SMEM),
            pl.BlockSpec((None, rows, qw), lambda bi, ni: (bi, ni, 0)),
            pl.BlockSpec((None, t, kvw), lambda bi, ni: (bi, 0, qw // kvw)),
            pl.BlockSpec((None, t, kvw), lambda bi, ni: (bi, 0, qw // kvw + 1)),
            pl.BlockSpec((None, t_ctx, kvw), lambda bi, ni: (bi, 0, 0)),
            pl.BlockSpec((None, t_ctx, kvw), lambda bi, ni: (bi, 0, 1)),
        ],
        out_specs=pl.BlockSpec((None, rows, qw), lambda bi, ni: (bi, ni, 0)),
        out_shape=jax.ShapeDtypeStruct((b, t, qw), BF16),
        scratch_shapes=[pltpu.VMEM((band, kvw), BF16), pltpu.VMEM((band, kvw), BF16)],
        compiler_params=_cparams(2, 32),
        name="window_attention",
    )(sink.astype(F32), p, p, p, pc, pc)


def _rope_tables(t):
    n_freq = SWA_HEAD_DIM // 4
    pos = jnp.arange(t)
    row = (pos // GRID_W).astype(F32)
    col = (pos % GRID_W).astype(F32)
    inv_freq = ROPE_BASE ** (-jnp.arange(n_freq, dtype=F32) / n_freq)
    ang_r, ang_c = row[:, None] * inv_freq, col[:, None] * inv_freq
    zero = jnp.zeros_like(ang_r)
    cos = jnp.concatenate([jnp.cos(ang_r)] * 2 + [jnp.cos(ang_c)] * 2, axis=1)
    sin_minus = jnp.concatenate([-jnp.sin(ang_r), zero, -jnp.sin(ang_c), zero], axis=1)
    sin_plus = jnp.concatenate([zero, jnp.sin(ang_r), zero, jnp.sin(ang_c)], axis=1)
    reps = LANES // SWA_HEAD_DIM
    return tuple(jnp.tile(tab, (1, reps)) for tab in (cos, sin_minus, sin_plus))


def _use_windows(t):
    cap = EC_CAPACITY * t // N_EXPERTS
    assert 2 * MOE_WINDOW == LANES
    return (t % (MOE_TILE * MOE_TILES_PER_STEP) == 0 and cap >= MOE_WINDOW
            and (cap - MOE_WINDOW) % BF16_ROWS == 0)


def _moe(streams, w_gate, w_up, w_down, layer):
    routed = []
    for hx, aff in streams:
        slot, wsel, tile_lo = _topk_select(aff)
        windowed = _use_windows(hx.shape[1])
        xs = _gather_windowed(slot, tile_lo, hx) if windowed else _gather(slot, hx)
        routed.append((xs, slot, wsel, tile_lo, windowed))
    ys_list = _expert_ffn([r[0] for r in routed], w_gate, w_up, w_down, layer)

    def make_combine(ys, slot, wsel, tile_lo, windowed):
        def combine(x, mod, row_of_batch, final_g=None):
            if windowed:
                return _combine_windowed(ys, slot, wsel, tile_lo, x, mod, layer, row_of_batch, final_g)
            return _combine(ys, jnp.swapaxes(slot, 1, 2), jnp.swapaxes(wsel, 1, 2), x, mod, layer,
                            row_of_batch, final_g)
        return combine

    return [make_combine(ys, *r[1:]) for ys, r in zip(ys_list, routed)]


def kernel(x, c, ctx, c_ctx, ada_w, ada_b, norm1_g, norm2_g, final_g, ret_w_in, ret_decay_f, ret_decay_b, ret_gn_f, ret_gn_b, ret_w_out, swa_w_qkv, swa_sink, swa_w_out, moe_router, moe_w_gate, moe_w_up, moe_w_down):
    b, t, d = x.shape
    lat_row = lambda bi: bi
    ctx_row = lambda bi: b
    cond = jnp.zeros((COND_ROWS, d), F32).at[:b].set(c).at[b].set(c_ctx)
    mod = _adaln(cond, ada_w, ada_b)

    ret_tn = 2048
    gate_col0 = (ret_w_in.shape[2] // 8) * 4
    assert gate_col0 % ret_tn == 0
    ret_proj = dict(tm=ROW_TILE, tn=ret_tn, silu_from_tile=gate_col0 // ret_tn)
    p = _norm_mod_matmul(x, norm1_g[0], mod, 0, lat_row, ret_w_in[0], **ret_proj)
    pc = _norm_mod_matmul(ctx, norm1_g[0], mod, 0, ctx_row, ret_w_in[0], **ret_proj)
    y, yc = _retention(p, pc, ret_decay_f[0], ret_decay_b[0], ret_gn_f[0], ret_gn_b[0])
    x1, hx, aff = _outproj_router(y, ret_w_out[0], x, mod, 0, lat_row, norm2_g[0], moe_router[0], tm=ROW_TILE)
    c1, hc, affc = _outproj_router(yc, ret_w_out[0], ctx, mod, 0, ctx_row, norm2_g[0], moe_router[0], tm=ROW_TILE)
    combine_lat, combine_ctx = _moe([(hx, aff), (hc, affc)], moe_w_gate, moe_w_up, moe_w_down, 0)
    x2 = combine_lat(x1, mod, lat_row)
    c2 = combine_ctx(c1, mod, ctx_row)

    qw = SWA_HEAD_DIM * (swa_w_qkv.shape[2] // SWA_HEAD_DIM - 2 * SWA_KV_HEADS)
    cos, sin_minus, sin_plus = _rope_tables(t)
    rope_groups = (qw + SWA_KV_HEADS * SWA_HEAD_DIM) // LANES
    p = _norm_mod_matmul(x2, norm1_g[1], mod, 1, lat_row, swa_w_qkv[0], tm=ROW_TILE,
                         tn=swa_w_qkv.shape[2], rope=(cos, sin_minus, sin_plus, rope_groups))
    w_kv = swa_w_qkv[0][:, qw:]
    pc = _norm_mod_matmul(c2, norm1_g[1], mod, 1, ctx_row, w_kv, tm=ROW_TILE, tn=w_kv.shape[1])
    o = _window_attention(p, pc, swa_sink[0])
    x3, hx, aff = _outproj_router(o, swa_w_out[0], x2, mod, 1, lat_row, norm2_g[1], moe_router[1], tm=ROW_TILE)
    (combine_lat,) = _moe([(hx, aff)], moe_w_gate, moe_w_up, moe_w_down, 1)
    return combine_lat(x3, mod, lat_row, final_g)
```

```python
import functools

import jax
import jax.numpy as jnp
from jax import lax
from jax.experimental import pallas as pl
from jax.experimental.pallas import tpu as pltpu

F32 = jnp.float32
BF16 = jnp.bfloat16

NORM_EPS = 1e-6
GRID_W = 64
RET_HEADS = 4
RET_CHUNK = 256
SWA_HEAD_DIM = 64
SWA_KV_HEADS = 2
SWA_WINDOW = 128
SWA_BLOCK = SWA_WINDOW
ROPE_BASE = 10000.0
N_EXPERTS = 16
EC_CAPACITY = 2

ROW_TILE = 1024
MOE_TILE = 256
MOE_WINDOW = 64
MOE_TILES_PER_STEP = 4

LANES = 128
BF16_ROWS = 16
MIB = 1024 * 1024
COND_ROWS = 16

NT_DIMS = (((1,), (1,)), ((), ()))
TN_DIMS = (((0,), (0,)), ((), ()))


def _cparams(n_axes, vmem_mib):
    return pltpu.CompilerParams(
        dimension_semantics=("arbitrary",) * n_axes,
        vmem_limit_bytes=vmem_mib * MIB,
    )


def _silu(v):
    half = 0.5 * v
    return half + half * jnp.tanh(half)


def _rmsnorm_mod(xv, g, shift, scale):
    h = xv * lax.rsqrt(jnp.mean(xv * xv, axis=-1, keepdims=True) + NORM_EPS) * g
    return h * (1.0 + scale) + shift


def _mod_spec(layer, chunk, d, row_of_batch, n_grid_axes, batch_axis):
    def index_map(*ids):
        return (layer, row_of_batch(ids[batch_axis]), 0, chunk)
    del n_grid_axes
    return pl.BlockSpec((None, None, 1, d), index_map)


def _adaln_body(cond_ref, w_ref, b_ref, o_ref):
    s = _silu(cond_ref[...])
    o_ref[...] = jnp.dot(s.astype(BF16), w_ref[...].astype(BF16),
                         preferred_element_type=F32) + b_ref[...]


def _adaln(cond, ada_w, ada_b):
    layers, d, n = ada_w.shape
    tn = d
    out = pl.pallas_call(
        _adaln_body,
        grid=(layers, n // tn),
        in_specs=[
            pl.BlockSpec((COND_ROWS, d), lambda l, j: (0, 0)),
            pl.BlockSpec((None, d, tn), lambda l, j: (l, 0, j)),
            pl.BlockSpec((None, 1, tn), lambda l, j: (l, 0, j)),
        ],
        out_specs=pl.BlockSpec((None, COND_ROWS, tn), lambda l, j: (l, 0, j)),
        out_shape=jax.ShapeDtypeStruct((layers, COND_ROWS, n), F32),
        compiler_params=_cparams(2, 32),
        name="adaln",
    )(cond, ada_w, ada_b.reshape(layers, 1, n))
    return out.reshape(layers, COND_ROWS, 1, n)


def _nmm_body(x_ref, g_ref, sh_ref, sc_ref, w_ref, *rest, rope_groups, silu_from_tile):
    if rope_groups:
        cos_ref, sm_ref, sp_ref, o_ref, wbf_ref = rest
    else:
        o_ref, wbf_ref = rest

    @pl.when((pl.program_id(1) == 0) & (pl.program_id(2) == 0))
    def _():
        wbf_ref[...] = w_ref[...].astype(BF16)

    def project():
        h = _rmsnorm_mod(x_ref[...], g_ref[...], sh_ref[...], sc_ref[...])
        return jnp.dot(h.astype(BF16), wbf_ref[...], preferred_element_type=F32)

    if silu_from_tile is not None:
        @pl.when(pl.program_id(0) < silu_from_tile)
        def _():
            o_ref[...] = project().astype(o_ref.dtype)

        @pl.when(pl.program_id(0) >= silu_from_tile)
        def _():
            o_ref[...] = _silu(project()).astype(o_ref.dtype)

        return
    acc = project()
    if not rope_groups:
        o_ref[...] = acc.astype(o_ref.dtype)
        return
    c, sm, sp = cos_ref[...], sm_ref[...], sp_ref[...]
    n_groups = acc.shape[1] // LANES
    for j in range(n_groups):
        a = acc[:, j * LANES:(j + 1) * LANES]
        if j < rope_groups:
            a = (a * c + pltpu.roll(a, LANES - 16, axis=1) * sm
                 + pltpu.roll(a, 16, axis=1) * sp)
        o_ref[:, j * LANES:(j + 1) * LANES] = a.astype(o_ref.dtype)


def _norm_mod_matmul(x, g, mod, layer, row_of_batch, w, *, tm, tn, rope=None, silu_from_tile=None):
    b, t, d = x.shape
    n = w.shape[1]
    tm = min(tm, t)
    grid = (n // tn, b, t // tm)
    in_specs = [
        pl.BlockSpec((None, tm, d), lambda j, bi, ti: (bi, ti, 0)),
        pl.BlockSpec((1, d), lambda j, bi, ti: (0, 0)),
        _mod_spec(layer, 0, d, row_of_batch, 3, 1),
        _mod_spec(layer, 1, d, row_of_batch, 3, 1),
        pl.BlockSpec((d, tn), lambda j, bi, ti: (0, j), pipeline_mode=pl.Buffered(1)),
    ]
    args = [x, g.reshape(1, d), mod, mod, w]
    rope_groups = 0
    if rope is not None:
        cos, sm, sp, rope_groups = rope
        for tab in (cos, sm, sp):
            in_specs.append(pl.BlockSpec((tm, LANES), lambda j, bi, ti: (ti, 0)))
            args.append(tab)
    return pl.pallas_call(
        functools.partial(_nmm_body, rope_groups=rope_groups, silu_from_tile=silu_from_tile),
        grid=grid,
        in_specs=in_specs,
        out_specs=pl.BlockSpec((None, tm, tn), lambda j, bi, ti: (bi, ti, j)),
        out_shape=jax.ShapeDtypeStruct((b, t, n), BF16),
        scratch_shapes=[pltpu.VMEM((d, tn), BF16)],
        compiler_params=_cparams(3, 48),
        name="norm_mod_proj",
    )(*args)


def _ret_body(dec_ref, gnf_ref, gnb_ref,
              qc_ref, kc_ref, vc_ref, gfc_ref, gbc_ref,
              q_ref, k_ref, v_ref, gf_ref, gb_ref,
              y_ref, yc_ref,
              statef_ref, stateb_ref, yacc_ref, intra_ref, qd_ref, kd_ref, cd_ref):
    c = RET_CHUNK
    dk = q_ref.shape[1]
    dv = v_ref.shape[1]
    t_ctx = qc_ref.shape[0]
    t_lat = q_ref.shape[0]

    dec = dec_ref[...]
    lg = jnp.minimum(dec, 0.0) - jnp.log1p(jnp.exp(-jnp.abs(dec)))
    row = lax.broadcasted_iota(jnp.int32, (c, c), 0)
    col = lax.broadcasted_iota(jnp.int32, (c, c), 1)
    pos_v = lax.broadcasted_iota(jnp.int32, (c, dv), 0).astype(F32)
    pos_k = lax.broadcasted_iota(jnp.int32, (c, dk), 0).astype(F32)
    for d in range(2):
        lg_c, lg_v, lg_k = lg[d:d + 1, :c], lg[d:d + 1, :], lg[d:d + 1, :dk]
        diff = (row - col) if d == 0 else (col - row)
        intra_ref[d] = jnp.where(diff >= 0, jnp.exp(jnp.maximum(diff, 0).astype(F32) * lg_c), 0.0)
        if d == 0:
            qd_ref[d] = jnp.exp((pos_v + 1.0) * lg_v)
            kd_ref[d] = jnp.exp((c - 1.0 - pos_k) * lg_k)
        else:
            qd_ref[d] = jnp.exp((c - pos_v) * lg_v)
            kd_ref[d] = jnp.exp(pos_k * lg_k)
        cd_ref[d] = jnp.exp(float(c) * lg_v)

    def chunk(d, refs, r0, yoff, out_ref, first_visit):
        qr, kr, vr, gr = refs
        state_ref = statef_ref if d == 0 else stateb_ref
        if not isinstance(r0, int):
            r0, yoff = pl.multiple_of(r0, c), pl.multiple_of(yoff, c)
        qs = qr[pl.ds(r0, c), :] * jnp.asarray(dk ** -0.5, BF16)
        kc = kr[pl.ds(r0, c), :]
        vc = vr[pl.ds(r0, c), :]
        s = lax.dot_general(qs, kc, NT_DIMS, preferred_element_type=F32) * intra_ref[d]
        st = state_ref[...]
        o = (jnp.dot(s.astype(BF16), vc, preferred_element_type=F32)
             + jnp.dot(qs, st.astype(BF16), preferred_element_type=F32) * qd_ref[d])
        kk = (kc.astype(F32) * kd_ref[d]).astype(BF16)
        state_ref[...] = cd_ref[d] * st + lax.dot_general(kk, vc, TN_DIMS, preferred_element_type=F32)
        mu = jnp.mean(o, axis=-1, keepdims=True)
        oc = o - mu
        var = jnp.mean(oc * oc, axis=-1, keepdims=True)
        gn = gnf_ref[...] if d == 0 else gnb_ref[...]
        y = gr[pl.ds(r0, c), :].astype(F32) * (oc * lax.rsqrt(var + NORM_EPS) * gn)
        if first_visit:
            yacc_ref[pl.ds(yoff, c), :] = y
        else:
            out_ref[pl.ds(r0, c), :] = (yacc_ref[pl.ds(yoff, c), :] + y).astype(out_ref.dtype)

    statef_ref[...] = jnp.zeros_like(statef_ref)
    stateb_ref[...] = jnp.zeros_like(stateb_ref)
    segments = (
        ((qc_ref, kc_ref, vc_ref), (gfc_ref, gbc_ref), t_ctx // c, 0, yc_ref),
        ((q_ref, k_ref, v_ref), (gf_ref, gb_ref), t_lat // c, t_ctx, y_ref),
    )
    for qkv, gates, n, ybase, out_ref in segments:
        half = n // 2

        def step(i, carry, first_visit, qkv=qkv, gates=gates, n=n, ybase=ybase, out_ref=out_ref):
            for d in range(2):
                ci = i if d == 0 else n - 1 - i
                chunk(d, qkv + (gates[d],), ci * c, ybase + ci * c, out_ref, first_visit)
            return carry

        unroll = 4 if half >= 4 else 1
        if half:
            lax.fori_loop(0, half, functools.partial(step, first_visit=True), 0, unroll=unroll)
        if n % 2:
            for d in range(2):
                chunk(d, qkv + (gates[d],), half * c, ybase + half * c, out_ref, d == 0)
        if half:
            lax.fori_loop(n - half, n, functools.partial(step, first_visit=False), 0, unroll=unroll)


def _retention(p, pc, decay_f, decay_b, gn_f, gn_b):
    b, t, cols = p.shape
    t_ctx = pc.shape[1]
    assert t % RET_CHUNK == 0 and t_ctx % RET_CHUNK == 0
    h = RET_HEADS
    dk = cols // (8 * h)
    dv = 2 * dk
    hv = h * dv
    dec = jnp.broadcast_to(jnp.stack([decay_f, decay_b], axis=1)[:, :, None], (h, 2, dv)).astype(F32)
    k_blk, v_blk, gf_blk, gb_blk = h, (2 * h * dk) // dv, (2 * h * dk + hv) // dv, (2 * h * dk + 2 * hv) // dv

    def specs(rows):
        return [
            pl.BlockSpec((None, rows, dk), lambda bi, hi: (bi, 0, hi)),
            pl.BlockSpec((None, rows, dk), lambda bi, hi: (bi, 0, k_blk + hi)),
            pl.BlockSpec((None, rows, dv), lambda bi, hi: (bi, 0, v_blk + hi)),
            pl.BlockSpec((None, rows, dv), lambda bi, hi: (bi, 0, gf_blk + hi)),
            pl.BlockSpec((None, rows, dv), lambda bi, hi: (bi, 0, gb_blk + hi)),
        ]

    gn_spec = pl.BlockSpec((1, dv), lambda bi, hi: (0, hi))
    return pl.pallas_call(
        _ret_body,
        grid=(b, h),
        in_specs=[pl.BlockSpec((None, 2, dv), lambda bi, hi: (hi, 0, 0)), gn_spec, gn_spec]
        + specs(t_ctx) + specs(t),
        out_specs=[
            pl.BlockSpec((None, t, dv), lambda bi, hi: (bi, 0, hi)),
            pl.BlockSpec((None, t_ctx, dv), lambda bi, hi: (bi, 0, hi)),
        ],
        out_shape=[
            jax.ShapeDtypeStruct((b, t, hv), BF16),
            jax.ShapeDtypeStruct((b, t_ctx, hv), BF16),
        ],
        scratch_shapes=[
            pltpu.VMEM((dk, dv), F32),
            pltpu.VMEM((dk, dv), F32),
            pltpu.VMEM((t_ctx + t, dv), F32),
            pltpu.VMEM((2, RET_CHUNK, RET_CHUNK), F32),
            pltpu.VMEM((2, RET_CHUNK, dv), F32),
            pltpu.VMEM((2, RET_CHUNK, dk), F32),
            pltpu.VMEM((2, 1, dv), F32),
        ],
        compiler_params=_cparams(2, 48),
        name="retention",
    )(dec, gn_f.reshape(1, hv), gn_b.reshape(1, hv), pc, pc, pc, pc, pc, p, p, p, p, p)


def _outproj_body(y_ref, w_ref, x_ref, g1_ref, ng_ref, sh_ref, sc_ref, wr_ref,
                  x1_ref, hx_ref, aff_ref, wbf_ref):
    @pl.when((pl.program_id(0) == 0) & (pl.program_id(1) == 0))
    def _():
        wbf_ref[...] = w_ref[...].astype(BF16)

    x1 = x_ref[...] + g1_ref[...] * jnp.dot(y_ref[...], wbf_ref[...], preferred_element_type=F32)
    x1_ref[...] = x1
    h = _rmsnorm_mod(x1, ng_ref[...], sh_ref[...], sc_ref[...])
    hb = h.astype(BF16)
    hx_ref[...] = hb
    hlo = (h - hb.astype(F32)).astype(BF16)
    wr = wr_ref[...]
    whi = wr.astype(BF16)
    wlo = (wr - whi.astype(F32)).astype(BF16)
    logits = (lax.dot_general(whi, hb, NT_DIMS, preferred_element_type=F32)
              + lax.dot_general(whi, hlo, NT_DIMS, preferred_element_type=F32)
              + lax.dot_general(wlo, hb, NT_DIMS, preferred_element_type=F32)
              + lax.dot_general(wlo, hlo, NT_DIMS, preferred_element_type=F32))
    e = jnp.exp(logits - jnp.max(logits, axis=0, keepdims=True))
    aff_ref[...] = e / jnp.sum(e, axis=0, keepdims=True)


def _outproj_router(y, w, x, mod, layer, row_of_batch, norm_g, w_router, *, tm):
    b, t, kdim = y.shape
    d = w.shape[1]
    e = w_router.shape[1]
    tm = min(tm, t)
    row = lambda bi, ti: (0, 0)
    return pl.pallas_call(
        _outproj_body,
        grid=(b, t // tm),
        in_specs=[
            pl.BlockSpec((None, tm, kdim), lambda bi, ti: (bi, ti, 0)),
            pl.BlockSpec((kdim, d), row, pipeline_mode=pl.Buffered(1)),
            pl.BlockSpec((None, tm, d), lambda bi, ti: (bi, ti, 0)),
            _mod_spec(layer, 2, d, row_of_batch, 2, 0),
            pl.BlockSpec((1, d), row),
            _mod_spec(layer, 3, d, row_of_batch, 2, 0),
            _mod_spec(layer, 4, d, row_of_batch, 2, 0),
            pl.BlockSpec((e, d), row),
        ],
        out_specs=[
            pl.BlockSpec((None, tm, d), lambda bi, ti: (bi, ti, 0)),
            pl.BlockSpec((None, tm, d), lambda bi, ti: (bi, ti, 0)),
            pl.BlockSpec((None, e, tm), lambda bi, ti: (bi, 0, ti)),
        ],
        out_shape=[
            jax.ShapeDtypeStruct((b, t, d), F32),
            jax.ShapeDtypeStruct((b, t, d), BF16),
            jax.ShapeDtypeStruct((b, e, t), F32),
        ],
        scratch_shapes=[pltpu.VMEM((kdim, d), BF16)],
        compiler_params=_cparams(2, 48),
        name="outproj_router",
    )(y, w, x, mod, norm_g.reshape(1, d), mod, mod, w_router.T)


def _topk_body(aff_ref, slot_ref, wsel_ref, lo_ref, *, cap):
    a = aff_ref[...]
    r, t = a.shape
    capf = float(cap)

    def count_ge(v):
        return jnp.sum(jnp.where(a >= v, 1.0, 0.0), axis=1, keepdims=True)

    def bisect(_, lohi):
        lo, hi = lohi
        mid = 0.5 * (lo + hi)
        ok = count_ge(mid) >= capf
        return jnp.where(ok, mid, lo), jnp.where(ok, hi, mid)

    lo, hi = lax.fori_loop(0, 32, bisect, (jnp.zeros((r, 1), F32), jnp.full((r, 1), 2.0, F32)))

    def unfinished(state):
        return jnp.min(state[2]) < 0.5

    def walk(state):
        lo, hi, done = state
        top = jnp.max(jnp.where(a < hi, a, -1.0), axis=1, keepdims=True)
        found = (count_ge(top) >= capf) & (done < 0.5)
        moved = (done < 0.5) & jnp.logical_not(found)
        return jnp.where(found, top, lo), jnp.where(moved, top, hi), jnp.where(found, 1.0, done)

    thr, _, _ = lax.while_loop(unfinished, walk, (lo, hi, jnp.zeros((r, 1), F32)))

    gt = a > thr
    eq = a == thr
    need = capf - jnp.sum(jnp.where(gt, 1.0, 0.0), axis=1, keepdims=True)
    upper = jnp.where(lax.broadcasted_iota(jnp.int32, (LANES, LANES), 0)
                      < lax.broadcasted_iota(jnp.int32, (LANES, LANES), 1), 1.0, 0.0).astype(BF16)
    carry_eq = jnp.zeros((r, 1), F32)
    carry_sel = jnp.zeros((r, 1), F32)
    lane = lax.broadcasted_iota(jnp.int32, (r, LANES), 1)
    tile_lo = jnp.zeros((r, LANES), F32)
    for j in range(t // LANES):
        sl = slice(j * LANES, (j + 1) * LANES)
        if (j * LANES) % MOE_TILE == 0:
            tile_lo = jnp.where(lane == (j * LANES) // MOE_TILE, carry_sel, tile_lo)
        eqf = jnp.where(eq[:, sl], 1.0, 0.0)
        pre_eq = jnp.dot(eqf.astype(BF16), upper, preferred_element_type=F32) + carry_eq
        self_ = jnp.where(gt[:, sl], 1.0, jnp.where(pre_eq < need, eqf, 0.0))
        pre_sel = jnp.dot(self_.astype(BF16), upper, preferred_element_type=F32) + carry_sel
        sel = self_ > 0.5
        slot_ref[:, sl] = jnp.where(sel, pre_sel, -1.0).astype(jnp.int32)
        wsel_ref[:, sl] = jnp.where(sel, a[:, sl], 0.0)
        carry_eq = carry_eq + jnp.sum(eqf, axis=1, keepdims=True)
        carry_sel = carry_sel + jnp.sum(self_, axis=1, keepdims=True)
    tile_lo = jnp.where(lane == t // MOE_TILE, carry_sel, tile_lo)
    lo_ref[...] = tile_lo.astype(jnp.int32)


def _topk_select(aff):
    b, e, t = aff.shape
    cap = EC_CAPACITY * t // N_EXPERTS
    r = b * e
    n_tiles = t // MOE_TILE
    assert t % MOE_TILE == 0 and n_tiles < LANES
    full = pl.BlockSpec((r, t), lambda i: (0, 0))
    slot, wsel, lo = pl.pallas_call(
        functools.partial(_topk_body, cap=cap),
        grid=(1,),
        in_specs=[full],
        out_specs=[full, full, pl.BlockSpec((r, LANES), lambda i: (0, 0))],
        out_shape=[jax.ShapeDtypeStruct((r, t), jnp.int32), jax.ShapeDtypeStruct((r, t), F32),
                   jax.ShapeDtypeStruct((r, LANES), jnp.int32)],
        compiler_params=_cparams(1, 32),
        name="topk_select",
    )(aff.reshape(r, t))
    tile_lo = jnp.swapaxes(lo.reshape(b, e, LANES)[:, :, :n_tiles + 1], 1, 2).reshape(-1)
    return slot.reshape(b, e, t), wsel.reshape(b, e, t), tile_lo


def _gather_body(slot_ref, h_ref, o_ref, onehot_ref):
    n_exp, cap, d = o_ref.shape
    t = h_ref.shape[0]
    rank = lax.broadcasted_iota(jnp.int32, (cap, t), 0)
    for e in range(n_exp):
        onehot_ref[e * cap:(e + 1) * cap, :] = jnp.where(
            slot_ref[e:e + 1, :] == rank, 1.0, 0.0).astype(BF16)
    xs = jnp.dot(onehot_ref[...], h_ref[...], preferred_element_type=F32)
    o_ref[...] = xs.reshape(n_exp, cap, d).astype(o_ref.dtype)


def _gather(slot, hx, *, experts_per_step=8):
    b, e, t = slot.shape
    d = hx.shape[2]
    cap = EC_CAPACITY * t // N_EXPERTS
    eg = experts_per_step
    return pl.pallas_call(
        _gather_body,
        grid=(b, e // eg),
        in_specs=[
            pl.BlockSpec((None, eg, t), lambda bi, gi: (bi, gi, 0)),
            pl.BlockSpec((None, t, d), lambda bi, gi: (bi, 0, 0)),
        ],
        out_specs=pl.BlockSpec((eg, cap, d), lambda bi, gi: (gi, bi, 0)),
        out_shape=jax.ShapeDtypeStruct((e, b * cap, d), BF16),
        scratch_shapes=[pltpu.VMEM((eg * cap, t), BF16)],
        compiler_params=_cparams(2, 48),
        name="moe_gather",
    )(slot, hx)


def _window_starts(lo_ref, base, n_exp, cap):
    starts = []
    fits = None
    for e in range(n_exp):
        lo = lo_ref[base + e]
        hi = lo_ref[base + n_exp + e]
        start = jnp.minimum((lo // BF16_ROWS) * BF16_ROWS, cap - MOE_WINDOW)
        starts.append(pl.multiple_of(start, BF16_ROWS))
        ok = hi - start <= MOE_WINDOW
        fits = ok if fits is None else fits & ok
    return starts, fits


def _gather_win_body(lo_ref, slot_ref, h_ref, o_ref, onehot_ref):
    n_exp, cap, d = o_ref.shape
    tt = MOE_TILE
    tiles = h_ref.shape[0] // tt
    w = MOE_WINDOW
    i = pl.program_id(1)

    @pl.when(i == 0)
    def _():
        o_ref[...] = jnp.zeros_like(o_ref)

    for sub in range(tiles):
        tok = slice(sub * tt, (sub + 1) * tt)
        base = (pl.program_id(0) * (pl.num_programs(1) * tiles + 1) + i * tiles + sub) * n_exp
        starts, fits = _window_starts(lo_ref, base, n_exp, cap)

        @pl.when(fits)
        def _(tok=tok, starts=starts):
            rank = lax.broadcasted_iota(jnp.int32, (w, tt), 0)
            for e in range(n_exp):
                onehot_ref[e * w:(e + 1) * w, :] = jnp.where(
                    slot_ref[e:e + 1, tok] - starts[e] == rank, 1.0, 0.0).astype(BF16)
            rows = jnp.dot(onehot_ref[...], h_ref[tok, :], preferred_element_type=F32)
            for e in range(n_exp):
                o_ref[e, pl.ds(starts[e], w), :] += rows[e * w:(e + 1) * w].astype(o_ref.dtype)

        @pl.when(jnp.logical_not(fits))
        def _(tok=tok):
            rank = lax.broadcasted_iota(jnp.int32, (cap, tt), 0)
            for e in range(n_exp):
                onehot = jnp.where(slot_ref[e:e + 1, tok] == rank, 1.0, 0.0).astype(BF16)
                o_ref[e] += jnp.dot(onehot, h_ref[tok, :], preferred_element_type=F32).astype(o_ref.dtype)


def _gather_windowed(slot, tile_lo, hx):
    b, e, t = slot.shape
    d = hx.shape[2]
    cap = EC_CAPACITY * t // N_EXPERTS
    rows = MOE_TILE * MOE_TILES_PER_STEP
    return pl.pallas_call(
        _gather_win_body,
        grid_spec=pltpu.PrefetchScalarGridSpec(
            num_scalar_prefetch=1,
            grid=(b, t // rows),
            in_specs=[
                pl.BlockSpec((None, e, rows), lambda bi, ti, lo: (bi, 0, ti)),
                pl.BlockSpec((None, rows, d), lambda bi, ti, lo: (bi, ti, 0)),
            ],
            out_specs=pl.BlockSpec((e, cap, d), lambda bi, ti, lo: (0, bi, 0)),
            scratch_shapes=[pltpu.VMEM((e * MOE_WINDOW, MOE_TILE), BF16)],
        ),
        out_shape=jax.ShapeDtypeStruct((e, b * cap, d), BF16),
        compiler_params=_cparams(2, 48),
        name="moe_gather_windowed",
    )(tile_lo, slot, hx)


def _ffn_body(*refs, n_streams, row_chunk):
    xs_refs = refs[:n_streams]
    wg_ref, wu_ref, wd_ref = refs[n_streams:n_streams + 3]
    ys_refs = refs[n_streams + 3:2 * n_streams + 3]
    acc_refs = refs[2 * n_streams + 3:3 * n_streams + 3]
    wgb_ref, wub_ref, wdb_ref = refs[3 * n_streams + 3:]
    f = pl.program_id(1)
    wgb_ref[...] = wg_ref[...].astype(BF16)
    wub_ref[...] = wu_ref[...].astype(BF16)
    wdb_ref[...] = wd_ref[...].astype(BF16)
    for xs_ref, ys_ref, acc_ref in zip(xs_refs, ys_refs, acc_refs):
        rc = min(row_chunk, xs_ref.shape[0])

        @pl.when(f == 0)
        def _(acc_ref=acc_ref):
            acc_ref[...] = jnp.zeros_like(acc_ref)

        def rows(i, carry, xs_ref=xs_ref, acc_ref=acc_ref, rc=rc):
            r0 = pl.multiple_of(i * rc, rc)
            xb = xs_ref[pl.ds(r0, rc), :]
            g = jnp.dot(xb, wgb_ref[...], preferred_element_type=F32)
            u = jnp.dot(xb, wub_ref[...], preferred_element_type=F32)
            hid = (_silu(g) * u).astype(BF16)
            acc_ref[pl.ds(r0, rc), :] += jnp.dot(hid, wdb_ref[...], preferred_element_type=F32)
            return carry

        lax.fori_loop(0, xs_ref.shape[0] // rc, rows, 0)

        @pl.when(f == pl.num_programs(1) - 1)
        def _(ys_ref=ys_ref, acc_ref=acc_ref):
            ys_ref[...] = acc_ref[...].astype(ys_ref.dtype)


def _expert_ffn(xs_list, w_gate, w_up, w_down, layer, *, tf=512, row_chunk=1024):
    _, e, d, ff = w_gate.shape
    n = len(xs_list)
    xs_specs = [pl.BlockSpec((None, xs.shape[1], d), lambda ei, fi: (ei, 0, 0)) for xs in xs_list]
    return pl.pallas_call(
        functools.partial(_ffn_body, n_streams=n, row_chunk=row_chunk),
        grid=(e, ff // tf),
        in_specs=xs_specs + [
            pl.BlockSpec((None, None, d, tf), lambda ei, fi: (layer, ei, 0, fi)),
            pl.BlockSpec((None, None, d, tf), lambda ei, fi: (layer, ei, 0, fi)),
            pl.BlockSpec((None, None, tf, d), lambda ei, fi: (layer, ei, fi, 0)),
        ],
        out_specs=xs_specs,
        out_shape=[jax.ShapeDtypeStruct(xs.shape, BF16) for xs in xs_list],
        scratch_shapes=[pltpu.VMEM(xs.shape[1:], F32) for xs in xs_list]
        + [pltpu.VMEM((d, tf), BF16), pltpu.VMEM((d, tf), BF16), pltpu.VMEM((tf, d), BF16)],
        compiler_params=_cparams(2, 56),
        name="expert_ffn",
    )(*xs_list, w_gate, w_up, w_down)


def _combine_body(ys_ref, slot_ref, wsel_ref, x_ref, g2_ref, *rest, final_norm):
    if final_norm:
        fg_ref, o_ref = rest
    else:
        (o_ref,) = rest
    n_exp, cap, _ = ys_ref.shape
    tm = x_ref.shape[0]
    rank = lax.broadcasted_iota(jnp.int32, (tm, cap), 1)
    slots = slot_ref[...]
    wsel = wsel_ref[...]
    o_ref[...] = jnp.zeros_like(o_ref)
    for e in range(n_exp):
        onehot = jnp.where(slots[:, e:e + 1] == rank, 1.0, 0.0).astype(BF16)
        o_ref[...] += wsel[:, e:e + 1] * jnp.dot(onehot, ys_ref[e], preferred_element_type=F32)
    out = x_ref[...] + g2_ref[...] * o_ref[...]
    if final_norm:
        out = out * lax.rsqrt(jnp.mean(out * out, axis=-1, keepdims=True) + NORM_EPS) * fg_ref[...]
    o_ref[...] = out


def _combine(ys, slot_t, wsel_t, x, mod, layer, row_of_batch, final_g=None, *, tm=ROW_TILE):
    b, t, d = x.shape
    e = ys.shape[0]
    cap = ys.shape[1] // b
    tm = min(tm, t)
    in_specs = [
        pl.BlockSpec((e, cap, d), lambda bi, ti: (0, bi, 0)),
        pl.BlockSpec((None, tm, e), lambda bi, ti: (bi, ti, 0)),
        pl.BlockSpec((None, tm, e), lambda bi, ti: (bi, ti, 0)),
        pl.BlockSpec((None, tm, d), lambda bi, ti: (bi, ti, 0)),
        _mod_spec(layer, 5, d, row_of_batch, 2, 0),
    ]
    args = [ys, slot_t, wsel_t, x, mod]
    if final_g is not None:
        in_specs.append(pl.BlockSpec((1, d), lambda bi, ti: (0, 0)))
        args.append(final_g.reshape(1, d))
    return pl.pallas_call(
        functools.partial(_combine_body, final_norm=final_g is not None),
        grid=(b, t // tm),
        in_specs=in_specs,
        out_specs=pl.BlockSpec((None, tm, d), lambda bi, ti: (bi, ti, 0)),
        out_shape=jax.ShapeDtypeStruct((b, t, d), F32),
        compiler_params=_cparams(2, 48),
        name="moe_combine",
    )(*args)


def _combine_win_body(lo_ref, ys_ref, slot_ref, wsel_ref, x_ref, g2_ref, *rest, final_norm):
    if final_norm:
        fg_ref, o_ref, ywin_ref, scat_ref = rest
    else:
        o_ref, ywin_ref, scat_ref = rest
    n_exp, cap, d = ys_ref.shape
    tt = MOE_TILE
    tiles = x_ref.shape[0] // tt
    w = MOE_WINDOW
    i = pl.program_id(1)

    def finish(tok, acc):
        out = x_ref[tok, :] + g2_ref[...] * acc
        if final_norm:
            out = out * lax.rsqrt(jnp.mean(out * out, axis=-1, keepdims=True) + NORM_EPS) * fg_ref[...]
        o_ref[tok, :] = out

    for sub in range(tiles):
        tok = slice(sub * tt, (sub + 1) * tt)
        base = (pl.program_id(0) * (pl.num_programs(1) * tiles + 1) + i * tiles + sub) * n_exp
        starts, fits = _window_starts(lo_ref, base, n_exp, cap)

        @pl.when(fits)
        def _(tok=tok, starts=starts):
            rank = lax.broadcasted_iota(jnp.int32, (w, tt), 0)
            for e in range(n_exp):
                ywin_ref[e * w:(e + 1) * w, :] = ys_ref[e, pl.ds(starts[e], w), :]
                scat_ref[e * w:(e + 1) * w, :] = jnp.where(
                    slot_ref[e:e + 1, tok] - starts[e] == rank, wsel_ref[e:e + 1, tok], 0.0).astype(BF16)
            finish(tok, lax.dot_general(scat_ref[...], ywin_ref[...], TN_DIMS,
                                        preferred_element_type=F32))

        @pl.when(jnp.logical_not(fits))
        def _(tok=tok):
            rank = lax.broadcasted_iota(jnp.int32, (cap, tt), 0)
            acc = jnp.zeros((tt, d), F32)
            for e in range(n_exp):
                scat = jnp.where(slot_ref[e:e + 1, tok] == rank, wsel_ref[e:e + 1, tok], 0.0).astype(BF16)
                acc = acc + lax.dot_general(scat, ys_ref[e], TN_DIMS, preferred_element_type=F32)
            finish(tok, acc)


def _combine_windowed(ys, slot, wsel, tile_lo, x, mod, layer, row_of_batch, final_g=None):
    b, t, d = x.shape
    e = ys.shape[0]
    cap = ys.shape[1] // b
    rows = MOE_TILE * MOE_TILES_PER_STEP
    tile = lambda bi, ti, lo: (bi, ti, 0)
    routing = pl.BlockSpec((None, e, rows), lambda bi, ti, lo: (bi, 0, ti))
    in_specs = [
        pl.BlockSpec((e, cap, d), lambda bi, ti, lo: (0, bi, 0)),
        routing,
        routing,
        pl.BlockSpec((None, rows, d), tile),
        _mod_spec(layer, 5, d, row_of_batch, 2, 0),
    ]
    args = [ys, slot, wsel, x, mod]
    if final_g is not None:
        in_specs.append(pl.BlockSpec((1, d), lambda bi, ti, lo: (0, 0)))
        args.append(final_g.reshape(1, d))
    return pl.pallas_call(
        functools.partial(_combine_win_body, final_norm=final_g is not None),
        grid_spec=pltpu.PrefetchScalarGridSpec(
            num_scalar_prefetch=1,
            grid=(b, t // rows),
            in_specs=in_specs,
            out_specs=pl.BlockSpec((None, rows, d), tile),
            scratch_shapes=[pltpu.VMEM((e * MOE_WINDOW, d), BF16),
                            pltpu.VMEM((e * MOE_WINDOW, MOE_TILE), BF16)],
        ),
        out_shape=jax.ShapeDtypeStruct((b, t, d), F32),
        compiler_params=_cparams(2, 48),
        name="moe_combine_windowed",
    )(tile_lo, *args)


def _attn_body(sink_ref, q_ref, k_ref, v_ref, kc_ref, vc_ref, o_ref):
    blk = SWA_BLOCK
    hd = SWA_HEAD_DIM
    win = 3 * blk
    t = k_ref.shape[0]
    t_ctx = kc_ref.shape[0]
    n_heads = q_ref.shape[1] // hd
    group = n_heads // SWA_KV_HEADS
    blocks_per_step = q_ref.shape[0] // blk
    rel = (lax.broadcasted_iota(jnp.int32, (blk, win), 0)
           - lax.broadcasted_iota(jnp.int32, (blk, win), 1))
    head_idx = lax.broadcasted_iota(jnp.int32, (group, 1, 1), 0)

    def query_block(sub, carry):
        n = pl.program_id(1) * blocks_per_step + sub
        q0 = pl.multiple_of(sub * blk, blk)
        w0 = pl.multiple_of(jnp.clip((n - 1) * blk, 0, t - win), blk)
        dist = rel + (n * blk - w0)
        bias = jnp.where(jnp.abs(dist) <= SWA_WINDOW, 0.0, -jnp.inf)[None]
        for kv in range(SWA_KV_HEADS):
            heads = range(kv * group, (kv + 1) * group)
            cols = slice(kv * hd, (kv + 1) * hd)
            qg = jnp.concatenate([q_ref[pl.ds(q0, blk), h * hd:(h + 1) * hd] for h in heads], axis=0)
            qg = qg * jnp.asarray(hd ** -0.5, BF16)
            s_ctx = lax.dot_general(qg, kc_ref[:, cols], NT_DIMS, preferred_element_type=F32)
            s_win = lax.dot_general(qg, k_ref[pl.ds(w0, win), cols], NT_DIMS, preferred_element_type=F32)
            s = jnp.concatenate([s_ctx.reshape(group, blk, t_ctx),
                                 s_win.reshape(group, blk, win) + bias], axis=-1)
            sink = jnp.zeros((group, 1, 1), F32)
            for i, h in enumerate(heads):
                sink = jnp.where(head_idx == i, sink_ref[h], sink)
            m = jnp.maximum(jnp.max(s, axis=-1, keepdims=True), sink)
            p = jnp.exp(s - m)
            den = jnp.sum(p, axis=-1, keepdims=True) + jnp.exp(sink - m)
            pb = p.reshape(group * blk, t_ctx + win).astype(BF16)
            o = (jnp.dot(pb[:, :t_ctx], vc_ref[:, cols], preferred_element_type=F32)
                 + jnp.dot(pb[:, t_ctx:], v_ref[pl.ds(w0, win), cols], preferred_element_type=F32))
            o = o.reshape(group, blk, hd) / den
            for i, h in enumerate(heads):
                o_ref[pl.ds(q0, blk), h * hd:(h + 1) * hd] = o[i].astype(o_ref.dtype)
        return carry

    lax.fori_loop(0, blocks_per_step, query_block, 0, unroll=2)


def _window_attention(p, pc, sink, *, blocks_per_step=4):
    b, t, cols = p.shape
    t_ctx = pc.shape[1]
    kvw = SWA_KV_HEADS * SWA_HEAD_DIM
    qw = cols - 2 * kvw
    assert t >= 3 * SWA_BLOCK
    rows = blocks_per_step * SWA_BLOCK
    return pl.pallas_call(
        _attn_body,
        grid=(b, t // rows),
        in_specs=[
            pl.BlockSpec(memory_space=pltpu.SMEM),
            pl.BlockSpec((None, rows, qw), lambda bi, ni: (bi, ni, 0)),
            pl.BlockSpec((None, t, kvw), lambda bi, ni: (bi, 0, qw // kvw)),
            pl.BlockSpec((None, t, kvw), lambda bi, ni: (bi, 0, qw // kvw + 1)),
            pl.BlockSpec((None, t_ctx, kvw), lambda bi, ni: (bi, 0, 0)),
            pl.BlockSpec((None, t_ctx, kvw), lambda bi, ni: (bi, 0, 1)),
        ],
        out_specs=pl.BlockSpec((None, rows, qw), lambda bi, ni: (bi, ni, 0)),
        out_shape=jax.ShapeDtypeStruct((b, t, qw), BF16),
        compiler_params=_cparams(2, 32),
        name="window_attention",
    )(sink.astype(F32), p, p, p, pc, pc)


def _rope_tables(t):
    n_freq = SWA_HEAD_DIM // 4
    pos = jnp.arange(t)
    row = (pos // GRID_W).astype(F32)
    col = (pos % GRID_W).astype(F32)
    inv_freq = ROPE_BASE ** (-jnp.arange(n_freq, dtype=F32) / n_freq)
    ang_r, ang_c = row[:, None] * inv_freq, col[:, None] * inv_freq
    zero = jnp.zeros_like(ang_r)
    cos = jnp.concatenate([jnp.cos(ang_r)] * 2 + [jnp.cos(ang_c)] * 2, axis=1)
    sin_minus = jnp.concatenate([-jnp.sin(ang_r), zero, -jnp.sin(ang_c), zero], axis=1)
    sin_plus = jnp.concatenate([zero, jnp.sin(ang_r), zero, jnp.sin(ang_c)], axis=1)
    reps = LANES // SWA_HEAD_DIM
    return tuple(jnp.tile(tab, (1, reps)) for tab in (cos, sin_minus, sin_plus))


def _use_windows(t):
    cap = EC_CAPACITY * t // N_EXPERTS
    assert 2 * MOE_WINDOW == LANES
    return (t % (MOE_TILE * MOE_TILES_PER_STEP) == 0 and cap >= MOE_WINDOW
            and (cap - MOE_WINDOW) % BF16_ROWS == 0)


def _moe(streams, w_gate, w_up, w_down, layer):
    routed = []
    for hx, aff in streams:
        slot, wsel, tile_lo = _topk_select(aff)
        windowed = _use_windows(hx.shape[1])
        xs = _gather_windowed(slot, tile_lo, hx) if windowed else _gather(slot, hx)
        routed.append((xs, slot, wsel, tile_lo, windowed))
    ys_list = _expert_ffn([r[0] for r in routed], w_gate, w_up, w_down, layer)

    def make_combine(ys, slot, wsel, tile_lo, windowed):
        def combine(x, mod, row_of_batch, final_g=None):
            if windowed:
                return _combine_windowed(ys, slot, wsel, tile_lo, x, mod, layer, row_of_batch, final_g)
            return _combine(ys, jnp.swapaxes(slot, 1, 2), jnp.swapaxes(wsel, 1, 2), x, mod, layer,
                            row_of_batch, final_g)
        return combine

    return [make_combine(ys, *r[1:]) for ys, r in zip(ys_list, routed)]


def kernel(x, c, ctx, c_ctx, ada_w, ada_b, norm1_g, norm2_g, final_g, ret_w_in, ret_decay_f, ret_decay_b, ret_gn_f, ret_gn_b, ret_w_out, swa_w_qkv, swa_sink, swa_w_out, moe_router, moe_w_gate, moe_w_up, moe_w_down):
    b, t, d = x.shape
    lat_row = lambda bi: bi
    ctx_row = lambda bi: b
    cond = jnp.zeros((COND_ROWS, d), F32).at[:b].set(c).at[b].set(c_ctx)
    mod = _adaln(cond, ada_w, ada_b)

    ret_tn = 2048
    gate_col0 = (ret_w_in.shape[2] // 8) * 4
    assert gate_col0 % ret_tn == 0
    ret_proj = dict(tm=ROW_TILE, tn=ret_tn, silu_from_tile=gate_col0 // ret_tn)
    p = _norm_mod_matmul(x, norm1_g[0], mod, 0, lat_row, ret_w_in[0], **ret_proj)
    t_ctx = ctx.shape[1]
    flat = lambda a: a.reshape(1, b * t_ctx, a.shape[-1])
    unflat = lambda a: a.reshape(b, t_ctx, a.shape[-1])
    pc = unflat(_norm_mod_matmul(flat(ctx), norm1_g[0], mod, 0, ctx_row, ret_w_in[0], **ret_proj))
    y, yc = _retention(p, pc, ret_decay_f[0], ret_decay_b[0], ret_gn_f[0], ret_gn_b[0])
    x1, hx, aff = _outproj_router(y, ret_w_out[0], x, mod, 0, lat_row, norm2_g[0], moe_router[0], tm=ROW_TILE)
    c1, hc, affc = _outproj_router(flat(yc), ret_w_out[0], flat(ctx), mod, 0, ctx_row, norm2_g[0],
                                   moe_router[0], tm=ROW_TILE)
    c1, hc = unflat(c1), unflat(hc)
    affc = jnp.swapaxes(affc.reshape(affc.shape[1], b, t_ctx), 0, 1)
    combine_lat, combine_ctx = _moe([(hx, aff), (hc, affc)], moe_w_gate, moe_w_up, moe_w_down, 0)
    x2 = combine_lat(x1, mod, lat_row)
    c2 = combine_ctx(c1, mod, ctx_row)

    qw = SWA_HEAD_DIM * (swa_w_qkv.shape[2] // SWA_HEAD_DIM - 2 * SWA_KV_HEADS)
    cos, sin_minus, sin_plus = _rope_tables(t)
    rope_groups = (qw + SWA_KV_HEADS * SWA_HEAD_DIM) // LANES
    p = _norm_mod_matmul(x2, norm1_g[1], mod, 1, lat_row, swa_w_qkv[0], tm=ROW_TILE,
                         tn=swa_w_qkv.shape[2], rope=(cos, sin_minus, sin_plus, rope_groups))
    w_kv = swa_w_qkv[0][:, qw:]
    pc = unflat(_norm_mod_matmul(flat(c2), norm1_g[1], mod, 1, ctx_row, w_kv, tm=ROW_TILE, tn=w_kv.shape[1]))
    o = _window_attention(p, pc, swa_sink[0])
    x3, hx, aff = _outproj_router(o, swa_w_out[0], x2, mod, 1, lat_row, norm2_g[1], moe_router[1], tm=ROW_TILE)
    (combine_lat,) = _moe([(hx, aff)], moe_w_gate, moe_w_up, moe_w_down, 1)
    return combine_lat(x3, mod, lat_row, final_g)
```

```python
import functools

import jax
import jax.numpy as jnp
from jax import lax
from jax.experimental import pallas as pl
from jax.experimental.pallas import tpu as pltpu

F32 = jnp.float32
BF16 = jnp.bfloat16

NORM_EPS = 1e-6
GRID_W = 64
RET_HEADS = 4
RET_CHUNK = 256
SWA_HEAD_DIM = 64
SWA_KV_HEADS = 2
SWA_WINDOW = 128
SWA_BLOCK = SWA_WINDOW
ROPE_BASE = 10000.0
N_EXPERTS = 16
EC_CAPACITY = 2

ROW_TILE = 1024
MOE_TILE = 256
MOE_WINDOW = 64
MOE_TILES_PER_STEP = 4

LANES = 128
BF16_ROWS = 16
MIB = 1024 * 1024
COND_ROWS = 16

LOG2E = 1.4426950408889634

NT_DIMS = (((1,), (1,)), ((), ()))
TN_DIMS = (((0,), (0,)), ((), ()))


def _cparams(n_axes, vmem_mib):
    return pltpu.CompilerParams(
        dimension_semantics=("arbitrary",) * n_axes,
        vmem_limit_bytes=vmem_mib * MIB,
    )


def _silu(v):
    half = 0.5 * v
    return half + half * jnp.tanh(half)


def _rmsnorm_mod(xv, g, shift, scale):
    h = xv * lax.rsqrt(jnp.mean(xv * xv, axis=-1, keepdims=True) + NORM_EPS) * g
    return h * (1.0 + scale) + shift


def _mod_spec(layer, chunk, d, row_of_batch, n_grid_axes, batch_axis):
    def index_map(*ids):
        return (layer, row_of_batch(ids[batch_axis]), 0, chunk)
    del n_grid_axes
    return pl.BlockSpec((None, None, 1, d), index_map)


def _adaln_body(cond_ref, w_ref, b_ref, o_ref):
    s = _silu(cond_ref[...])
    o_ref[...] = jnp.dot(s.astype(BF16), w_ref[...].astype(BF16),
                         preferred_element_type=F32) + b_ref[...]


def _adaln(cond, ada_w, ada_b):
    layers, d, n = ada_w.shape
    tn = d
    out = pl.pallas_call(
        _adaln_body,
        grid=(layers, n // tn),
        in_specs=[
            pl.BlockSpec((COND_ROWS, d), lambda l, j: (0, 0)),
            pl.BlockSpec((None, d, tn), lambda l, j: (l, 0, j)),
            pl.BlockSpec((None, 1, tn), lambda l, j: (l, 0, j)),
        ],
        out_specs=pl.BlockSpec((None, COND_ROWS, tn), lambda l, j: (l, 0, j)),
        out_shape=jax.ShapeDtypeStruct((layers, COND_ROWS, n), F32),
        compiler_params=_cparams(2, 32),
        name="adaln",
    )(cond, ada_w, ada_b.reshape(layers, 1, n))
    return out.reshape(layers, COND_ROWS, 1, n)


def _nmm_body(x_ref, g_ref, sh_ref, sc_ref, w_ref, *rest, rope_groups, q_scaling, silu_from_tile):
    if rope_groups:
        cos_ref, sm_ref, sp_ref, o_ref, wbf_ref = rest
    elif silu_from_tile is not None:
        gain_ref, o_ref, wbf_ref = rest
    else:
        o_ref, wbf_ref = rest

    @pl.when((pl.program_id(1) == 0) & (pl.program_id(2) == 0))
    def _():
        wbf_ref[...] = w_ref[...].astype(BF16)

    def project():
        h = _rmsnorm_mod(x_ref[...], g_ref[...], sh_ref[...], sc_ref[...])
        return jnp.dot(h.astype(BF16), wbf_ref[...], preferred_element_type=F32)

    if silu_from_tile is not None:
        @pl.when(pl.program_id(0) < silu_from_tile)
        def _():
            o_ref[...] = project().astype(o_ref.dtype)

        @pl.when(pl.program_id(0) >= silu_from_tile)
        def _():
            o_ref[...] = (_silu(project()) * gain_ref[...]).astype(o_ref.dtype)

        return
    acc = project()
    if not rope_groups:
        o_ref[...] = acc.astype(o_ref.dtype)
        return
    c, sm, sp = cos_ref[...], sm_ref[...], sp_ref[...]
    n_groups = acc.shape[1] // LANES
    for j in range(n_groups):
        a = acc[:, j * LANES:(j + 1) * LANES]
        if j < rope_groups:
            a = (a * c + pltpu.roll(a, LANES - 16, axis=1) * sm
                 + pltpu.roll(a, 16, axis=1) * sp)
        if j < q_scaling[0]:
            a = a * q_scaling[1]
        o_ref[:, j * LANES:(j + 1) * LANES] = a.astype(o_ref.dtype)


def _norm_mod_matmul(x, g, mod, layer, row_of_batch, w, *, tm, tn, rope=None, silu_from_tile=None,
                     silu_gain=None):
    b, t, d = x.shape
    n = w.shape[1]
    tm = min(tm, t)
    grid = (n // tn, b, t // tm)
    in_specs = [
        pl.BlockSpec((None, tm, d), lambda j, bi, ti: (bi, ti, 0)),
        pl.BlockSpec((1, d), lambda j, bi, ti: (0, 0)),
        _mod_spec(layer, 0, d, row_of_batch, 3, 1),
        _mod_spec(layer, 1, d, row_of_batch, 3, 1),
        pl.BlockSpec((d, tn), lambda j, bi, ti: (0, j), pipeline_mode=pl.Buffered(1)),
    ]
    args = [x, g.reshape(1, d), mod, mod, w]
    rope_groups, q_scaling = 0, (0, 1.0)
    if rope is not None:
        cos, sm, sp, rope_groups, q_scaling = rope
        for tab in (cos, sm, sp):
            in_specs.append(pl.BlockSpec((tm, LANES), lambda j, bi, ti: (ti, 0)))
            args.append(tab)
    if silu_from_tile is not None:
        assert rope is None and silu_gain.shape == (1, n)
        in_specs.append(pl.BlockSpec((1, tn), lambda j, bi, ti: (0, j)))
        args.append(silu_gain)
    return pl.pallas_call(
        functools.partial(_nmm_body, rope_groups=rope_groups, q_scaling=q_scaling,
                          silu_from_tile=silu_from_tile),
        grid=grid,
        in_specs=in_specs,
        out_specs=pl.BlockSpec((None, tm, tn), lambda j, bi, ti: (bi, ti, j)),
        out_shape=jax.ShapeDtypeStruct((b, t, n), BF16),
        scratch_shapes=[pltpu.VMEM((d, tn), BF16)],
        compiler_params=_cparams(3, 48),
        name="norm_mod_proj",
    )(*args)


def _ret_body(dec_ref,
              qc_ref, kc_ref, vc_ref, gfc_ref, gbc_ref,
              q_ref, k_ref, v_ref, gf_ref, gb_ref,
              y_ref, yc_ref,
              statef_ref, stateb_ref, yacc_ref, intra_ref, qd_ref, kd_ref, cd_ref):
    c = RET_CHUNK
    dk = q_ref.shape[1]
    dv = v_ref.shape[1]
    t_ctx = qc_ref.shape[0]
    t_lat = q_ref.shape[0]

    dec = dec_ref[...]
    lg = jnp.minimum(dec, 0.0) - jnp.log1p(jnp.exp(-jnp.abs(dec)))
    row = lax.broadcasted_iota(jnp.int32, (c, c), 0)
    col = lax.broadcasted_iota(jnp.int32, (c, c), 1)
    pos_v = lax.broadcasted_iota(jnp.int32, (c, dv), 0).astype(F32)
    pos_k = lax.broadcasted_iota(jnp.int32, (c, dk), 0).astype(F32)
    for d in range(2):
        lg_c, lg_v, lg_k = lg[d:d + 1, :c], lg[d:d + 1, :], lg[d:d + 1, :dk]
        diff = (row - col) if d == 0 else (col - row)
        intra_ref[d] = jnp.where(diff >= 0, jnp.exp(jnp.maximum(diff, 0).astype(F32) * lg_c), 0.0)
        if d == 0:
            qd_ref[d] = jnp.exp((pos_v + 1.0) * lg_v)
            kd_ref[d] = jnp.exp((c - 1.0 - pos_k) * lg_k)
        else:
            qd_ref[d] = jnp.exp((c - pos_v) * lg_v)
            kd_ref[d] = jnp.exp(pos_k * lg_k)
        cd_ref[d] = jnp.exp(float(c) * lg_v)

    def chunk(d, refs, r0, yoff, out_ref, first_visit):
        qr, kr, vr, gr = refs
        state_ref = statef_ref if d == 0 else stateb_ref
        if not isinstance(r0, int):
            r0, yoff = pl.multiple_of(r0, c), pl.multiple_of(yoff, c)
        qs = qr[pl.ds(r0, c), :] * jnp.asarray(dk ** -0.5, BF16)
        kc = kr[pl.ds(r0, c), :]
        vc = vr[pl.ds(r0, c), :]
        s = lax.dot_general(qs, kc, NT_DIMS, preferred_element_type=F32) * intra_ref[d]
        st = state_ref[...]
        o = (jnp.dot(s.astype(BF16), vc, preferred_element_type=F32)
             + jnp.dot(qs, st.astype(BF16), preferred_element_type=F32) * qd_ref[d])
        kk = (kc.astype(F32) * kd_ref[d]).astype(BF16)
        state_ref[...] = cd_ref[d] * st + lax.dot_general(kk, vc, TN_DIMS, preferred_element_type=F32)
        mu = jnp.mean(o, axis=-1, keepdims=True)
        oc = o - mu
        var = jnp.mean(oc * oc, axis=-1, keepdims=True)
        y = gr[pl.ds(r0, c), :].astype(F32) * (oc * lax.rsqrt(var + NORM_EPS))
        if first_visit:
            yacc_ref[pl.ds(yoff, c), :] = y
        else:
            out_ref[pl.ds(r0, c), :] = (yacc_ref[pl.ds(yoff, c), :] + y).astype(out_ref.dtype)

    statef_ref[...] = jnp.zeros_like(statef_ref)
    stateb_ref[...] = jnp.zeros_like(stateb_ref)
    segments = (
        ((qc_ref, kc_ref, vc_ref), (gfc_ref, gbc_ref), t_ctx // c, 0, yc_ref),
        ((q_ref, k_ref, v_ref), (gf_ref, gb_ref), t_lat // c, t_ctx, y_ref),
    )
    for qkv, gates, n, ybase, out_ref in segments:
        half = n // 2

        def step(i, carry, first_visit, qkv=qkv, gates=gates, n=n, ybase=ybase, out_ref=out_ref):
            for d in range(2):
                ci = i if d == 0 else n - 1 - i
                chunk(d, qkv + (gates[d],), ci * c, ybase + ci * c, out_ref, first_visit)
            return carry

        unroll = 4 if half >= 4 else 1
        if half:
            lax.fori_loop(0, half, functools.partial(step, first_visit=True), 0, unroll=unroll)
        if n % 2:
            for d in range(2):
                chunk(d, qkv + (gates[d],), half * c, ybase + half * c, out_ref, d == 0)
        if half:
            lax.fori_loop(n - half, n, functools.partial(step, first_visit=False), 0, unroll=unroll)


def _retention(p, pc, decay_f, decay_b):
    b, t, cols = p.shape
    t_ctx = pc.shape[1]
    assert t % RET_CHUNK == 0 and t_ctx % RET_CHUNK == 0
    h = RET_HEADS
    dk = cols // (8 * h)
    dv = 2 * dk
    hv = h * dv
    dec = jnp.broadcast_to(jnp.stack([decay_f, decay_b], axis=1)[:, :, None], (h, 2, dv)).astype(F32)
    k_blk, v_blk, gf_blk, gb_blk = h, (2 * h * dk) // dv, (2 * h * dk + hv) // dv, (2 * h * dk + 2 * hv) // dv

    def specs(rows):
        return [
            pl.BlockSpec((None, rows, dk), lambda bi, hi: (bi, 0, hi)),
            pl.BlockSpec((None, rows, dk), lambda bi, hi: (bi, 0, k_blk + hi)),
            pl.BlockSpec((None, rows, dv), lambda bi, hi: (bi, 0, v_blk + hi)),
            pl.BlockSpec((None, rows, dv), lambda bi, hi: (bi, 0, gf_blk + hi)),
            pl.BlockSpec((None, rows, dv), lambda bi, hi: (bi, 0, gb_blk + hi)),
        ]

    return pl.pallas_call(
        _ret_body,
        grid=(b, h),
        in_specs=[pl.BlockSpec((None, 2, dv), lambda bi, hi: (hi, 0, 0))] + specs(t_ctx) + specs(t),
        out_specs=[
            pl.BlockSpec((None, t, dv), lambda bi, hi: (bi, 0, hi)),
            pl.BlockSpec((None, t_ctx, dv), lambda bi, hi: (bi, 0, hi)),
        ],
        out_shape=[
            jax.ShapeDtypeStruct((b, t, hv), BF16),
            jax.ShapeDtypeStruct((b, t_ctx, hv), BF16),
        ],
        scratch_shapes=[
            pltpu.VMEM((dk, dv), F32),
            pltpu.VMEM((dk, dv), F32),
            pltpu.VMEM((t_ctx + t, dv), F32),
            pltpu.VMEM((2, RET_CHUNK, RET_CHUNK), F32),
            pltpu.VMEM((2, RET_CHUNK, dv), F32),
            pltpu.VMEM((2, RET_CHUNK, dk), F32),
            pltpu.VMEM((2, 1, dv), F32),
        ],
        compiler_params=_cparams(2, 48),
        name="retention",
    )(dec, pc, pc, pc, pc, pc, p, p, p, p, p)


def _outproj_body(y_ref, w_ref, x_ref, g1_ref, ng_ref, sh_ref, sc_ref, wr_ref,
                  x1_ref, hx_ref, aff_ref, wbf_ref):
    @pl.when((pl.program_id(0) == 0) & (pl.program_id(1) == 0))
    def _():
        wbf_ref[...] = w_ref[...].astype(BF16)

    x1 = x_ref[...] + g1_ref[...] * jnp.dot(y_ref[...], wbf_ref[...], preferred_element_type=F32)
    x1_ref[...] = x1
    h = _rmsnorm_mod(x1, ng_ref[...], sh_ref[...], sc_ref[...])
    hb = h.astype(BF16)
    hx_ref[...] = hb
    hlo = (h - hb.astype(F32)).astype(BF16)
    wr = wr_ref[...]
    whi = wr.astype(BF16)
    wlo = (wr - whi.astype(F32)).astype(BF16)
    n_exp = wr.shape[0]
    with_hb = lax.dot_general(jnp.concatenate([whi, wlo], axis=0), hb, NT_DIMS,
                              preferred_element_type=F32)
    logits = (with_hb[:n_exp] + with_hb[n_exp:]
              + lax.dot_general(whi, hlo, NT_DIMS, preferred_element_type=F32))
    e = jnp.exp(logits - jnp.max(logits, axis=0, keepdims=True))
    aff_ref[...] = e / jnp.sum(e, axis=0, keepdims=True)


def _outproj_router(y, w, x, mod, layer, row_of_batch, norm_g, w_router, *, tm):
    b, t, kdim = y.shape
    d = w.shape[1]
    e = w_router.shape[1]
    tm = min(tm, t)
    row = lambda bi, ti: (0, 0)
    return pl.pallas_call(
        _outproj_body,
        grid=(b, t // tm),
        in_specs=[
            pl.BlockSpec((None, tm, kdim), lambda bi, ti: (bi, ti, 0)),
            pl.BlockSpec((kdim, d), row, pipeline_mode=pl.Buffered(1)),
            pl.BlockSpec((None, tm, d), lambda bi, ti: (bi, ti, 0)),
            _mod_spec(layer, 2, d, row_of_batch, 2, 0),
            pl.BlockSpec((1, d), row),
            _mod_spec(layer, 3, d, row_of_batch, 2, 0),
            _mod_spec(layer, 4, d, row_of_batch, 2, 0),
            pl.BlockSpec((e, d), row),
        ],
        out_specs=[
            pl.BlockSpec((None, tm, d), lambda bi, ti: (bi, ti, 0)),
            pl.BlockSpec((None, tm, d), lambda bi, ti: (bi, ti, 0)),
            pl.BlockSpec((None, e, tm), lambda bi, ti: (bi, 0, ti)),
        ],
        out_shape=[
            jax.ShapeDtypeStruct((b, t, d), F32),
            jax.ShapeDtypeStruct((b, t, d), BF16),
            jax.ShapeDtypeStruct((b, e, t), F32),
        ],
        scratch_shapes=[pltpu.VMEM((kdim, d), BF16)],
        compiler_params=_cparams(2, 48),
        name="outproj_router",
    )(y, w, x, mod, norm_g.reshape(1, d), mod, mod, w_router.T)


def _topk_body(aff_ref, slot_ref, wsel_ref, lo_ref, *, cap):
    a = aff_ref[...]
    r, t = a.shape
    capf = float(cap)

    def count_ge(v):
        return jnp.sum(jnp.where(a >= v, 1.0, 0.0), axis=1, keepdims=True)

    def bisect(_, lohi):
        lo, hi = lohi
        mid = 0.5 * (lo + hi)
        ok = count_ge(mid) >= capf
        return jnp.where(ok, mid, lo), jnp.where(ok, hi, mid)

    lo, hi = lax.fori_loop(0, 32, bisect, (jnp.zeros((r, 1), F32), jnp.full((r, 1), 2.0, F32)))

    def unfinished(state):
        return jnp.min(state[2]) < 0.5

    def walk(state):
        lo, hi, done = state
        top = jnp.max(jnp.where(a < hi, a, -1.0), axis=1, keepdims=True)
        found = (count_ge(top) >= capf) & (done < 0.5)
        moved = (done < 0.5) & jnp.logical_not(found)
        return jnp.where(found, top, lo), jnp.where(moved, top, hi), jnp.where(found, 1.0, done)

    thr, _, _ = lax.while_loop(unfinished, walk, (lo, hi, jnp.zeros((r, 1), F32)))

    gt = a > thr
    eq = a == thr
    need = capf - jnp.sum(jnp.where(gt, 1.0, 0.0), axis=1, keepdims=True)
    upper = jnp.where(lax.broadcasted_iota(jnp.int32, (LANES, LANES), 0)
                      < lax.broadcasted_iota(jnp.int32, (LANES, LANES), 1), 1.0, 0.0).astype(BF16)
    carry_eq = jnp.zeros((r, 1), F32)
    carry_sel = jnp.zeros((r, 1), F32)
    lane = lax.broadcasted_iota(jnp.int32, (r, LANES), 1)
    tile_lo = jnp.zeros((r, LANES), F32)
    for j in range(t // LANES):
        sl = slice(j * LANES, (j + 1) * LANES)
        if (j * LANES) % MOE_TILE == 0:
            tile_lo = jnp.where(lane == (j * LANES) // MOE_TILE, carry_sel, tile_lo)
        eqf = jnp.where(eq[:, sl], 1.0, 0.0)
        pre_eq = jnp.dot(eqf.astype(BF16), upper, preferred_element_type=F32) + carry_eq
        self_ = jnp.where(gt[:, sl], 1.0, jnp.where(pre_eq < need, eqf, 0.0))
        pre_sel = jnp.dot(self_.astype(BF16), upper, preferred_element_type=F32) + carry_sel
        sel = self_ > 0.5
        slot_ref[:, sl] = jnp.where(sel, pre_sel, -1.0).astype(jnp.int32)
        wsel_ref[:, sl] = jnp.where(sel, a[:, sl], 0.0)
        carry_eq = carry_eq + jnp.sum(eqf, axis=1, keepdims=True)
        carry_sel = carry_sel + jnp.sum(self_, axis=1, keepdims=True)
    tile_lo = jnp.where(lane == t // MOE_TILE, carry_sel, tile_lo)
    lo_ref[...] = tile_lo.astype(jnp.int32)


def _topk_select(aff):
    b, e, t = aff.shape
    cap = EC_CAPACITY * t // N_EXPERTS
    r = b * e
    n_tiles = t // MOE_TILE
    assert t % MOE_TILE == 0 and n_tiles < LANES
    full = pl.BlockSpec((r, t), lambda i: (0, 0))
    slot, wsel, lo = pl.pallas_call(
        functools.partial(_topk_body, cap=cap),
        grid=(1,),
        in_specs=[full],
        out_specs=[full, full, pl.BlockSpec((r, LANES), lambda i: (0, 0))],
        out_shape=[jax.ShapeDtypeStruct((r, t), jnp.int32), jax.ShapeDtypeStruct((r, t), F32),
                   jax.ShapeDtypeStruct((r, LANES), jnp.int32)],
        compiler_params=_cparams(1, 32),
        name="topk_select",
    )(aff.reshape(r, t))
    tile_lo = jnp.swapaxes(lo.reshape(b, e, LANES)[:, :, :n_tiles + 1], 1, 2).reshape(-1)
    return slot.reshape(b, e, t), wsel.reshape(b, e, t), tile_lo


def _gather_body(slot_ref, h_ref, o_ref, onehot_ref):
    n_exp, cap, d = o_ref.shape
    t = h_ref.shape[0]
    rank = lax.broadcasted_iota(jnp.int32, (cap, t), 0)
    for e in range(n_exp):
        onehot_ref[e * cap:(e + 1) * cap, :] = jnp.where(
            slot_ref[e:e + 1, :] == rank, 1.0, 0.0).astype(BF16)
    xs = jnp.dot(onehot_ref[...], h_ref[...], preferred_element_type=F32)
    o_ref[...] = xs.reshape(n_exp, cap, d).astype(o_ref.dtype)


def _gather(slot, hx, *, experts_per_step=8):
    b, e, t = slot.shape
    d = hx.shape[2]
    cap = EC_CAPACITY * t // N_EXPERTS
    eg = experts_per_step
    return pl.pallas_call(
        _gather_body,
        grid=(b, e // eg),
        in_specs=[
            pl.BlockSpec((None, eg, t), lambda bi, gi: (bi, gi, 0)),
            pl.BlockSpec((None, t, d), lambda bi, gi: (bi, 0, 0)),
        ],
        out_specs=pl.BlockSpec((eg, cap, d), lambda bi, gi: (gi, bi, 0)),
        out_shape=jax.ShapeDtypeStruct((e, b * cap, d), BF16),
        scratch_shapes=[pltpu.VMEM((eg * cap, t), BF16)],
        compiler_params=_cparams(2, 48),
        name="moe_gather",
    )(slot, hx)


def _window_starts(lo_ref, base, n_exp, cap):
    starts = []
    fits = None
    for e in range(n_exp):
        lo = lo_ref[base + e]
        hi = lo_ref[base + n_exp + e]
        start = jnp.minimum((lo // BF16_ROWS) * BF16_ROWS, cap - MOE_WINDOW)
        starts.append(pl.multiple_of(start, BF16_ROWS))
        ok = hi - start <= MOE_WINDOW
        fits = ok if fits is None else fits & ok
    return starts, fits


def _gather_win_body(lo_ref, slot_ref, h_ref, o_ref, onehot_ref):
    n_exp, cap, d = o_ref.shape
    tt = MOE_TILE
    tiles = h_ref.shape[0] // tt
    w = MOE_WINDOW
    i = pl.program_id(1)

    @pl.when(i == 0)
    def _():
        o_ref[...] = jnp.zeros_like(o_ref)

    for sub in range(tiles):
        tok = slice(sub * tt, (sub + 1) * tt)
        base = (pl.program_id(0) * (pl.num_programs(1) * tiles + 1) + i * tiles + sub) * n_exp
        starts, fits = _window_starts(lo_ref, base, n_exp, cap)

        @pl.when(fits)
        def _(tok=tok, starts=starts):
            rank = lax.broadcasted_iota(jnp.int32, (w, tt), 0)
            for e in range(n_exp):
                onehot_ref[e * w:(e + 1) * w, :] = jnp.where(
                    slot_ref[e:e + 1, tok] - starts[e] == rank, 1.0, 0.0).astype(BF16)
            rows = jnp.dot(onehot_ref[...], h_ref[tok, :], preferred_element_type=F32)
            for e in range(n_exp):
                o_ref[e, pl.ds(starts[e], w), :] += rows[e * w:(e + 1) * w].astype(o_ref.dtype)

        @pl.when(jnp.logical_not(fits))
        def _(tok=tok):
            rank = lax.broadcasted_iota(jnp.int32, (cap, tt), 0)
            for e in range(n_exp):
                onehot = jnp.where(slot_ref[e:e + 1, tok] == rank, 1.0, 0.0).astype(BF16)
                o_ref[e] += jnp.dot(onehot, h_ref[tok, :], preferred_element_type=F32).astype(o_ref.dtype)


def _gather_windowed(slot, tile_lo, hx):
    b, e, t = slot.shape
    d = hx.shape[2]
    cap = EC_CAPACITY * t // N_EXPERTS
    rows = MOE_TILE * MOE_TILES_PER_STEP
    return pl.pallas_call(
        _gather_win_body,
        grid_spec=pltpu.PrefetchScalarGridSpec(
            num_scalar_prefetch=1,
            grid=(b, t // rows),
            in_specs=[
                pl.BlockSpec((None, e, rows), lambda bi, ti, lo: (bi, 0, ti)),
                pl.BlockSpec((None, rows, d), lambda bi, ti, lo: (bi, ti, 0)),
            ],
            out_specs=pl.BlockSpec((e, cap, d), lambda bi, ti, lo: (0, bi, 0)),
            scratch_shapes=[pltpu.VMEM((e * MOE_WINDOW, MOE_TILE), BF16)],
        ),
        out_shape=jax.ShapeDtypeStruct((e, b * cap, d), BF16),
        compiler_params=_cparams(2, 48),
        name="moe_gather_windowed",
    )(tile_lo, slot, hx)


def _ffn_body(*refs, n_streams, row_chunk):
    xs_refs = refs[:n_streams]
    wg_ref, wu_ref, wd_ref = refs[n_streams:n_streams + 3]
    ys_refs = refs[n_streams + 3:2 * n_streams + 3]
    acc_refs = refs[2 * n_streams + 3:3 * n_streams + 3]
    wgb_ref, wub_ref, wdb_ref = refs[3 * n_streams + 3:]
    f = pl.program_id(1)
    wgb_ref[...] = wg_ref[...].astype(BF16)
    wub_ref[...] = wu_ref[...].astype(BF16)
    wdb_ref[...] = wd_ref[...].astype(BF16)
    for xs_ref, ys_ref, acc_ref in zip(xs_refs, ys_refs, acc_refs):
        rc = min(row_chunk, xs_ref.shape[0])

        @pl.when(f == 0)
        def _(acc_ref=acc_ref):
            acc_ref[...] = jnp.zeros_like(acc_ref)

        def rows(i, carry, xs_ref=xs_ref, acc_ref=acc_ref, rc=rc):
            r0 = pl.multiple_of(i * rc, rc)
            xb = xs_ref[pl.ds(r0, rc), :]
            g = jnp.dot(xb, wgb_ref[...], preferred_element_type=F32)
            u = jnp.dot(xb, wub_ref[...], preferred_element_type=F32)
            hid = (_silu(g) * u).astype(BF16)
            acc_ref[pl.ds(r0, rc), :] += jnp.dot(hid, wdb_ref[...], preferred_element_type=F32)
            return carry

        lax.fori_loop(0, xs_ref.shape[0] // rc, rows, 0)

        @pl.when(f == pl.num_programs(1) - 1)
        def _(ys_ref=ys_ref, acc_ref=acc_ref):
            ys_ref[...] = acc_ref[...].astype(ys_ref.dtype)


def _expert_ffn(xs_list, w_gate, w_up, w_down, layer, *, tf=512, row_chunk=1024):
    _, e, d, ff = w_gate.shape
    n = len(xs_list)
    xs_specs = [pl.BlockSpec((None, xs.shape[1], d), lambda ei, fi: (ei, 0, 0)) for xs in xs_list]
    return pl.pallas_call(
        functools.partial(_ffn_body, n_streams=n, row_chunk=row_chunk),
        grid=(e, ff // tf),
        in_specs=xs_specs + [
            pl.BlockSpec((None, None, d, tf), lambda ei, fi: (layer, ei, 0, fi)),
            pl.BlockSpec((None, None, d, tf), lambda ei, fi: (layer, ei, 0, fi)),
            pl.BlockSpec((None, None, tf, d), lambda ei, fi: (layer, ei, fi, 0)),
        ],
        out_specs=xs_specs,
        out_shape=[jax.ShapeDtypeStruct(xs.shape, BF16) for xs in xs_list],
        scratch_shapes=[pltpu.VMEM(xs.shape[1:], F32) for xs in xs_list]
        + [pltpu.VMEM((d, tf), BF16), pltpu.VMEM((d, tf), BF16), pltpu.VMEM((tf, d), BF16)],
        compiler_params=_cparams(2, 56),
        name="expert_ffn",
    )(*xs_list, w_gate, w_up, w_down)


def _combine_body(ys_ref, slot_ref, wsel_ref, x_ref, g2_ref, *rest, final_norm):
    if final_norm:
        fg_ref, o_ref = rest
    else:
        (o_ref,) = rest
    n_exp, cap, _ = ys_ref.shape
    tm = x_ref.shape[0]
    rank = lax.broadcasted_iota(jnp.int32, (tm, cap), 1)
    slots = slot_ref[...]
    wsel = wsel_ref[...]
    o_ref[...] = jnp.zeros_like(o_ref)
    for e in range(n_exp):
        onehot = jnp.where(slots[:, e:e + 1] == rank, 1.0, 0.0).astype(BF16)
        o_ref[...] += wsel[:, e:e + 1] * jnp.dot(onehot, ys_ref[e], preferred_element_type=F32)
    out = x_ref[...] + g2_ref[...] * o_ref[...]
    if final_norm:
        out = out * lax.rsqrt(jnp.mean(out * out, axis=-1, keepdims=True) + NORM_EPS) * fg_ref[...]
    o_ref[...] = out


def _combine(ys, slot_t, wsel_t, x, mod, layer, row_of_batch, final_g=None, *, tm=ROW_TILE):
    b, t, d = x.shape
    e = ys.shape[0]
    cap = ys.shape[1] // b
    tm = min(tm, t)
    in_specs = [
        pl.BlockSpec((e, cap, d), lambda bi, ti: (0, bi, 0)),
        pl.BlockSpec((None, tm, e), lambda bi, ti: (bi, ti, 0)),
        pl.BlockSpec((None, tm, e), lambda bi, ti: (bi, ti, 0)),
        pl.BlockSpec((None, tm, d), lambda bi, ti: (bi, ti, 0)),
        _mod_spec(layer, 5, d, row_of_batch, 2, 0),
    ]
    args = [ys, slot_t, wsel_t, x, mod]
    if final_g is not None:
        in_specs.append(pl.BlockSpec((1, d), lambda bi, ti: (0, 0)))
        args.append(final_g.reshape(1, d))
    return pl.pallas_call(
        functools.partial(_combine_body, final_norm=final_g is not None),
        grid=(b, t // tm),
        in_specs=in_specs,
        out_specs=pl.BlockSpec((None, tm, d), lambda bi, ti: (bi, ti, 0)),
        out_shape=jax.ShapeDtypeStruct((b, t, d), F32),
        compiler_params=_cparams(2, 48),
        name="moe_combine",
    )(*args)


def _combine_win_body(lo_ref, ys_ref, slot_ref, wsel_ref, x_ref, g2_ref, *rest, final_norm):
    if final_norm:
        fg_ref, o_ref, ywin_ref, scat_ref = rest
    else:
        o_ref, ywin_ref, scat_ref = rest
    n_exp, cap, d = ys_ref.shape
    tt = MOE_TILE
    tiles = x_ref.shape[0] // tt
    w = MOE_WINDOW
    i = pl.program_id(1)

    def finish(tok, acc):
        out = x_ref[tok, :] + g2_ref[...] * acc
        if final_norm:
            out = out * lax.rsqrt(jnp.mean(out * out, axis=-1, keepdims=True) + NORM_EPS) * fg_ref[...]
        o_ref[tok, :] = out

    for sub in range(tiles):
        tok = slice(sub * tt, (sub + 1) * tt)
        base = (pl.program_id(0) * (pl.num_programs(1) * tiles + 1) + i * tiles + sub) * n_exp
        starts, fits = _window_starts(lo_ref, base, n_exp, cap)

        @pl.when(fits)
        def _(tok=tok, starts=starts):
            rank = lax.broadcasted_iota(jnp.int32, (w, tt), 0)
            for e in range(n_exp):
                ywin_ref[e * w:(e + 1) * w, :] = ys_ref[e, pl.ds(starts[e], w), :]
                scat_ref[e * w:(e + 1) * w, :] = jnp.where(
                    slot_ref[e:e + 1, tok] - starts[e] == rank, wsel_ref[e:e + 1, tok], 0.0).astype(BF16)
            finish(tok, lax.dot_general(scat_ref[...], ywin_ref[...], TN_DIMS,
                                        preferred_element_type=F32))

        @pl.when(jnp.logical_not(fits))
        def _(tok=tok):
            rank = lax.broadcasted_iota(jnp.int32, (cap, tt), 0)
            acc = jnp.zeros((tt, d), F32)
            for e in range(n_exp):
                scat = jnp.where(slot_ref[e:e + 1, tok] == rank, wsel_ref[e:e + 1, tok], 0.0).astype(BF16)
                acc = acc + lax.dot_general(scat, ys_ref[e], TN_DIMS, preferred_element_type=F32)
            finish(tok, acc)


def _combine_windowed(ys, slot, wsel, tile_lo, x, mod, layer, row_of_batch, final_g=None):
    b, t, d = x.shape
    e = ys.shape[0]
    cap = ys.shape[1] // b
    rows = MOE_TILE * MOE_TILES_PER_STEP
    tile = lambda bi, ti, lo: (bi, ti, 0)
    routing = pl.BlockSpec((None, e, rows), lambda bi, ti, lo: (bi, 0, ti))
    in_specs = [
        pl.BlockSpec((e, cap, d), lambda bi, ti, lo: (0, bi, 0)),
        routing,
        routing,
        pl.BlockSpec((None, rows, d), tile),
        _mod_spec(layer, 5, d, row_of_batch, 2, 0),
    ]
    args = [ys, slot, wsel, x, mod]
    if final_g is not None:
        in_specs.append(pl.BlockSpec((1, d), lambda bi, ti, lo: (0, 0)))
        args.append(final_g.reshape(1, d))
    return pl.pallas_call(
        functools.partial(_combine_win_body, final_norm=final_g is not None),
        grid_spec=pltpu.PrefetchScalarGridSpec(
            num_scalar_prefetch=1,
            grid=(b, t // rows),
            in_specs=in_specs,
            out_specs=pl.BlockSpec((None, rows, d), tile),
            scratch_shapes=[pltpu.VMEM((e * MOE_WINDOW, d), BF16),
                            pltpu.VMEM((e * MOE_WINDOW, MOE_TILE), BF16)],
        ),
        out_shape=jax.ShapeDtypeStruct((b, t, d), F32),
        compiler_params=_cparams(2, 48),
        name="moe_combine_windowed",
    )(tile_lo, *args)


def _attn_body(sink_ref, q_ref, k_ref, v_ref, kc_ref, vc_ref, o_ref):
    blk = SWA_BLOCK
    hd = SWA_HEAD_DIM
    win = 3 * blk
    t = k_ref.shape[0]
    t_ctx = kc_ref.shape[0]
    n_heads = q_ref.shape[1] // hd
    group = n_heads // SWA_KV_HEADS
    blocks_per_step = q_ref.shape[0] // blk
    rel = (lax.broadcasted_iota(jnp.int32, (blk, win), 0)
           - lax.broadcasted_iota(jnp.int32, (blk, win), 1))
    head_idx = lax.broadcasted_iota(jnp.int32, (group, 1, 1), 0)

    def query_block(sub, carry):
        n = pl.program_id(1) * blocks_per_step + sub
        q0 = pl.multiple_of(sub * blk, blk)
        w0 = pl.multiple_of(jnp.clip((n - 1) * blk, 0, t - win), blk)
        dist = rel + (n * blk - w0)
        bias = jnp.where(jnp.abs(dist) <= SWA_WINDOW, 0.0, -jnp.inf)[None]
        for kv in range(SWA_KV_HEADS):
            heads = range(kv * group, (kv + 1) * group)
            cols = slice(kv * hd, (kv + 1) * hd)
            qg = jnp.concatenate([q_ref[pl.ds(q0, blk), h * hd:(h + 1) * hd] for h in heads], axis=0)
            s_ctx = lax.dot_general(qg, kc_ref[:, cols], NT_DIMS, preferred_element_type=F32)
            s_win = lax.dot_general(qg, k_ref[pl.ds(w0, win), cols], NT_DIMS, preferred_element_type=F32)
            s = jnp.concatenate([s_ctx.reshape(group, blk, t_ctx),
                                 s_win.reshape(group, blk, win) + bias], axis=-1)
            sink = jnp.zeros((group, 1, 1), F32)
            for i, h in enumerate(heads):
                sink = jnp.where(head_idx == i, sink_ref[h], sink)
            sink = sink * LOG2E
            m = jnp.maximum(jnp.max(s, axis=-1, keepdims=True), sink)
            p = jnp.exp2(s - m)
            den = jnp.sum(p, axis=-1, keepdims=True) + jnp.exp2(sink - m)
            pb = p.reshape(group * blk, t_ctx + win).astype(BF16)
            o = (jnp.dot(pb[:, :t_ctx], vc_ref[:, cols], preferred_element_type=F32)
                 + jnp.dot(pb[:, t_ctx:], v_ref[pl.ds(w0, win), cols], preferred_element_type=F32))
            o = o.reshape(group, blk, hd) / den
            for i, h in enumerate(heads):
                o_ref[pl.ds(q0, blk), h * hd:(h + 1) * hd] = o[i].astype(o_ref.dtype)
        return carry

    lax.fori_loop(0, blocks_per_step, query_block, 0, unroll=2)


def _window_attention(p, pc, sink, *, blocks_per_step=4):
    b, t, cols = p.shape
    t_ctx = pc.shape[1]
    kvw = SWA_KV_HEADS * SWA_HEAD_DIM
    qw = cols - 2 * kvw
    assert t >= 3 * SWA_BLOCK
    rows = blocks_per_step * SWA_BLOCK
    return pl.pallas_call(
        _attn_body,
        grid=(b, t // rows),
        in_specs=[
            pl.BlockSpec(memory_space=pltpu.SMEM),
            pl.BlockSpec((None, rows, qw), lambda bi, ni: (bi, ni, 0)),
            pl.BlockSpec((None, t, kvw), lambda bi, ni: (bi, 0, qw // kvw)),
            pl.BlockSpec((None, t, kvw), lambda bi, ni: (bi, 0, qw // kvw + 1)),
            pl.BlockSpec((None, t_ctx, kvw), lambda bi, ni: (bi, 0, 0)),
            pl.BlockSpec((None, t_ctx, kvw), lambda bi, ni: (bi, 0, 1)),
        ],
        out_specs=pl.BlockSpec((None, rows, qw), lambda bi, ni: (bi, ni, 0)),
        out_shape=jax.ShapeDtypeStruct((b, t, qw), BF16),
        compiler_params=_cparams(2, 32),
        name="window_attention",
    )(sink.astype(F32), p, p, p, pc, pc)


def _rope_tables(t):
    n_freq = SWA_HEAD_DIM // 4
    pos = jnp.arange(t)
    row = (pos // GRID_W).astype(F32)
    col = (pos % GRID_W).astype(F32)
    inv_freq = ROPE_BASE ** (-jnp.arange(n_freq, dtype=F32) / n_freq)
    ang_r, ang_c = row[:, None] * inv_freq, col[:, None] * inv_freq
    zero = jnp.zeros_like(ang_r)
    cos = jnp.concatenate([jnp.cos(ang_r)] * 2 + [jnp.cos(ang_c)] * 2, axis=1)
    sin_minus = jnp.concatenate([-jnp.sin(ang_r), zero, -jnp.sin(ang_c), zero], axis=1)
    sin_plus = jnp.concatenate([zero, jnp.sin(ang_r), zero, jnp.sin(ang_c)], axis=1)
    reps = LANES // SWA_HEAD_DIM
    return tuple(jnp.tile(tab, (1, reps)) for tab in (cos, sin_minus, sin_plus))


def _use_windows(t):
    cap = EC_CAPACITY * t // N_EXPERTS
    assert 2 * MOE_WINDOW == LANES
    return (t % (MOE_TILE * MOE_TILES_PER_STEP) == 0 and cap >= MOE_WINDOW
            and (cap - MOE_WINDOW) % BF16_ROWS == 0)


def _moe(streams, w_gate, w_up, w_down, layer):
    routed = []
    for hx, aff in streams:
        slot, wsel, tile_lo = _topk_select(aff)
        windowed = _use_windows(hx.shape[1])
        xs = _gather_windowed(slot, tile_lo, hx) if windowed else _gather(slot, hx)
        routed.append((xs, slot, wsel, tile_lo, windowed))
    ys_list = _expert_ffn([r[0] for r in routed], w_gate, w_up, w_down, layer)

    def make_combine(ys, slot, wsel, tile_lo, windowed):
        def combine(x, mod, row_of_batch, final_g=None):
            if windowed:
                return _combine_windowed(ys, slot, wsel, tile_lo, x, mod, layer, row_of_batch, final_g)
            return _combine(ys, jnp.swapaxes(slot, 1, 2), jnp.swapaxes(wsel, 1, 2), x, mod, layer,
                            row_of_batch, final_g)
        return combine

    return [make_combine(ys, *r[1:]) for ys, r in zip(ys_list, routed)]


def kernel(x, c, ctx, c_ctx, ada_w, ada_b, norm1_g, norm2_g, final_g, ret_w_in, ret_decay_f, ret_decay_b, ret_gn_f, ret_gn_b, ret_w_out, swa_w_qkv, swa_sink, swa_w_out, moe_router, moe_w_gate, moe_w_up, moe_w_down):
    b, t, d = x.shape
    lat_row = lambda bi: bi
    ctx_row = lambda bi: b
    cond = jnp.zeros((COND_ROWS, d), F32).at[:b].set(c).at[b].set(c_ctx)
    mod = _adaln(cond, ada_w, ada_b)

    ret_tn = 2048
    gate_col0 = (ret_w_in.shape[2] // 8) * 4
    assert gate_col0 % ret_tn == 0
    gate_gain = jnp.concatenate([jnp.ones((gate_col0,), F32), ret_gn_f[0].astype(F32),
                                 ret_gn_b[0].astype(F32)]).reshape(1, -1)
    ret_proj = dict(tm=ROW_TILE, tn=ret_tn, silu_from_tile=gate_col0 // ret_tn, silu_gain=gate_gain)
    p = _norm_mod_matmul(x, norm1_g[0], mod, 0, lat_row, ret_w_in[0], **ret_proj)
    t_ctx = ctx.shape[1]
    flat = lambda a: a.reshape(1, b * t_ctx, a.shape[-1])
    unflat = lambda a: a.reshape(b, t_ctx, a.shape[-1])
    pc = unflat(_norm_mod_matmul(flat(ctx), norm1_g[0], mod, 0, ctx_row, ret_w_in[0], **ret_proj))
    y, yc = _retention(p, pc, ret_decay_f[0], ret_decay_b[0])
    x1, hx, aff = _outproj_router(y, ret_w_out[0], x, mod, 0, lat_row, norm2_g[0], moe_router[0], tm=ROW_TILE)
    c1, hc, affc = _outproj_router(flat(yc), ret_w_out[0], flat(ctx), mod, 0, ctx_row, norm2_g[0],
                                   moe_router[0], tm=ROW_TILE)
    c1, hc = unflat(c1), unflat(hc)
    affc = jnp.swapaxes(affc.reshape(affc.shape[1], b, t_ctx), 0, 1)
    combine_lat, combine_ctx = _moe([(hx, aff), (hc, affc)], moe_w_gate, moe_w_up, moe_w_down, 0)
    x2 = combine_lat(x1, mod, lat_row)
    c2 = combine_ctx(c1, mod, ctx_row)

    qw = SWA_HEAD_DIM * (swa_w_qkv.shape[2] // SWA_HEAD_DIM - 2 * SWA_KV_HEADS)
    cos, sin_minus, sin_plus = _rope_tables(t)
    rope_groups = (qw + SWA_KV_HEADS * SWA_HEAD_DIM) // LANES
    p = _norm_mod_matmul(x2, norm1_g[1], mod, 1, lat_row, swa_w_qkv[0], tm=ROW_TILE,
                         tn=swa_w_qkv.shape[2],
                         rope=(cos, sin_minus, sin_plus, rope_groups,
                               (qw // LANES, SWA_HEAD_DIM ** -0.5 * LOG2E)))
    w_kv = swa_w_qkv[0][:, qw:]
    pc = unflat(_norm_mod_matmul(flat(c2), norm1_g[1], mod, 1, ctx_row, w_kv, tm=ROW_TILE, tn=w_kv.shape[1]))
    o = _window_attention(p, pc, swa_sink[0])
    x3, hx, aff = _outproj_router(o, swa_w_out[0], x2, mod, 1, lat_row, norm2_g[1], moe_router[1], tm=ROW_TILE)
    (combine_lat,) = _moe([(hx, aff)], moe_w_gate, moe_w_up, moe_w_down, 1)
    return combine_lat(x3, mod, lat_row, final_g)
```

```python
import functools

import jax
import jax.numpy as jnp
from jax import lax
from jax.experimental import pallas as pl
from jax.experimental.pallas import tpu as pltpu

F32 = jnp.float32
BF16 = jnp.bfloat16

NORM_EPS = 1e-6
GRID_W = 64
RET_HEADS = 4
RET_CHUNK = 256
SWA_HEAD_DIM = 64
SWA_KV_HEADS = 2
SWA_WINDOW = 128
SWA_BLOCK = SWA_WINDOW
ROPE_BASE = 10000.0
N_EXPERTS = 16
EC_CAPACITY = 2

ROW_TILE = 1024
MOE_TILE = 256
MOE_WINDOW = 64
MOE_TILES_PER_STEP = 4

LANES = 128
BF16_ROWS = 16
MIB = 1024 * 1024
COND_ROWS = 16

LOG2E = 1.4426950408889634

NT_DIMS = (((1,), (1,)), ((), ()))
TN_DIMS = (((0,), (0,)), ((), ()))


def _cparams(n_axes, vmem_mib):
    return pltpu.CompilerParams(
        dimension_semantics=("arbitrary",) * n_axes,
        vmem_limit_bytes=vmem_mib * MIB,
    )


def _silu(v):
    half = 0.5 * v
    return half + half * jnp.tanh(half)


def _rmsnorm_mod(xv, g, shift, scale):
    h = xv * lax.rsqrt(jnp.mean(xv * xv, axis=-1, keepdims=True) + NORM_EPS) * g
    return h * (1.0 + scale) + shift


def _mod_spec(layer, chunk, d, row_of_batch, n_grid_axes, batch_axis):
    def index_map(*ids):
        return (layer, row_of_batch(ids[batch_axis]), 0, chunk)
    del n_grid_axes
    return pl.BlockSpec((None, None, 1, d), index_map)


def _adaln_body(cond_ref, w_ref, b_ref, o_ref):
    s = _silu(cond_ref[...])
    o_ref[...] = jnp.dot(s.astype(BF16), w_ref[...].astype(BF16),
                         preferred_element_type=F32) + b_ref[...]


def _adaln(cond, ada_w, ada_b):
    layers, d, n = ada_w.shape
    tn = d
    out = pl.pallas_call(
        _adaln_body,
        grid=(layers, n // tn),
        in_specs=[
            pl.BlockSpec((COND_ROWS, d), lambda l, j: (0, 0)),
            pl.BlockSpec((None, d, tn), lambda l, j: (l, 0, j)),
            pl.BlockSpec((None, 1, tn), lambda l, j: (l, 0, j)),
        ],
        out_specs=pl.BlockSpec((None, COND_ROWS, tn), lambda l, j: (l, 0, j)),
        out_shape=jax.ShapeDtypeStruct((layers, COND_ROWS, n), F32),
        compiler_params=_cparams(2, 32),
        name="adaln",
    )(cond, ada_w, ada_b.reshape(layers, 1, n))
    return out.reshape(layers, COND_ROWS, 1, n)


def _nmm_body(x_ref, g_ref, sh_ref, sc_ref, w_ref, *rest, rope_groups, q_scaling, silu_from_tile):
    if rope_groups:
        cos_ref, sm_ref, sp_ref, o_ref, wbf_ref = rest
    elif silu_from_tile is not None:
        gain_ref, o_ref, wbf_ref = rest
    else:
        o_ref, wbf_ref = rest

    @pl.when((pl.program_id(1) == 0) & (pl.program_id(2) == 0))
    def _():
        wbf_ref[...] = w_ref[...].astype(BF16)

    def project():
        h = _rmsnorm_mod(x_ref[...], g_ref[...], sh_ref[...], sc_ref[...])
        return jnp.dot(h.astype(BF16), wbf_ref[...], preferred_element_type=F32)

    if silu_from_tile is not None:
        @pl.when(pl.program_id(0) < silu_from_tile)
        def _():
            o_ref[...] = project().astype(o_ref.dtype)

        @pl.when(pl.program_id(0) >= silu_from_tile)
        def _():
            o_ref[...] = (_silu(project()) * gain_ref[...]).astype(o_ref.dtype)

        return
    acc = project()
    if not rope_groups:
        o_ref[...] = acc.astype(o_ref.dtype)
        return
    c, sm, sp = cos_ref[...], sm_ref[...], sp_ref[...]
    n_groups = acc.shape[1] // LANES
    for j in range(n_groups):
        a = acc[:, j * LANES:(j + 1) * LANES]
        if j < rope_groups:
            a = (a * c + pltpu.roll(a, LANES - 16, axis=1) * sm
                 + pltpu.roll(a, 16, axis=1) * sp)
        if j < q_scaling[0]:
            a = a * q_scaling[1]
        o_ref[:, j * LANES:(j + 1) * LANES] = a.astype(o_ref.dtype)


def _norm_mod_matmul(x, g, mod, layer, row_of_batch, w, *, tm, tn, rope=None, silu_from_tile=None,
                     silu_gain=None):
    b, t, d = x.shape
    n = w.shape[1]
    tm = min(tm, t)
    grid = (n // tn, b, t // tm)
    in_specs = [
        pl.BlockSpec((None, tm, d), lambda j, bi, ti: (bi, ti, 0)),
        pl.BlockSpec((1, d), lambda j, bi, ti: (0, 0)),
        _mod_spec(layer, 0, d, row_of_batch, 3, 1),
        _mod_spec(layer, 1, d, row_of_batch, 3, 1),
        pl.BlockSpec((d, tn), lambda j, bi, ti: (0, j), pipeline_mode=pl.Buffered(1)),
    ]
    args = [x, g.reshape(1, d), mod, mod, w]
    rope_groups, q_scaling = 0, (0, 1.0)
    if rope is not None:
        cos, sm, sp, rope_groups, q_scaling = rope
        for tab in (cos, sm, sp):
            in_specs.append(pl.BlockSpec((tm, LANES), lambda j, bi, ti: (ti, 0)))
            args.append(tab)
    if silu_from_tile is not None:
        assert rope is None and silu_gain.shape == (1, n)
        in_specs.append(pl.BlockSpec((1, tn), lambda j, bi, ti: (0, j)))
        args.append(silu_gain)
    return pl.pallas_call(
        functools.partial(_nmm_body, rope_groups=rope_groups, q_scaling=q_scaling,
                          silu_from_tile=silu_from_tile),
        grid=grid,
        in_specs=in_specs,
        out_specs=pl.BlockSpec((None, tm, tn), lambda j, bi, ti: (bi, ti, j)),
        out_shape=jax.ShapeDtypeStruct((b, t, n), BF16),
        scratch_shapes=[pltpu.VMEM((d, tn), BF16)],
        compiler_params=_cparams(3, 48),
        name="norm_mod_proj",
    )(*args)


def _ret_body(dec_ref,
              qc_ref, kc_ref, vc_ref, gfc_ref, gbc_ref,
              q_ref, k_ref, v_ref, gf_ref, gb_ref,
              y_ref, yc_ref,
              statef_ref, stateb_ref, yacc_ref, intra_ref, qd_ref, kd_ref, cd_ref):
    c = RET_CHUNK
    dk = q_ref.shape[1]
    dv = v_ref.shape[1]
    t_ctx = qc_ref.shape[0]
    t_lat = q_ref.shape[0]

    dec = dec_ref[...]
    lg = jnp.minimum(dec, 0.0) - jnp.log1p(jnp.exp(-jnp.abs(dec)))
    row = lax.broadcasted_iota(jnp.int32, (c, c), 0)
    col = lax.broadcasted_iota(jnp.int32, (c, c), 1)
    pos_v = lax.broadcasted_iota(jnp.int32, (c, dv), 0).astype(F32)
    pos_k = lax.broadcasted_iota(jnp.int32, (c, dk), 0).astype(F32)
    for d in range(2):
        lg_c, lg_v, lg_k = lg[d:d + 1, :c], lg[d:d + 1, :], lg[d:d + 1, :dk]
        diff = (row - col) if d == 0 else (col - row)
        intra_ref[d] = jnp.where(diff >= 0, jnp.exp(jnp.maximum(diff, 0).astype(F32) * lg_c), 0.0)
        if d == 0:
            qd_ref[d] = jnp.exp((pos_v + 1.0) * lg_v)
            kd_ref[d] = jnp.exp((c - 1.0 - pos_k) * lg_k)
        else:
            qd_ref[d] = jnp.exp((c - pos_v) * lg_v)
            kd_ref[d] = jnp.exp(pos_k * lg_k)
        cd_ref[d] = jnp.exp(float(c) * lg_v)

    def chunk(d, refs, r0, yoff, out_ref, first_visit):
        qr, kr, vr, gr = refs
        state_ref = statef_ref if d == 0 else stateb_ref
        if not isinstance(r0, int):
            r0, yoff = pl.multiple_of(r0, c), pl.multiple_of(yoff, c)
        qs = qr[pl.ds(r0, c), :] * jnp.asarray(dk ** -0.5, BF16)
        kc = kr[pl.ds(r0, c), :]
        vc = vr[pl.ds(r0, c), :]
        s = lax.dot_general(qs, kc, NT_DIMS, preferred_element_type=F32) * intra_ref[d]
        st = state_ref[...]
        o = (jnp.dot(s.astype(BF16), vc, preferred_element_type=F32)
             + jnp.dot(qs, st.astype(BF16), preferred_element_type=F32) * qd_ref[d])
        kk = (kc.astype(F32) * kd_ref[d]).astype(BF16)
        state_ref[...] = cd_ref[d] * st + lax.dot_general(kk, vc, TN_DIMS, preferred_element_type=F32)
        mu = jnp.mean(o, axis=-1, keepdims=True)
        oc = o - mu
        var = jnp.mean(oc * oc, axis=-1, keepdims=True)
        y = gr[pl.ds(r0, c), :].astype(F32) * (oc * lax.rsqrt(var + NORM_EPS))
        if first_visit:
            yacc_ref[pl.ds(yoff, c), :] = y
        else:
            out_ref[pl.ds(r0, c), :] = (yacc_ref[pl.ds(yoff, c), :] + y).astype(out_ref.dtype)

    statef_ref[...] = jnp.zeros_like(statef_ref)
    stateb_ref[...] = jnp.zeros_like(stateb_ref)
    segments = (
        ((qc_ref, kc_ref, vc_ref), (gfc_ref, gbc_ref), t_ctx // c, 0, yc_ref),
        ((q_ref, k_ref, v_ref), (gf_ref, gb_ref), t_lat // c, t_ctx, y_ref),
    )
    for qkv, gates, n, ybase, out_ref in segments:
        half = n // 2

        def step(i, carry, first_visit, qkv=qkv, gates=gates, n=n, ybase=ybase, out_ref=out_ref):
            for d in range(2):
                ci = i if d == 0 else n - 1 - i
                chunk(d, qkv + (gates[d],), ci * c, ybase + ci * c, out_ref, first_visit)
            return carry

        unroll = 4 if half >= 4 else 1
        if half:
            lax.fori_loop(0, half, functools.partial(step, first_visit=True), 0, unroll=unroll)
        if n % 2:
            for d in range(2):
                chunk(d, qkv + (gates[d],), half * c, ybase + half * c, out_ref, d == 0)
        if half:
            lax.fori_loop(n - half, n, functools.partial(step, first_visit=False), 0, unroll=unroll)


def _retention(p, pc, decay_f, decay_b):
    b, t, cols = p.shape
    t_ctx = pc.shape[1]
    assert t % RET_CHUNK == 0 and t_ctx % RET_CHUNK == 0
    h = RET_HEADS
    dk = cols // (8 * h)
    dv = 2 * dk
    hv = h * dv
    dec = jnp.broadcast_to(jnp.stack([decay_f, decay_b], axis=1)[:, :, None], (h, 2, dv)).astype(F32)
    k_blk, v_blk, gf_blk, gb_blk = h, (2 * h * dk) // dv, (2 * h * dk + hv) // dv, (2 * h * dk + 2 * hv) // dv

    def specs(rows):
        return [
            pl.BlockSpec((None, rows, dk), lambda bi, hi: (bi, 0, hi)),
            pl.BlockSpec((None, rows, dk), lambda bi, hi: (bi, 0, k_blk + hi)),
            pl.BlockSpec((None, rows, dv), lambda bi, hi: (bi, 0, v_blk + hi)),
            pl.BlockSpec((None, rows, dv), lambda bi, hi: (bi, 0, gf_blk + hi)),
            pl.BlockSpec((None, rows, dv), lambda bi, hi: (bi, 0, gb_blk + hi)),
        ]

    return pl.pallas_call(
        _ret_body,
        grid=(b, h),
        in_specs=[pl.BlockSpec((None, 2, dv), lambda bi, hi: (hi, 0, 0))] + specs(t_ctx) + specs(t),
        out_specs=[
            pl.BlockSpec((None, t, dv), lambda bi, hi: (bi, 0, hi)),
            pl.BlockSpec((None, t_ctx, dv), lambda bi, hi: (bi, 0, hi)),
        ],
        out_shape=[
            jax.ShapeDtypeStruct((b, t, hv), BF16),
            jax.ShapeDtypeStruct((b, t_ctx, hv), BF16),
        ],
        scratch_shapes=[
            pltpu.VMEM((dk, dv), F32),
            pltpu.VMEM((dk, dv), F32),
            pltpu.VMEM((t_ctx + t, dv), F32),
            pltpu.VMEM((2, RET_CHUNK, RET_CHUNK), F32),
            pltpu.VMEM((2, RET_CHUNK, dv), F32),
            pltpu.VMEM((2, RET_CHUNK, dk), F32),
            pltpu.VMEM((2, 1, dv), F32),
        ],
        compiler_params=_cparams(2, 48),
        name="retention",
    )(dec, pc, pc, pc, pc, pc, p, p, p, p, p)


def _outproj_body(y_ref, w_ref, x_ref, g1_ref, ng_ref, sh_ref, sc_ref, wr_ref,
                  x1_ref, hx_ref, aff_ref, wbf_ref):
    @pl.when((pl.program_id(0) == 0) & (pl.program_id(1) == 0))
    def _():
        wbf_ref[...] = w_ref[...].astype(BF16)

    x1 = x_ref[...] + g1_ref[...] * jnp.dot(y_ref[...], wbf_ref[...], preferred_element_type=F32)
    x1_ref[...] = x1
    h = _rmsnorm_mod(x1, ng_ref[...], sh_ref[...], sc_ref[...])
    hb = h.astype(BF16)
    hx_ref[...] = hb
    hlo = (h - hb.astype(F32)).astype(BF16)
    wr = wr_ref[...]
    whi = wr.astype(BF16)
    wlo = (wr - whi.astype(F32)).astype(BF16)
    n_exp = wr.shape[0]
    with_hb = lax.dot_general(jnp.concatenate([whi, wlo], axis=0), hb, NT_DIMS,
                              preferred_element_type=F32)
    logits = (with_hb[:n_exp] + with_hb[n_exp:]
              + lax.dot_general(whi, hlo, NT_DIMS, preferred_element_type=F32))
    e = jnp.exp(logits - jnp.max(logits, axis=0, keepdims=True))
    aff_ref[...] = e / jnp.sum(e, axis=0, keepdims=True)


def _outproj_router(y, w, x, mod, layer, row_of_batch, norm_g, w_router, *, tm):
    b, t, kdim = y.shape
    d = w.shape[1]
    e = w_router.shape[1]
    tm = min(tm, t)
    row = lambda bi, ti: (0, 0)
    return pl.pallas_call(
        _outproj_body,
        grid=(b, t // tm),
        in_specs=[
            pl.BlockSpec((None, tm, kdim), lambda bi, ti: (bi, ti, 0)),
            pl.BlockSpec((kdim, d), row, pipeline_mode=pl.Buffered(1)),
            pl.BlockSpec((None, tm, d), lambda bi, ti: (bi, ti, 0)),
            _mod_spec(layer, 2, d, row_of_batch, 2, 0),
            pl.BlockSpec((1, d), row),
            _mod_spec(layer, 3, d, row_of_batch, 2, 0),
            _mod_spec(layer, 4, d, row_of_batch, 2, 0),
            pl.BlockSpec((e, d), row),
        ],
        out_specs=[
            pl.BlockSpec((None, tm, d), lambda bi, ti: (bi, ti, 0)),
            pl.BlockSpec((None, tm, d), lambda bi, ti: (bi, ti, 0)),
            pl.BlockSpec((None, e, tm), lambda bi, ti: (bi, 0, ti)),
        ],
        out_shape=[
            jax.ShapeDtypeStruct((b, t, d), F32),
            jax.ShapeDtypeStruct((b, t, d), BF16),
            jax.ShapeDtypeStruct((b, e, t), F32),
        ],
        scratch_shapes=[pltpu.VMEM((kdim, d), BF16)],
        compiler_params=_cparams(2, 48),
        name="outproj_router",
    )(y, w, x, mod, norm_g.reshape(1, d), mod, mod, w_router.T)


def _topk_body(aff_ref, slot_ref, wsel_ref, lo_ref, *, cap):
    a = aff_ref[...]
    r, t = a.shape
    capf = float(cap)

    def count_ge(v):
        return jnp.sum(jnp.where(a >= v, 1.0, 0.0), axis=1, keepdims=True)

    def bisect(_, lohi):
        lo, hi = lohi
        mid = 0.5 * (lo + hi)
        ok = count_ge(mid) >= capf
        return jnp.where(ok, mid, lo), jnp.where(ok, hi, mid)

    lo, hi = lax.fori_loop(0, 32, bisect, (jnp.zeros((r, 1), F32), jnp.full((r, 1), 2.0, F32)))

    def unfinished(state):
        return jnp.min(state[2]) < 0.5

    def walk(state):
        lo, hi, done = state
        top = jnp.max(jnp.where(a < hi, a, -1.0), axis=1, keepdims=True)
        found = (count_ge(top) >= capf) & (done < 0.5)
        moved = (done < 0.5) & jnp.logical_not(found)
        return jnp.where(found, top, lo), jnp.where(moved, top, hi), jnp.where(found, 1.0, done)

    thr, _, _ = lax.while_loop(unfinished, walk, (lo, hi, jnp.zeros((r, 1), F32)))

    gt = a > thr
    eq = a == thr
    need = capf - jnp.sum(jnp.where(gt, 1.0, 0.0), axis=1, keepdims=True)
    upper = jnp.where(lax.broadcasted_iota(jnp.int32, (LANES, LANES), 0)
                      < lax.broadcasted_iota(jnp.int32, (LANES, LANES), 1), 1.0, 0.0).astype(BF16)
    carry_eq = jnp.zeros((r, 1), F32)
    carry_sel = jnp.zeros((r, 1), F32)
    lane = lax.broadcasted_iota(jnp.int32, (r, LANES), 1)
    tile_lo = jnp.zeros((r, LANES), F32)
    for j in range(t // LANES):
        sl = slice(j * LANES, (j + 1) * LANES)
        if (j * LANES) % MOE_TILE == 0:
            tile_lo = jnp.where(lane == (j * LANES) // MOE_TILE, carry_sel, tile_lo)
        eqf = jnp.where(eq[:, sl], 1.0, 0.0)
        pre_eq = jnp.dot(eqf.astype(BF16), upper, preferred_element_type=F32) + carry_eq
        self_ = jnp.where(gt[:, sl], 1.0, jnp.where(pre_eq < need, eqf, 0.0))
        pre_sel = jnp.dot(self_.astype(BF16), upper, preferred_element_type=F32) + carry_sel
        sel = self_ > 0.5
        slot_ref[:, sl] = jnp.where(sel, pre_sel, -1.0).astype(jnp.int32)
        wsel_ref[:, sl] = jnp.where(sel, a[:, sl], 0.0)
        carry_eq = carry_eq + jnp.sum(eqf, axis=1, keepdims=True)
        carry_sel = carry_sel + jnp.sum(self_, axis=1, keepdims=True)
    tile_lo = jnp.where(lane == t // MOE_TILE, carry_sel, tile_lo)
    lo_ref[...] = tile_lo.astype(jnp.int32)


def _topk_select(aff):
    b, e, t = aff.shape
    cap = EC_CAPACITY * t // N_EXPERTS
    r = b * e
    n_tiles = t // MOE_TILE
    assert t % MOE_TILE == 0 and n_tiles < LANES
    full = pl.BlockSpec((r, t), lambda i: (0, 0))
    slot, wsel, lo = pl.pallas_call(
        functools.partial(_topk_body, cap=cap),
        grid=(1,),
        in_specs=[full],
        out_specs=[full, full, pl.BlockSpec((r, LANES), lambda i: (0, 0))],
        out_shape=[jax.ShapeDtypeStruct((r, t), jnp.int32), jax.ShapeDtypeStruct((r, t), F32),
                   jax.ShapeDtypeStruct((r, LANES), jnp.int32)],
        compiler_params=_cparams(1, 32),
        name="topk_select",
    )(aff.reshape(r, t))
    tile_lo = jnp.swapaxes(lo.reshape(b, e, LANES)[:, :, :n_tiles + 1], 1, 2).reshape(-1)
    return slot.reshape(b, e, t), wsel.reshape(b, e, t), tile_lo


def _gather_body(slot_ref, h_ref, o_ref, onehot_ref):
    n_exp, cap, d = o_ref.shape
    t = h_ref.shape[0]
    rank = lax.broadcasted_iota(jnp.int32, (cap, t), 0)
    for e in range(n_exp):
        onehot_ref[e * cap:(e + 1) * cap, :] = jnp.where(
            slot_ref[e:e + 1, :] == rank, 1.0, 0.0).astype(BF16)
    xs = jnp.dot(onehot_ref[...], h_ref[...], preferred_element_type=F32)
    o_ref[...] = xs.reshape(n_exp, cap, d).astype(o_ref.dtype)


def _gather(slot, hx, *, experts_per_step=8):
    b, e, t = slot.shape
    d = hx.shape[2]
    cap = EC_CAPACITY * t // N_EXPERTS
    eg = experts_per_step
    return pl.pallas_call(
        _gather_body,
        grid=(b, e // eg),
        in_specs=[
            pl.BlockSpec((None, eg, t), lambda bi, gi: (bi, gi, 0)),
            pl.BlockSpec((None, t, d), lambda bi, gi: (bi, 0, 0)),
        ],
        out_specs=pl.BlockSpec((eg, cap, d), lambda bi, gi: (gi, bi, 0)),
        out_shape=jax.ShapeDtypeStruct((e, b * cap, d), BF16),
        scratch_shapes=[pltpu.VMEM((eg * cap, t), BF16)],
        compiler_params=_cparams(2, 48),
        name="moe_gather",
    )(slot, hx)


def _window_starts(lo_ref, base, n_exp, cap):
    starts = []
    fits = None
    for e in range(n_exp):
        lo = lo_ref[base + e]
        hi = lo_ref[base + n_exp + e]
        start = jnp.minimum((lo // BF16_ROWS) * BF16_ROWS, cap - MOE_WINDOW)
        starts.append(pl.multiple_of(start, BF16_ROWS))
        ok = hi - start <= MOE_WINDOW
        fits = ok if fits is None else fits & ok
    return starts, fits


def _gather_win_body(lo_ref, slot_ref, h_ref, o_ref, onehot_ref):
    n_exp, cap, d = o_ref.shape
    tt = MOE_TILE
    tiles = h_ref.shape[0] // tt
    w = MOE_WINDOW
    i = pl.program_id(1)

    @pl.when(i == 0)
    def _():
        o_ref[...] = jnp.zeros_like(o_ref)

    for sub in range(tiles):
        tok = slice(sub * tt, (sub + 1) * tt)
        base = (pl.program_id(0) * (pl.num_programs(1) * tiles + 1) + i * tiles + sub) * n_exp
        starts, fits = _window_starts(lo_ref, base, n_exp, cap)

        @pl.when(fits)
        def _(tok=tok, starts=starts):
            rank = lax.broadcasted_iota(jnp.int32, (w, tt), 0)
            for e in range(n_exp):
                onehot_ref[e * w:(e + 1) * w, :] = jnp.where(
                    slot_ref[e:e + 1, tok] - starts[e] == rank, 1.0, 0.0).astype(BF16)
            rows = jnp.dot(onehot_ref[...], h_ref[tok, :], preferred_element_type=F32)
            for e in range(n_exp):
                o_ref[e, pl.ds(starts[e], w), :] += rows[e * w:(e + 1) * w].astype(o_ref.dtype)

        @pl.when(jnp.logical_not(fits))
        def _(tok=tok):
            rank = lax.broadcasted_iota(jnp.int32, (cap, tt), 0)
            for e in range(n_exp):
                onehot = jnp.where(slot_ref[e:e + 1, tok] == rank, 1.0, 0.0).astype(BF16)
                o_ref[e] += jnp.dot(onehot, h_ref[tok, :], preferred_element_type=F32).astype(o_ref.dtype)


def _gather_windowed(slot, tile_lo, hx):
    b, e, t = slot.shape
    d = hx.shape[2]
    cap = EC_CAPACITY * t // N_EXPERTS
    rows = MOE_TILE * MOE_TILES_PER_STEP
    return pl.pallas_call(
        _gather_win_body,
        grid_spec=pltpu.PrefetchScalarGridSpec(
            num_scalar_prefetch=1,
            grid=(b, t // rows),
            in_specs=[
                pl.BlockSpec((None, e, rows), lambda bi, ti, lo: (bi, 0, ti)),
                pl.BlockSpec((None, rows, d), lambda bi, ti, lo: (bi, ti, 0)),
            ],
            out_specs=pl.BlockSpec((e, cap, d), lambda bi, ti, lo: (0, bi, 0)),
            scratch_shapes=[pltpu.VMEM((e * MOE_WINDOW, MOE_TILE), BF16)],
        ),
        out_shape=jax.ShapeDtypeStruct((e, b * cap, d), BF16),
        compiler_params=_cparams(2, 48),
        name="moe_gather_windowed",
    )(tile_lo, slot, hx)


def _ffn_body(*refs, n_streams, row_chunk):
    xs_refs = refs[:n_streams]
    wg_ref, wu_ref, wd_ref = refs[n_streams:n_streams + 3]
    ys_refs = refs[n_streams + 3:2 * n_streams + 3]
    acc_refs = refs[2 * n_streams + 3:3 * n_streams + 3]
    wgb_ref, wub_ref, wdb_ref = refs[3 * n_streams + 3:]
    f = pl.program_id(1)
    wgb_ref[...] = wg_ref[...].astype(BF16)
    wub_ref[...] = wu_ref[...].astype(BF16)
    wdb_ref[...] = wd_ref[...].astype(BF16)
    last = pl.num_programs(1) - 1
    for xs_ref, ys_ref, acc_ref in zip(xs_refs, ys_refs, acc_refs):
        rc = min(row_chunk, xs_ref.shape[0])

        def rows(i, carry, mode, xs_ref=xs_ref, ys_ref=ys_ref, acc_ref=acc_ref, rc=rc):
            r0 = pl.multiple_of(i * rc, rc)
            xb = xs_ref[pl.ds(r0, rc), :]
            g = jnp.dot(xb, wgb_ref[...], preferred_element_type=F32)
            u = jnp.dot(xb, wub_ref[...], preferred_element_type=F32)
            hid = (_silu(g) * u).astype(BF16)
            part = jnp.dot(hid, wdb_ref[...], preferred_element_type=F32)
            if mode == "first":
                acc_ref[pl.ds(r0, rc), :] = part
            elif mode == "middle":
                acc_ref[pl.ds(r0, rc), :] += part
            else:
                ys_ref[pl.ds(r0, rc), :] = (acc_ref[pl.ds(r0, rc), :] + part).astype(ys_ref.dtype)
            return carry

        n_chunks = xs_ref.shape[0] // rc
        for mode, cond in (("first", f == 0), ("middle", (f > 0) & (f < last)), ("last", f == last)):
            @pl.when(cond)
            def _(mode=mode, rows=rows, n_chunks=n_chunks):
                lax.fori_loop(0, n_chunks, functools.partial(rows, mode=mode), 0)


def _expert_ffn(xs_list, w_gate, w_up, w_down, layer, *, tf=512, row_chunk=1024):
    _, e, d, ff = w_gate.shape
    assert ff % tf == 0 and ff // tf >= 2
    n = len(xs_list)
    xs_specs = [pl.BlockSpec((None, xs.shape[1], d), lambda ei, fi: (ei, 0, 0)) for xs in xs_list]
    return pl.pallas_call(
        functools.partial(_ffn_body, n_streams=n, row_chunk=row_chunk),
        grid=(e, ff // tf),
        in_specs=xs_specs + [
            pl.BlockSpec((None, None, d, tf), lambda ei, fi: (layer, ei, 0, fi)),
            pl.BlockSpec((None, None, d, tf), lambda ei, fi: (layer, ei, 0, fi)),
            pl.BlockSpec((None, None, tf, d), lambda ei, fi: (layer, ei, fi, 0)),
        ],
        out_specs=xs_specs,
        out_shape=[jax.ShapeDtypeStruct(xs.shape, BF16) for xs in xs_list],
        scratch_shapes=[pltpu.VMEM(xs.shape[1:], F32) for xs in xs_list]
        + [pltpu.VMEM((d, tf), BF16), pltpu.VMEM((d, tf), BF16), pltpu.VMEM((tf, d), BF16)],
        compiler_params=_cparams(2, 56),
        name="expert_ffn",
    )(*xs_list, w_gate, w_up, w_down)


def _combine_body(ys_ref, slot_ref, wsel_ref, x_ref, g2_ref, *rest, final_norm):
    if final_norm:
        fg_ref, o_ref = rest
    else:
        (o_ref,) = rest
    n_exp, cap, _ = ys_ref.shape
    tm = x_ref.shape[0]
    rank = lax.broadcasted_iota(jnp.int32, (tm, cap), 1)
    slots = slot_ref[...]
    wsel = wsel_ref[...]
    o_ref[...] = jnp.zeros_like(o_ref)
    for e in range(n_exp):
        onehot = jnp.where(slots[:, e:e + 1] == rank, 1.0, 0.0).astype(BF16)
        o_ref[...] += wsel[:, e:e + 1] * jnp.dot(onehot, ys_ref[e], preferred_element_type=F32)
    out = x_ref[...] + g2_ref[...] * o_ref[...]
    if final_norm:
        out = out * lax.rsqrt(jnp.mean(out * out, axis=-1, keepdims=True) + NORM_EPS) * fg_ref[...]
    o_ref[...] = out


def _combine(ys, slot_t, wsel_t, x, mod, layer, row_of_batch, final_g=None, *, tm=ROW_TILE):
    b, t, d = x.shape
    e = ys.shape[0]
    cap = ys.shape[1] // b
    tm = min(tm, t)
    in_specs = [
        pl.BlockSpec((e, cap, d), lambda bi, ti: (0, bi, 0)),
        pl.BlockSpec((None, tm, e), lambda bi, ti: (bi, ti, 0)),
        pl.BlockSpec((None, tm, e), lambda bi, ti: (bi, ti, 0)),
        pl.BlockSpec((None, tm, d), lambda bi, ti: (bi, ti, 0)),
        _mod_spec(layer, 5, d, row_of_batch, 2, 0),
    ]
    args = [ys, slot_t, wsel_t, x, mod]
    if final_g is not None:
        in_specs.append(pl.BlockSpec((1, d), lambda bi, ti: (0, 0)))
        args.append(final_g.reshape(1, d))
    return pl.pallas_call(
        functools.partial(_combine_body, final_norm=final_g is not None),
        grid=(b, t // tm),
        in_specs=in_specs,
        out_specs=pl.BlockSpec((None, tm, d), lambda bi, ti: (bi, ti, 0)),
        out_shape=jax.ShapeDtypeStruct((b, t, d), F32),
        compiler_params=_cparams(2, 48),
        name="moe_combine",
    )(*args)


def _combine_win_body(lo_ref, ys_ref, slot_ref, wsel_ref, x_ref, g2_ref, *rest, final_norm):
    if final_norm:
        fg_ref, o_ref, ywin_ref, scat_ref = rest
    else:
        o_ref, ywin_ref, scat_ref = rest
    n_exp, cap, d = ys_ref.shape
    tt = MOE_TILE
    tiles = x_ref.shape[0] // tt
    w = MOE_WINDOW
    i = pl.program_id(1)

    def finish(tok, acc):
        out = x_ref[tok, :] + g2_ref[...] * acc
        if final_norm:
            out = out * lax.rsqrt(jnp.mean(out * out, axis=-1, keepdims=True) + NORM_EPS) * fg_ref[...]
        o_ref[tok, :] = out

    for sub in range(tiles):
        tok = slice(sub * tt, (sub + 1) * tt)
        base = (pl.program_id(0) * (pl.num_programs(1) * tiles + 1) + i * tiles + sub) * n_exp
        starts, fits = _window_starts(lo_ref, base, n_exp, cap)

        @pl.when(fits)
        def _(tok=tok, starts=starts):
            rank = lax.broadcasted_iota(jnp.int32, (w, tt), 0)
            for e in range(n_exp):
                ywin_ref[e * w:(e + 1) * w, :] = ys_ref[e, pl.ds(starts[e], w), :]
                scat_ref[e * w:(e + 1) * w, :] = jnp.where(
                    slot_ref[e:e + 1, tok] - starts[e] == rank, wsel_ref[e:e + 1, tok], 0.0).astype(BF16)
            finish(tok, lax.dot_general(scat_ref[...], ywin_ref[...], TN_DIMS,
                                        preferred_element_type=F32))

        @pl.when(jnp.logical_not(fits))
        def _(tok=tok):
            rank = lax.broadcasted_iota(jnp.int32, (cap, tt), 0)
            acc = jnp.zeros((tt, d), F32)
            for e in range(n_exp):
                scat = jnp.where(slot_ref[e:e + 1, tok] == rank, wsel_ref[e:e + 1, tok], 0.0).astype(BF16)
                acc = acc + lax.dot_general(scat, ys_ref[e], TN_DIMS, preferred_element_type=F32)
            finish(tok, acc)


def _combine_windowed(ys, slot, wsel, tile_lo, x, mod, layer, row_of_batch, final_g=None):
    b, t, d = x.shape
    e = ys.shape[0]
    cap = ys.shape[1] // b
    rows = MOE_TILE * MOE_TILES_PER_STEP
    tile = lambda bi, ti, lo: (bi, ti, 0)
    routing = pl.BlockSpec((None, e, rows), lambda bi, ti, lo: (bi, 0, ti))
    in_specs = [
        pl.BlockSpec((e, cap, d), lambda bi, ti, lo: (0, bi, 0)),
        routing,
        routing,
        pl.BlockSpec((None, rows, d), tile),
        _mod_spec(layer, 5, d, row_of_batch, 2, 0),
    ]
    args = [ys, slot, wsel, x, mod]
    if final_g is not None:
        in_specs.append(pl.BlockSpec((1, d), lambda bi, ti, lo: (0, 0)))
        args.append(final_g.reshape(1, d))
    return pl.pallas_call(
        functools.partial(_combine_win_body, final_norm=final_g is not None),
        grid_spec=pltpu.PrefetchScalarGridSpec(
            num_scalar_prefetch=1,
            grid=(b, t // rows),
            in_specs=in_specs,
            out_specs=pl.BlockSpec((None, rows, d), tile),
            scratch_shapes=[pltpu.VMEM((e * MOE_WINDOW, d), BF16),
                            pltpu.VMEM((e * MOE_WINDOW, MOE_TILE), BF16)],
        ),
        out_shape=jax.ShapeDtypeStruct((b, t, d), F32),
        compiler_params=_cparams(2, 48),
        name="moe_combine_windowed",
    )(tile_lo, *args)


def _attn_body(sink_ref, q_ref, k_ref, v_ref, kc_ref, vc_ref, o_ref):
    blk = SWA_BLOCK
    hd = SWA_HEAD_DIM
    win = 3 * blk
    t = k_ref.shape[0]
    t_ctx = kc_ref.shape[0]
    n_heads = q_ref.shape[1] // hd
    group = n_heads // SWA_KV_HEADS
    blocks_per_step = q_ref.shape[0] // blk
    rel = (lax.broadcasted_iota(jnp.int32, (blk, win), 0)
           - lax.broadcasted_iota(jnp.int32, (blk, win), 1))
    head_idx = lax.broadcasted_iota(jnp.int32, (group, 1, 1), 0)

    def query_block(sub, carry):
        n = pl.program_id(1) * blocks_per_step + sub
        q0 = pl.multiple_of(sub * blk, blk)
        w0 = pl.multiple_of(jnp.clip((n - 1) * blk, 0, t - win), blk)
        dist = rel + (n * blk - w0)
        bias = jnp.where(jnp.abs(dist) <= SWA_WINDOW, 0.0, -jnp.inf)[None]
        for kv in range(SWA_KV_HEADS):
            heads = range(kv * group, (kv + 1) * group)
            cols = slice(kv * hd, (kv + 1) * hd)
            qg = jnp.concatenate([q_ref[pl.ds(q0, blk), h * hd:(h + 1) * hd] for h in heads], axis=0)
            s_ctx = lax.dot_general(qg, kc_ref[:, cols], NT_DIMS, preferred_element_type=F32)
            s_win = lax.dot_general(qg, k_ref[pl.ds(w0, win), cols], NT_DIMS, preferred_element_type=F32)
            s = jnp.concatenate([s_ctx.reshape(group, blk, t_ctx),
                                 s_win.reshape(group, blk, win) + bias], axis=-1)
            sink = jnp.zeros((group, 1, 1), F32)
            for i, h in enumerate(heads):
                sink = jnp.where(head_idx == i, sink_ref[h], sink)
            sink = sink * LOG2E
            m = jnp.maximum(jnp.max(s, axis=-1, keepdims=True), sink)
            p = jnp.exp2(s - m)
            den = jnp.sum(p, axis=-1, keepdims=True) + jnp.exp2(sink - m)
            pb = p.reshape(group * blk, t_ctx + win).astype(BF16)
            o = (jnp.dot(pb[:, :t_ctx], vc_ref[:, cols], preferred_element_type=F32)
                 + jnp.dot(pb[:, t_ctx:], v_ref[pl.ds(w0, win), cols], preferred_element_type=F32))
            o = o.reshape(group, blk, hd) / den
            for i, h in enumerate(heads):
                o_ref[pl.ds(q0, blk), h * hd:(h + 1) * hd] = o[i].astype(o_ref.dtype)
        return carry

    lax.fori_loop(0, blocks_per_step, query_block, 0, unroll=2)


def _window_attention(p, pc, sink, *, blocks_per_step=4):
    b, t, cols = p.shape
    t_ctx = pc.shape[1]
    kvw = SWA_KV_HEADS * SWA_HEAD_DIM
    qw = cols - 2 * kvw
    assert t >= 3 * SWA_BLOCK
    rows = blocks_per_step * SWA_BLOCK
    return pl.pallas_call(
        _attn_body,
        grid=(b, t // rows),
        in_specs=[
            pl.BlockSpec(memory_space=pltpu.SMEM),
            pl.BlockSpec((None, rows, qw), lambda bi, ni: (bi, ni, 0)),
            pl.BlockSpec((None, t, kvw), lambda bi, ni: (bi, 0, qw // kvw)),
            pl.BlockSpec((None, t, kvw), lambda bi, ni: (bi, 0, qw // kvw + 1)),
            pl.BlockSpec((None, t_ctx, kvw), lambda bi, ni: (bi, 0, 0)),
            pl.BlockSpec((None, t_ctx, kvw), lambda bi, ni: (bi, 0, 1)),
        ],
        out_specs=pl.BlockSpec((None, rows, qw), lambda bi, ni: (bi, ni, 0)),
        out_shape=jax.ShapeDtypeStruct((b, t, qw), BF16),
        compiler_params=_cparams(2, 32),
        name="window_attention",
    )(sink.astype(F32), p, p, p, pc, pc)


def _rope_tables(t):
    n_freq = SWA_HEAD_DIM // 4
    pos = jnp.arange(t)
    row = (pos // GRID_W).astype(F32)
    col = (pos % GRID_W).astype(F32)
    inv_freq = ROPE_BASE ** (-jnp.arange(n_freq, dtype=F32) / n_freq)
    ang_r, ang_c = row[:, None] * inv_freq, col[:, None] * inv_freq
    zero = jnp.zeros_like(ang_r)
    cos = jnp.concatenate([jnp.cos(ang_r)] * 2 + [jnp.cos(ang_c)] * 2, axis=1)
    sin_minus = jnp.concatenate([-jnp.sin(ang_r), zero, -jnp.sin(ang_c), zero], axis=1)
    sin_plus = jnp.concatenate([zero, jnp.sin(ang_r), zero, jnp.sin(ang_c)], axis=1)
    reps = LANES // SWA_HEAD_DIM
    return tuple(jnp.tile(tab, (1, reps)) for tab in (cos, sin_minus, sin_plus))


def _use_windows(t):
    cap = EC_CAPACITY * t // N_EXPERTS
    assert 2 * MOE_WINDOW == LANES
    return (t % (MOE_TILE * MOE_TILES_PER_STEP) == 0 and cap >= MOE_WINDOW
            and (cap - MOE_WINDOW) % BF16_ROWS == 0)


def _moe(streams, w_gate, w_up, w_down, layer):
    routed = []
    for hx, aff in streams:
        slot, wsel, tile_lo = _topk_select(aff)
        windowed = _use_windows(hx.shape[1])
        xs = _gather_windowed(slot, tile_lo, hx) if windowed else _gather(slot, hx)
        routed.append((xs, slot, wsel, tile_lo, windowed))
    ys_list = _expert_ffn([r[0] for r in routed], w_gate, w_up, w_down, layer)

    def make_combine(ys, slot, wsel, tile_lo, windowed):
        def combine(x, mod, row_of_batch, final_g=None):
            if windowed:
                return _combine_windowed(ys, slot, wsel, tile_lo, x, mod, layer, row_of_batch, final_g)
            return _combine(ys, jnp.swapaxes(slot, 1, 2), jnp.swapaxes(wsel, 1, 2), x, mod, layer,
                            row_of_batch, final_g)
        return combine

    return [make_combine(ys, *r[1:]) for ys, r in zip(ys_list, routed)]


def kernel(x, c, ctx, c_ctx, ada_w, ada_b, norm1_g, norm2_g, final_g, ret_w_in, ret_decay_f, ret_decay_b, ret_gn_f, ret_gn_b, ret_w_out, swa_w_qkv, swa_sink, swa_w_out, moe_router, moe_w_gate, moe_w_up, moe_w_down):
    b, t, d = x.shape
    lat_row = lambda bi: bi
    ctx_row = lambda bi: b
    cond = jnp.zeros((COND_ROWS, d), F32).at[:b].set(c).at[b].set(c_ctx)
    mod = _adaln(cond, ada_w, ada_b)

    ret_tn = 2048
    gate_col0 = (ret_w_in.shape[2] // 8) * 4
    assert gate_col0 % ret_tn == 0
    gate_gain = jnp.concatenate([jnp.ones((gate_col0,), F32), ret_gn_f[0].astype(F32),
                                 ret_gn_b[0].astype(F32)]).reshape(1, -1)
    ret_proj = dict(tm=ROW_TILE, tn=ret_tn, silu_from_tile=gate_col0 // ret_tn, silu_gain=gate_gain)
    p = _norm_mod_matmul(x, norm1_g[0], mod, 0, lat_row, ret_w_in[0], **ret_proj)
    t_ctx = ctx.shape[1]
    flat = lambda a: a.reshape(1, b * t_ctx, a.shape[-1])
    unflat = lambda a: a.reshape(b, t_ctx, a.shape[-1])
    pc = unflat(_norm_mod_matmul(flat(ctx), norm1_g[0], mod, 0, ctx_row, ret_w_in[0], **ret_proj))
    y, yc = _retention(p, pc, ret_decay_f[0], ret_decay_b[0])
    x1, hx, aff = _outproj_router(y, ret_w_out[0], x, mod, 0, lat_row, norm2_g[0], moe_router[0], tm=ROW_TILE)
    c1, hc, affc = _outproj_router(flat(yc), ret_w_out[0], flat(ctx), mod, 0, ctx_row, norm2_g[0],
                                   moe_router[0], tm=ROW_TILE)
    c1, hc = unflat(c1), unflat(hc)
    affc = jnp.swapaxes(affc.reshape(affc.shape[1], b, t_ctx), 0, 1)
    combine_lat, combine_ctx = _moe([(hx, aff), (hc, affc)], moe_w_gate, moe_w_up, moe_w_down, 0)
    x2 = combine_lat(x1, mod, lat_row)
    c2 = combine_ctx(c1, mod, ctx_row)

    qw = SWA_HEAD_DIM * (swa_w_qkv.shape[2] // SWA_HEAD_DIM - 2 * SWA_KV_HEADS)
    cos, sin_minus, sin_plus = _rope_tables(t)
    rope_groups = (qw + SWA_KV_HEADS * SWA_HEAD_DIM) // LANES
    p = _norm_mod_matmul(x2, norm1_g[1], mod, 1, lat_row, swa_w_qkv[0], tm=ROW_TILE,
                         tn=swa_w_qkv.shape[2],
                         rope=(cos, sin_minus, sin_plus, rope_groups,
                               (qw // LANES, SWA_HEAD_DIM ** -0.5 * LOG2E)))
    w_kv = swa_w_qkv[0][:, qw:]
    pc = unflat(_norm_mod_matmul(flat(c2), norm1_g[1], mod, 1, ctx_row, w_kv, tm=ROW_TILE, tn=w_kv.shape[1]))
    o = _window_attention(p, pc, swa_sink[0])
    x3, hx, aff = _outproj_router(o, swa_w_out[0], x2, mod, 1, lat_row, norm2_g[1], moe_router[1], tm=ROW_TILE)
    (combine_lat,) = _moe([(hx, aff)], moe_w_gate, moe_w_up, moe_w_down, 1)
    return combine_lat(x3, mod, lat_row, final_g)
```

```python
import functools

import jax
import jax.numpy as jnp
from jax import lax
from jax.experimental import pallas as pl
from jax.experimental.pallas import tpu as pltpu

F32 = jnp.float32
BF16 = jnp.bfloat16

NORM_EPS = 1e-6
GRID_W = 64
RET_HEADS = 4
RET_CHUNK = 256
SWA_HEAD_DIM = 64
SWA_KV_HEADS = 2
SWA_WINDOW = 128
SWA_BLOCK = SWA_WINDOW
ROPE_BASE = 10000.0
N_EXPERTS = 16
EC_CAPACITY = 2

ROW_TILE = 1024
MOE_TILE = 256
MOE_WINDOW = 64
MOE_TILES_PER_STEP = 4

LANES = 128
BF16_ROWS = 16
MIB = 1024 * 1024
COND_ROWS = 16

LOG2E = 1.4426950408889634

NT_DIMS = (((1,), (1,)), ((), ()))
TN_DIMS = (((0,), (0,)), ((), ()))


def _cparams(n_axes, vmem_mib):
    return pltpu.CompilerParams(
        dimension_semantics=("arbitrary",) * n_axes,
        vmem_limit_bytes=vmem_mib * MIB,
    )


def _silu(v):
    half = 0.5 * v
    return half + half * jnp.tanh(half)


def _rmsnorm_mod(xv, g, shift, scale):
    h = xv * lax.rsqrt(jnp.mean(xv * xv, axis=-1, keepdims=True) + NORM_EPS) * g
    return h * (1.0 + scale) + shift


def _mod_spec(layer, chunk, d, row_of_batch, n_grid_axes, batch_axis):
    def index_map(*ids):
        return (layer, row_of_batch(ids[batch_axis]), 0, chunk)
    del n_grid_axes
    return pl.BlockSpec((None, None, 1, d), index_map)


def _adaln_body(cond_ref, w_ref, b_ref, o_ref):
    s = _silu(cond_ref[...])
    o_ref[...] = jnp.dot(s.astype(BF16), w_ref[...].astype(BF16),
                         preferred_element_type=F32) + b_ref[...]


def _adaln(cond, ada_w, ada_b):
    layers, d, n = ada_w.shape
    tn = d
    out = pl.pallas_call(
        _adaln_body,
        grid=(layers, n // tn),
        in_specs=[
            pl.BlockSpec((COND_ROWS, d), lambda l, j: (0, 0)),
            pl.BlockSpec((None, d, tn), lambda l, j: (l, 0, j)),
            pl.BlockSpec((None, 1, tn), lambda l, j: (l, 0, j)),
        ],
        out_specs=pl.BlockSpec((None, COND_ROWS, tn), lambda l, j: (l, 0, j)),
        out_shape=jax.ShapeDtypeStruct((layers, COND_ROWS, n), F32),
        compiler_params=_cparams(2, 32),
        name="adaln",
    )(cond, ada_w, ada_b.reshape(layers, 1, n))
    return out.reshape(layers, COND_ROWS, 1, n)


def _nmm_body(x_ref, g_ref, sh_ref, sc_ref, w_ref, *rest, rope_groups, q_scaling, silu_from_tile):
    if rope_groups:
        cos_ref, sm_ref, sp_ref, o_ref, wbf_ref = rest
    elif silu_from_tile is not None:
        gain_ref, o_ref, wbf_ref = rest
    else:
        o_ref, wbf_ref = rest

    @pl.when((pl.program_id(1) == 0) & (pl.program_id(2) == 0))
    def _():
        wbf_ref[...] = w_ref[...].astype(BF16)

    def project():
        h = _rmsnorm_mod(x_ref[...], g_ref[...], sh_ref[...], sc_ref[...])
        return jnp.dot(h.astype(BF16), wbf_ref[...], preferred_element_type=F32)

    if silu_from_tile is not None:
        @pl.when(pl.program_id(0) < silu_from_tile)
        def _():
            o_ref[...] = project().astype(o_ref.dtype)

        @pl.when(pl.program_id(0) >= silu_from_tile)
        def _():
            o_ref[...] = (_silu(project()) * gain_ref[...]).astype(o_ref.dtype)

        return
    acc = project()
    if not rope_groups:
        o_ref[...] = acc.astype(o_ref.dtype)
        return
    c, sm, sp = cos_ref[...], sm_ref[...], sp_ref[...]
    n_groups = acc.shape[1] // LANES
    for j in range(n_groups):
        a = acc[:, j * LANES:(j + 1) * LANES]
        if j < rope_groups:
            a = (a * c + pltpu.roll(a, LANES - 16, axis=1) * sm
                 + pltpu.roll(a, 16, axis=1) * sp)
        if j < q_scaling[0]:
            a = a * q_scaling[1]
        o_ref[:, j * LANES:(j + 1) * LANES] = a.astype(o_ref.dtype)


def _norm_mod_matmul(x, g, mod, layer, row_of_batch, w, *, tm, tn, rope=None, silu_from_tile=None,
                     silu_gain=None):
    b, t, d = x.shape
    n = w.shape[1]
    tm = min(tm, t)
    grid = (n // tn, b, t // tm)
    in_specs = [
        pl.BlockSpec((None, tm, d), lambda j, bi, ti: (bi, ti, 0)),
        pl.BlockSpec((1, d), lambda j, bi, ti: (0, 0)),
        _mod_spec(layer, 0, d, row_of_batch, 3, 1),
        _mod_spec(layer, 1, d, row_of_batch, 3, 1),
        pl.BlockSpec((d, tn), lambda j, bi, ti: (0, j)),
    ]
    args = [x, g.reshape(1, d), mod, mod, w]
    rope_groups, q_scaling = 0, (0, 1.0)
    if rope is not None:
        cos, sm, sp, rope_groups, q_scaling = rope
        for tab in (cos, sm, sp):
            in_specs.append(pl.BlockSpec((tm, LANES), lambda j, bi, ti: (ti, 0)))
            args.append(tab)
    if silu_from_tile is not None:
        assert rope is None and silu_gain.shape == (1, n)
        in_specs.append(pl.BlockSpec((1, tn), lambda j, bi, ti: (0, j)))
        args.append(silu_gain)
    return pl.pallas_call(
        functools.partial(_nmm_body, rope_groups=rope_groups, q_scaling=q_scaling,
                          silu_from_tile=silu_from_tile),
        grid=grid,
        in_specs=in_specs,
        out_specs=pl.BlockSpec((None, tm, tn), lambda j, bi, ti: (bi, ti, j)),
        out_shape=jax.ShapeDtypeStruct((b, t, n), BF16),
        scratch_shapes=[pltpu.VMEM((d, tn), BF16)],
        compiler_params=_cparams(3, 48),
        name="norm_mod_proj",
    )(*args)


def _ret_body(dec_ref,
              qc_ref, kc_ref, vc_ref, gfc_ref, gbc_ref,
              q_ref, k_ref, v_ref, gf_ref, gb_ref,
              y_ref, yc_ref,
              statef_ref, stateb_ref, yacc_ref, intra_ref, qd_ref, kd_ref, cd_ref):
    c = RET_CHUNK
    dk = q_ref.shape[1]
    dv = v_ref.shape[1]
    t_ctx = qc_ref.shape[0]
    t_lat = q_ref.shape[0]

    dec = dec_ref[...]
    lg = jnp.minimum(dec, 0.0) - jnp.log1p(jnp.exp(-jnp.abs(dec)))
    row = lax.broadcasted_iota(jnp.int32, (c, c), 0)
    col = lax.broadcasted_iota(jnp.int32, (c, c), 1)
    pos_v = lax.broadcasted_iota(jnp.int32, (c, dv), 0).astype(F32)
    pos_k = lax.broadcasted_iota(jnp.int32, (c, dk), 0).astype(F32)
    for d in range(2):
        lg_c, lg_v, lg_k = lg[d:d + 1, :c], lg[d:d + 1, :], lg[d:d + 1, :dk]
        diff = (row - col) if d == 0 else (col - row)
        intra_ref[d] = jnp.where(diff >= 0, jnp.exp(jnp.maximum(diff, 0).astype(F32) * lg_c), 0.0)
        if d == 0:
            qd_ref[d] = jnp.exp((pos_v + 1.0) * lg_v)
            kd_ref[d] = jnp.exp((c - 1.0 - pos_k) * lg_k)
        else:
            qd_ref[d] = jnp.exp((c - pos_v) * lg_v)
            kd_ref[d] = jnp.exp(pos_k * lg_k)
        cd_ref[d] = jnp.exp(float(c) * lg_v)

    def chunk(d, refs, r0, yoff, out_ref, first_visit):
        qr, kr, vr, gr = refs
        state_ref = statef_ref if d == 0 else stateb_ref
        if not isinstance(r0, int):
            r0, yoff = pl.multiple_of(r0, c), pl.multiple_of(yoff, c)
        qs = qr[pl.ds(r0, c), :] * jnp.asarray(dk ** -0.5, BF16)
        kc = kr[pl.ds(r0, c), :]
        vc = vr[pl.ds(r0, c), :]
        s = lax.dot_general(qs, kc, NT_DIMS, preferred_element_type=F32) * intra_ref[d]
        st = state_ref[...]
        o = (jnp.dot(s.astype(BF16), vc, preferred_element_type=F32)
             + jnp.dot(qs, st.astype(BF16), preferred_element_type=F32) * qd_ref[d])
        kk = (kc.astype(F32) * kd_ref[d]).astype(BF16)
        state_ref[...] = cd_ref[d] * st + lax.dot_general(kk, vc, TN_DIMS, preferred_element_type=F32)
        mu = jnp.mean(o, axis=-1, keepdims=True)
        oc = o - mu
        var = jnp.mean(oc * oc, axis=-1, keepdims=True)
        y = gr[pl.ds(r0, c), :].astype(F32) * (oc * lax.rsqrt(var + NORM_EPS))
        if first_visit:
            yacc_ref[pl.ds(yoff, c), :] = y
        else:
            out_ref[pl.ds(r0, c), :] = (yacc_ref[pl.ds(yoff, c), :] + y).astype(out_ref.dtype)

    statef_ref[...] = jnp.zeros_like(statef_ref)
    stateb_ref[...] = jnp.zeros_like(stateb_ref)
    segments = (
        ((qc_ref, kc_ref, vc_ref), (gfc_ref, gbc_ref), t_ctx // c, 0, yc_ref),
        ((q_ref, k_ref, v_ref), (gf_ref, gb_ref), t_lat // c, t_ctx, y_ref),
    )
    for qkv, gates, n, ybase, out_ref in segments:
        half = n // 2

        def step(i, carry, first_visit, qkv=qkv, gates=gates, n=n, ybase=ybase, out_ref=out_ref):
            for d in range(2):
                ci = i if d == 0 else n - 1 - i
                chunk(d, qkv + (gates[d],), ci * c, ybase + ci * c, out_ref, first_visit)
            return carry

        unroll = 4 if half >= 4 else 1
        if half:
            lax.fori_loop(0, half, functools.partial(step, first_visit=True), 0, unroll=unroll)
        if n % 2:
            for d in range(2):
                chunk(d, qkv + (gates[d],), half * c, ybase + half * c, out_ref, d == 0)
        if half:
            lax.fori_loop(n - half, n, functools.partial(step, first_visit=False), 0, unroll=unroll)


def _retention(p, pc, decay_f, decay_b):
    b, t, cols = p.shape
    t_ctx = pc.shape[1]
    assert t % RET_CHUNK == 0 and t_ctx % RET_CHUNK == 0
    h = RET_HEADS
    dk = cols // (8 * h)
    dv = 2 * dk
    hv = h * dv
    dec = jnp.broadcast_to(jnp.stack([decay_f, decay_b], axis=1)[:, :, None], (h, 2, dv)).astype(F32)
    k_blk, v_blk, gf_blk, gb_blk = h, (2 * h * dk) // dv, (2 * h * dk + hv) // dv, (2 * h * dk + 2 * hv) // dv

    def specs(rows):
        return [
            pl.BlockSpec((None, rows, dk), lambda bi, hi: (bi, 0, hi)),
            pl.BlockSpec((None, rows, dk), lambda bi, hi: (bi, 0, k_blk + hi)),
            pl.BlockSpec((None, rows, dv), lambda bi, hi: (bi, 0, v_blk + hi)),
            pl.BlockSpec((None, rows, dv), lambda bi, hi: (bi, 0, gf_blk + hi)),
            pl.BlockSpec((None, rows, dv), lambda bi, hi: (bi, 0, gb_blk + hi)),
        ]

    return pl.pallas_call(
        _ret_body,
        grid=(b, h),
        in_specs=[pl.BlockSpec((None, 2, dv), lambda bi, hi: (hi, 0, 0))] + specs(t_ctx) + specs(t),
        out_specs=[
            pl.BlockSpec((None, t, dv), lambda bi, hi: (bi, 0, hi)),
            pl.BlockSpec((None, t_ctx, dv), lambda bi, hi: (bi, 0, hi)),
        ],
        out_shape=[
            jax.ShapeDtypeStruct((b, t, hv), BF16),
            jax.ShapeDtypeStruct((b, t_ctx, hv), BF16),
        ],
        scratch_shapes=[
            pltpu.VMEM((dk, dv), F32),
            pltpu.VMEM((dk, dv), F32),
            pltpu.VMEM((t_ctx + t, dv), F32),
            pltpu.VMEM((2, RET_CHUNK, RET_CHUNK), F32),
            pltpu.VMEM((2, RET_CHUNK, dv), F32),
            pltpu.VMEM((2, RET_CHUNK, dk), F32),
            pltpu.VMEM((2, 1, dv), F32),
        ],
        compiler_params=_cparams(2, 48),
        name="retention",
    )(dec, pc, pc, pc, pc, pc, p, p, p, p, p)


def _outproj_body(y_ref, w_ref, x_ref, g1_ref, ng_ref, sh_ref, sc_ref, wr_ref,
                  x1_ref, hx_ref, aff_ref, wbf_ref):
    @pl.when((pl.program_id(0) == 0) & (pl.program_id(1) == 0))
    def _():
        wbf_ref[...] = w_ref[...].astype(BF16)

    x1 = x_ref[...] + g1_ref[...] * jnp.dot(y_ref[...], wbf_ref[...], preferred_element_type=F32)
    x1_ref[...] = x1
    h = _rmsnorm_mod(x1, ng_ref[...], sh_ref[...], sc_ref[...])
    hb = h.astype(BF16)
    hx_ref[...] = hb
    hlo = (h - hb.astype(F32)).astype(BF16)
    wr = wr_ref[...]
    whi = wr.astype(BF16)
    wlo = (wr - whi.astype(F32)).astype(BF16)
    n_exp = wr.shape[0]
    with_hb = lax.dot_general(jnp.concatenate([whi, wlo], axis=0), hb, NT_DIMS,
                              preferred_element_type=F32)
    logits = (with_hb[:n_exp] + with_hb[n_exp:]
              + lax.dot_general(whi, hlo, NT_DIMS, preferred_element_type=F32))
    e = jnp.exp(logits - jnp.max(logits, axis=0, keepdims=True))
    aff_ref[...] = e / jnp.sum(e, axis=0, keepdims=True)


def _outproj_router(y, w, x, mod, layer, row_of_batch, norm_g, w_router, *, tm):
    b, t, kdim = y.shape
    d = w.shape[1]
    e = w_router.shape[1]
    tm = min(tm, t)
    row = lambda bi, ti: (0, 0)
    return pl.pallas_call(
        _outproj_body,
        grid=(b, t // tm),
        in_specs=[
            pl.BlockSpec((None, tm, kdim), lambda bi, ti: (bi, ti, 0)),
            pl.BlockSpec((kdim, d), row, pipeline_mode=pl.Buffered(1)),
            pl.BlockSpec((None, tm, d), lambda bi, ti: (bi, ti, 0)),
            _mod_spec(layer, 2, d, row_of_batch, 2, 0),
            pl.BlockSpec((1, d), row),
            _mod_spec(layer, 3, d, row_of_batch, 2, 0),
            _mod_spec(layer, 4, d, row_of_batch, 2, 0),
            pl.BlockSpec((e, d), row),
        ],
        out_specs=[
            pl.BlockSpec((None, tm, d), lambda bi, ti: (bi, ti, 0)),
            pl.BlockSpec((None, tm, d), lambda bi, ti: (bi, ti, 0)),
            pl.BlockSpec((None, e, tm), lambda bi, ti: (bi, 0, ti)),
        ],
        out_shape=[
            jax.ShapeDtypeStruct((b, t, d), F32),
            jax.ShapeDtypeStruct((b, t, d), BF16),
            jax.ShapeDtypeStruct((b, e, t), F32),
        ],
        scratch_shapes=[pltpu.VMEM((kdim, d), BF16)],
        compiler_params=_cparams(2, 48),
        name="outproj_router",
    )(y, w, x, mod, norm_g.reshape(1, d), mod, mod, w_router.T)


def _topk_body(aff_ref, slot_ref, wsel_ref, lo_ref, *, cap):
    a = aff_ref[...]
    r, t = a.shape
    capf = float(cap)

    def count_ge(v):
        return jnp.sum(jnp.where(a >= v, 1.0, 0.0), axis=1, keepdims=True)

    def bisect(_, lohi):
        lo, hi = lohi
        mid = 0.5 * (lo + hi)
        ok = count_ge(mid) >= capf
        return jnp.where(ok, mid, lo), jnp.where(ok, hi, mid)

    lo, hi = lax.fori_loop(0, 32, bisect, (jnp.zeros((r, 1), F32), jnp.full((r, 1), 2.0, F32)))

    def unfinished(state):
        return jnp.min(state[2]) < 0.5

    def walk(state):
        lo, hi, done = state
        top = jnp.max(jnp.where(a < hi, a, -1.0), axis=1, keepdims=True)
        found = (count_ge(top) >= capf) & (done < 0.5)
        moved = (done < 0.5) & jnp.logical_not(found)
        return jnp.where(found, top, lo), jnp.where(moved, top, hi), jnp.where(found, 1.0, done)

    thr, _, _ = lax.while_loop(unfinished, walk, (lo, hi, jnp.zeros((r, 1), F32)))

    gt = a > thr
    eq = a == thr
    need = capf - jnp.sum(jnp.where(gt, 1.0, 0.0), axis=1, keepdims=True)
    upper = jnp.where(lax.broadcasted_iota(jnp.int32, (LANES, LANES), 0)
                      < lax.broadcasted_iota(jnp.int32, (LANES, LANES), 1), 1.0, 0.0).astype(BF16)
    carry_eq = jnp.zeros((r, 1), F32)
    carry_sel = jnp.zeros((r, 1), F32)
    lane = lax.broadcasted_iota(jnp.int32, (r, LANES), 1)
    tile_lo = jnp.zeros((r, LANES), F32)
    for j in range(t // LANES):
        sl = slice(j * LANES, (j + 1) * LANES)
        if (j * LANES) % MOE_TILE == 0:
            tile_lo = jnp.where(lane == (j * LANES) // MOE_TILE, carry_sel, tile_lo)
        eqf = jnp.where(eq[:, sl], 1.0, 0.0)
        pre_eq = jnp.dot(eqf.astype(BF16), upper, preferred_element_type=F32) + carry_eq
        self_ = jnp.where(gt[:, sl], 1.0, jnp.where(pre_eq < need, eqf, 0.0))
        pre_sel = jnp.dot(self_.astype(BF16), upper, preferred_element_type=F32) + carry_sel
        sel = self_ > 0.5
        slot_ref[:, sl] = jnp.where(sel, pre_sel, -1.0).astype(jnp.int32)
        wsel_ref[:, sl] = jnp.where(sel, a[:, sl], 0.0)
        carry_eq = carry_eq + jnp.sum(eqf, axis=1, keepdims=True)
        carry_sel = carry_sel + jnp.sum(self_, axis=1, keepdims=True)
    tile_lo = jnp.where(lane == t // MOE_TILE, carry_sel, tile_lo)
    lo_ref[...] = tile_lo.astype(jnp.int32)


def _topk_select(aff):
    b, e, t = aff.shape
    cap = EC_CAPACITY * t // N_EXPERTS
    r = b * e
    n_tiles = t // MOE_TILE
    assert t % MOE_TILE == 0 and n_tiles < LANES
    full = pl.BlockSpec((r, t), lambda i: (0, 0))
    slot, wsel, lo = pl.pallas_call(
        functools.partial(_topk_body, cap=cap),
        grid=(1,),
        in_specs=[full],
        out_specs=[full, full, pl.BlockSpec((r, LANES), lambda i: (0, 0))],
        out_shape=[jax.ShapeDtypeStruct((r, t), jnp.int32), jax.ShapeDtypeStruct((r, t), F32),
                   jax.ShapeDtypeStruct((r, LANES), jnp.int32)],
        compiler_params=_cparams(1, 32),
        name="topk_select",
    )(aff.reshape(r, t))
    tile_lo = jnp.swapaxes(lo.reshape(b, e, LANES)[:, :, :n_tiles + 1], 1, 2).reshape(-1)
    return slot.reshape(b, e, t), wsel.reshape(b, e, t), tile_lo


def _gather_body(slot_ref, h_ref, o_ref, onehot_ref):
    n_exp, cap, d = o_ref.shape
    t = h_ref.shape[0]
    rank = lax.broadcasted_iota(jnp.int32, (cap, t), 0)
    for e in range(n_exp):
        onehot_ref[e * cap:(e + 1) * cap, :] = jnp.where(
            slot_ref[e:e + 1, :] == rank, 1.0, 0.0).astype(BF16)
    xs = jnp.dot(onehot_ref[...], h_ref[...], preferred_element_type=F32)
    o_ref[...] = xs.reshape(n_exp, cap, d).astype(o_ref.dtype)


def _gather(slot, hx, *, experts_per_step=8):
    b, e, t = slot.shape
    d = hx.shape[2]
    cap = EC_CAPACITY * t // N_EXPERTS
    eg = experts_per_step
    return pl.pallas_call(
        _gather_body,
        grid=(b, e // eg),
        in_specs=[
            pl.BlockSpec((None, eg, t), lambda bi, gi: (bi, gi, 0)),
            pl.BlockSpec((None, t, d), lambda bi, gi: (bi, 0, 0)),
        ],
        out_specs=pl.BlockSpec((eg, cap, d), lambda bi, gi: (gi, bi, 0)),
        out_shape=jax.ShapeDtypeStruct((e, b * cap, d), BF16),
        scratch_shapes=[pltpu.VMEM((eg * cap, t), BF16)],
        compiler_params=_cparams(2, 48),
        name="moe_gather",
    )(slot, hx)


def _window_starts(lo_ref, base, n_exp, cap):
    starts = []
    fits = None
    for e in range(n_exp):
        lo = lo_ref[base + e]
        hi = lo_ref[base + n_exp + e]
        start = jnp.minimum((lo // BF16_ROWS) * BF16_ROWS, cap - MOE_WINDOW)
        starts.append(pl.multiple_of(start, BF16_ROWS))
        ok = hi - start <= MOE_WINDOW
        fits = ok if fits is None else fits & ok
    return starts, fits


def _gather_win_body(lo_ref, slot_ref, h_ref, o_ref, onehot_ref):
    n_exp, cap, d = o_ref.shape
    tt = MOE_TILE
    tiles = h_ref.shape[0] // tt
    w = MOE_WINDOW
    i = pl.program_id(1)

    @pl.when(i == 0)
    def _():
        o_ref[...] = jnp.zeros_like(o_ref)

    for sub in range(tiles):
        tok = slice(sub * tt, (sub + 1) * tt)
        base = (pl.program_id(0) * (pl.num_programs(1) * tiles + 1) + i * tiles + sub) * n_exp
        starts, fits = _window_starts(lo_ref, base, n_exp, cap)

        @pl.when(fits)
        def _(tok=tok, starts=starts):
            rank = lax.broadcasted_iota(jnp.int32, (w, tt), 0)
            for e in range(n_exp):
                onehot_ref[e * w:(e + 1) * w, :] = jnp.where(
                    slot_ref[e:e + 1, tok] - starts[e] == rank, 1.0, 0.0).astype(BF16)
            rows = jnp.dot(onehot_ref[...], h_ref[tok, :], preferred_element_type=F32)
            for e in range(n_exp):
                o_ref[e, pl.ds(starts[e], w), :] += rows[e * w:(e + 1) * w].astype(o_ref.dtype)

        @pl.when(jnp.logical_not(fits))
        def _(tok=tok):
            rank = lax.broadcasted_iota(jnp.int32, (cap, tt), 0)
            for e in range(n_exp):
                onehot = jnp.where(slot_ref[e:e + 1, tok] == rank, 1.0, 0.0).astype(BF16)
                o_ref[e] += jnp.dot(onehot, h_ref[tok, :], preferred_element_type=F32).astype(o_ref.dtype)


def _gather_windowed(slot, tile_lo, hx):
    b, e, t = slot.shape
    d = hx.shape[2]
    cap = EC_CAPACITY * t // N_EXPERTS
    rows = MOE_TILE * MOE_TILES_PER_STEP
    return pl.pallas_call(
        _gather_win_body,
        grid_spec=pltpu.PrefetchScalarGridSpec(
            num_scalar_prefetch=1,
            grid=(b, t // rows),
            in_specs=[
                pl.BlockSpec((None, e, rows), lambda bi, ti, lo: (bi, 0, ti)),
                pl.BlockSpec((None, rows, d), lambda bi, ti, lo: (bi, ti, 0)),
            ],
            out_specs=pl.BlockSpec((e, cap, d), lambda bi, ti, lo: (0, bi, 0)),
            scratch_shapes=[pltpu.VMEM((e * MOE_WINDOW, MOE_TILE), BF16)],
        ),
        out_shape=jax.ShapeDtypeStruct((e, b * cap, d), BF16),
        compiler_params=_cparams(2, 48),
        name="moe_gather_windowed",
    )(tile_lo, slot, hx)


def _ffn_body(*refs, n_streams, row_chunk):
    xs_refs = refs[:n_streams]
    wg_ref, wu_ref, wd_ref = refs[n_streams:n_streams + 3]
    ys_refs = refs[n_streams + 3:2 * n_streams + 3]
    acc_refs = refs[2 * n_streams + 3:3 * n_streams + 3]
    wgb_ref, wub_ref, wdb_ref = refs[3 * n_streams + 3:]
    f = pl.program_id(1)
    wgb_ref[...] = wg_ref[...].astype(BF16)
    wub_ref[...] = wu_ref[...].astype(BF16)
    wdb_ref[...] = wd_ref[...].astype(BF16)
    last = pl.num_programs(1) - 1
    for xs_ref, ys_ref, acc_ref in zip(xs_refs, ys_refs, acc_refs):
        rc = min(row_chunk, xs_ref.shape[0])

        def rows(i, carry, mode, xs_ref=xs_ref, ys_ref=ys_ref, acc_ref=acc_ref, rc=rc):
            r0 = pl.multiple_of(i * rc, rc)
            xb = xs_ref[pl.ds(r0, rc), :]
            g = jnp.dot(xb, wgb_ref[...], preferred_element_type=F32)
            u = jnp.dot(xb, wub_ref[...], preferred_element_type=F32)
            hid = (_silu(g) * u).astype(BF16)
            part = jnp.dot(hid, wdb_ref[...], preferred_element_type=F32)
            if mode == "first":
                acc_ref[pl.ds(r0, rc), :] = part
            elif mode == "middle":
                acc_ref[pl.ds(r0, rc), :] += part
            else:
                ys_ref[pl.ds(r0, rc), :] = (acc_ref[pl.ds(r0, rc), :] + part).astype(ys_ref.dtype)
            return carry

        n_chunks = xs_ref.shape[0] // rc
        for mode, cond in (("first", f == 0), ("middle", (f > 0) & (f < last)), ("last", f == last)):
            @pl.when(cond)
            def _(mode=mode, rows=rows, n_chunks=n_chunks):
                lax.fori_loop(0, n_chunks, functools.partial(rows, mode=mode), 0)


def _expert_ffn(xs_list, w_gate, w_up, w_down, layer, *, tf=512, row_chunk=1024):
    _, e, d, ff = w_gate.shape
    assert ff % tf == 0 and ff // tf >= 2
    n = len(xs_list)
    xs_specs = [pl.BlockSpec((None, xs.shape[1], d), lambda ei, fi: (ei, 0, 0)) for xs in xs_list]
    return pl.pallas_call(
        functools.partial(_ffn_body, n_streams=n, row_chunk=row_chunk),
        grid=(e, ff // tf),
        in_specs=xs_specs + [
            pl.BlockSpec((None, None, d, tf), lambda ei, fi: (layer, ei, 0, fi)),
            pl.BlockSpec((None, None, d, tf), lambda ei, fi: (layer, ei, 0, fi)),
            pl.BlockSpec((None, None, tf, d), lambda ei, fi: (layer, ei, fi, 0)),
        ],
        out_specs=xs_specs,
        out_shape=[jax.ShapeDtypeStruct(xs.shape, BF16) for xs in xs_list],
        scratch_shapes=[pltpu.VMEM(xs.shape[1:], F32) for xs in xs_list]
        + [pltpu.VMEM((d, tf), BF16), pltpu.VMEM((d, tf), BF16), pltpu.VMEM((tf, d), BF16)],
        compiler_params=_cparams(2, 56),
        name="expert_ffn",
    )(*xs_list, w_gate, w_up, w_down)


def _combine_body(ys_ref, slot_ref, wsel_ref, x_ref, g2_ref, *rest, final_norm):
    if final_norm:
        fg_ref, o_ref = rest
    else:
        (o_ref,) = rest
    n_exp, cap, _ = ys_ref.shape
    tm = x_ref.shape[0]
    rank = lax.broadcasted_iota(jnp.int32, (tm, cap), 1)
    slots = slot_ref[...]
    wsel = wsel_ref[...]
    o_ref[...] = jnp.zeros_like(o_ref)
    for e in range(n_exp):
        onehot = jnp.where(slots[:, e:e + 1] == rank, 1.0, 0.0).astype(BF16)
        o_ref[...] += wsel[:, e:e + 1] * jnp.dot(onehot, ys_ref[e], preferred_element_type=F32)
    out = x_ref[...] + g2_ref[...] * o_ref[...]
    if final_norm:
        out = out * lax.rsqrt(jnp.mean(out * out, axis=-1, keepdims=True) + NORM_EPS) * fg_ref[...]
    o_ref[...] = out


def _combine(ys, slot_t, wsel_t, x, mod, layer, row_of_batch, final_g=None, *, tm=ROW_TILE):
    b, t, d = x.shape
    e = ys.shape[0]
    cap = ys.shape[1] // b
    tm = min(tm, t)
    in_specs = [
        pl.BlockSpec((e, cap, d), lambda bi, ti: (0, bi, 0)),
        pl.BlockSpec((None, tm, e), lambda bi, ti: (bi, ti, 0)),
        pl.BlockSpec((None, tm, e), lambda bi, ti: (bi, ti, 0)),
        pl.BlockSpec((None, tm, d), lambda bi, ti: (bi, ti, 0)),
        _mod_spec(layer, 5, d, row_of_batch, 2, 0),
    ]
    args = [ys, slot_t, wsel_t, x, mod]
    if final_g is not None:
        in_specs.append(pl.BlockSpec((1, d), lambda bi, ti: (0, 0)))
        args.append(final_g.reshape(1, d))
    return pl.pallas_call(
        functools.partial(_combine_body, final_norm=final_g is not None),
        grid=(b, t // tm),
        in_specs=in_specs,
        out_specs=pl.BlockSpec((None, tm, d), lambda bi, ti: (bi, ti, 0)),
        out_shape=jax.ShapeDtypeStruct((b, t, d), F32),
        compiler_params=_cparams(2, 48),
        name="moe_combine",
    )(*args)


def _combine_win_body(lo_ref, ys_ref, slot_ref, wsel_ref, x_ref, g2_ref, *rest, final_norm):
    if final_norm:
        fg_ref, o_ref, ywin_ref, scat_ref = rest
    else:
        o_ref, ywin_ref, scat_ref = rest
    n_exp, cap, d = ys_ref.shape
    tt = MOE_TILE
    tiles = x_ref.shape[0] // tt
    w = MOE_WINDOW
    i = pl.program_id(1)

    def finish(tok, acc):
        out = x_ref[tok, :] + g2_ref[...] * acc
        if final_norm:
            out = out * lax.rsqrt(jnp.mean(out * out, axis=-1, keepdims=True) + NORM_EPS) * fg_ref[...]
        o_ref[tok, :] = out

    for sub in range(tiles):
        tok = slice(sub * tt, (sub + 1) * tt)
        base = (pl.program_id(0) * (pl.num_programs(1) * tiles + 1) + i * tiles + sub) * n_exp
        starts, fits = _window_starts(lo_ref, base, n_exp, cap)

        @pl.when(fits)
        def _(tok=tok, starts=starts):
            rank = lax.broadcasted_iota(jnp.int32, (w, tt), 0)
            for e in range(n_exp):
                ywin_ref[e * w:(e + 1) * w, :] = ys_ref[e, pl.ds(starts[e], w), :]
                scat_ref[e * w:(e + 1) * w, :] = jnp.where(
                    slot_ref[e:e + 1, tok] - starts[e] == rank, wsel_ref[e:e + 1, tok], 0.0).astype(BF16)
            finish(tok, lax.dot_general(scat_ref[...], ywin_ref[...], TN_DIMS,
                                        preferred_element_type=F32))

        @pl.when(jnp.logical_not(fits))
        def _(tok=tok):
            rank = lax.broadcasted_iota(jnp.int32, (cap, tt), 0)
            acc = jnp.zeros((tt, d), F32)
            for e in range(n_exp):
                scat = jnp.where(slot_ref[e:e + 1, tok] == rank, wsel_ref[e:e + 1, tok], 0.0).astype(BF16)
                acc = acc + lax.dot_general(scat, ys_ref[e], TN_DIMS, preferred_element_type=F32)
            finish(tok, acc)


def _combine_windowed(ys, slot, wsel, tile_lo, x, mod, layer, row_of_batch, final_g=None):
    b, t, d = x.shape
    e = ys.shape[0]
    cap = ys.shape[1] // b
    rows = MOE_TILE * MOE_TILES_PER_STEP
    tile = lambda bi, ti, lo: (bi, ti, 0)
    routing = pl.BlockSpec((None, e, rows), lambda bi, ti, lo: (bi, 0, ti))
    in_specs = [
        pl.BlockSpec((e, cap, d), lambda bi, ti, lo: (0, bi, 0)),
        routing,
        routing,
        pl.BlockSpec((None, rows, d), tile),
        _mod_spec(layer, 5, d, row_of_batch, 2, 0),
    ]
    args = [ys, slot, wsel, x, mod]
    if final_g is not None:
        in_specs.append(pl.BlockSpec((1, d), lambda bi, ti, lo: (0, 0)))
        args.append(final_g.reshape(1, d))
    return pl.pallas_call(
        functools.partial(_combine_win_body, final_norm=final_g is not None),
        grid_spec=pltpu.PrefetchScalarGridSpec(
            num_scalar_prefetch=1,
            grid=(b, t // rows),
            in_specs=in_specs,
            out_specs=pl.BlockSpec((None, rows, d), tile),
            scratch_shapes=[pltpu.VMEM((e * MOE_WINDOW, d), BF16),
                            pltpu.VMEM((e * MOE_WINDOW, MOE_TILE), BF16)],
        ),
        out_shape=jax.ShapeDtypeStruct((b, t, d), F32),
        compiler_params=_cparams(2, 48),
        name="moe_combine_windowed",
    )(tile_lo, *args)


def _attn_body(sink_ref, q_ref, k_ref, v_ref, kc_ref, vc_ref, o_ref):
    blk = SWA_BLOCK
    hd = SWA_HEAD_DIM
    win = 3 * blk
    t = k_ref.shape[0]
    t_ctx = kc_ref.shape[0]
    n_heads = q_ref.shape[1] // hd
    group = n_heads // SWA_KV_HEADS
    blocks_per_step = q_ref.shape[0] // blk
    rel = (lax.broadcasted_iota(jnp.int32, (blk, win), 0)
           - lax.broadcasted_iota(jnp.int32, (blk, win), 1))
    head_idx = lax.broadcasted_iota(jnp.int32, (group, 1, 1), 0)

    def query_block(sub, carry):
        n = pl.program_id(1) * blocks_per_step + sub
        q0 = pl.multiple_of(sub * blk, blk)
        w0 = pl.multiple_of(jnp.clip((n - 1) * blk, 0, t - win), blk)
        dist = rel + (n * blk - w0)
        bias = jnp.where(jnp.abs(dist) <= SWA_WINDOW, 0.0, -jnp.inf)[None]
        for kv in range(SWA_KV_HEADS):
            heads = range(kv * group, (kv + 1) * group)
            cols = slice(kv * hd, (kv + 1) * hd)
            qg = jnp.concatenate([q_ref[pl.ds(q0, blk), h * hd:(h + 1) * hd] for h in heads], axis=0)
            s_ctx = lax.dot_general(qg, kc_ref[:, cols], NT_DIMS, preferred_element_type=F32)
            s_win = lax.dot_general(qg, k_ref[pl.ds(w0, win), cols], NT_DIMS, preferred_element_type=F32)
            s = jnp.concatenate([s_ctx.reshape(group, blk, t_ctx),
                                 s_win.reshape(group, blk, win) + bias], axis=-1)
            sink = jnp.zeros((group, 1, 1), F32)
            for i, h in enumerate(heads):
                sink = jnp.where(head_idx == i, sink_ref[h], sink)
            sink = sink * LOG2E
            m = jnp.maximum(jnp.max(s, axis=-1, keepdims=True), sink)
            p = jnp.exp2(s - m)
            den = jnp.sum(p, axis=-1, keepdims=True) + jnp.exp2(sink - m)
            pb = p.reshape(group * blk, t_ctx + win).astype(BF16)
            o = (jnp.dot(pb[:, :t_ctx], vc_ref[:, cols], preferred_element_type=F32)
                 + jnp.dot(pb[:, t_ctx:], v_ref[pl.ds(w0, win), cols], preferred_element_type=F32))
            o = o.reshape(group, blk, hd) / den
            for i, h in enumerate(heads):
                o_ref[pl.ds(q0, blk), h * hd:(h + 1) * hd] = o[i].astype(o_ref.dtype)
        return carry

    lax.fori_loop(0, blocks_per_step, query_block, 0, unroll=2)


def _window_attention(p, pc, sink, *, blocks_per_step=4):
    b, t, cols = p.shape
    t_ctx = pc.shape[1]
    kvw = SWA_KV_HEADS * SWA_HEAD_DIM
    qw = cols - 2 * kvw
    assert t >= 3 * SWA_BLOCK
    rows = blocks_per_step * SWA_BLOCK
    return pl.pallas_call(
        _attn_body,
        grid=(b, t // rows),
        in_specs=[
            pl.BlockSpec(memory_space=pltpu.SMEM),
            pl.BlockSpec((None, rows, qw), lambda bi, ni: (bi, ni, 0)),
            pl.BlockSpec((None, t, kvw), lambda bi, ni: (bi, 0, qw // kvw)),
            pl.BlockSpec((None, t, kvw), lambda bi, ni: (bi, 0, qw // kvw + 1)),
            pl.BlockSpec((None, t_ctx, kvw), lambda bi, ni: (bi, 0, 0)),
            pl.BlockSpec((None, t_ctx, kvw), lambda bi, ni: (bi, 0, 1)),
        ],
        out_specs=pl.BlockSpec((None, rows, qw), lambda bi, ni: (bi, ni, 0)),
        out_shape=jax.ShapeDtypeStruct((b, t, qw), BF16),
        compiler_params=_cparams(2, 32),
        name="window_attention",
    )(sink.astype(F32), p, p, p, pc, pc)


def _rope_tables(t):
    n_freq = SWA_HEAD_DIM // 4
    pos = jnp.arange(t)
    row = (pos // GRID_W).astype(F32)
    col = (pos % GRID_W).astype(F32)
    inv_freq = ROPE_BASE ** (-jnp.arange(n_freq, dtype=F32) / n_freq)
    ang_r, ang_c = row[:, None] * inv_freq, col[:, None] * inv_freq
    zero = jnp.zeros_like(ang_r)
    cos = jnp.concatenate([jnp.cos(ang_r)] * 2 + [jnp.cos(ang_c)] * 2, axis=1)
    sin_minus = jnp.concatenate([-jnp.sin(ang_r), zero, -jnp.sin(ang_c), zero], axis=1)
    sin_plus = jnp.concatenate([zero, jnp.sin(ang_r), zero, jnp.sin(ang_c)], axis=1)
    reps = LANES // SWA_HEAD_DIM
    return tuple(jnp.tile(tab, (1, reps)) for tab in (cos, sin_minus, sin_plus))


def _use_windows(t):
    cap = EC_CAPACITY * t // N_EXPERTS
    assert 2 * MOE_WINDOW == LANES
    return (t % (MOE_TILE * MOE_TILES_PER_STEP) == 0 and cap >= MOE_WINDOW
            and (cap - MOE_WINDOW) % BF16_ROWS == 0)


def _moe(streams, w_gate, w_up, w_down, layer):
    routed = []
    for hx, aff in streams:
        slot, wsel, tile_lo = _topk_select(aff)
        windowed = _use_windows(hx.shape[1])
        xs = _gather_windowed(slot, tile_lo, hx) if windowed else _gather(slot, hx)
        routed.append((xs, slot, wsel, tile_lo, windowed))
    ys_list = _expert_ffn([r[0] for r in routed], w_gate, w_up, w_down, layer)

    def make_combine(ys, slot, wsel, tile_lo, windowed):
        def combine(x, mod, row_of_batch, final_g=None):
            if windowed:
                return _combine_windowed(ys, slot, wsel, tile_lo, x, mod, layer, row_of_batch, final_g)
            return _combine(ys, jnp.swapaxes(slot, 1, 2), jnp.swapaxes(wsel, 1, 2), x, mod, layer,
                            row_of_batch, final_g)
        return combine

    return [make_combine(ys, *r[1:]) for ys, r in zip(ys_list, routed)]


def kernel(x, c, ctx, c_ctx, ada_w, ada_b, norm1_g, norm2_g, final_g, ret_w_in, ret_decay_f, ret_decay_b, ret_gn_f, ret_gn_b, ret_w_out, swa_w_qkv, swa_sink, swa_w_out, moe_router, moe_w_gate, moe_w_up, moe_w_down):
    b, t, d = x.shape
    lat_row = lambda bi: bi
    ctx_row = lambda bi: b
    cond = jnp.zeros((COND_ROWS, d), F32).at[:b].set(c).at[b].set(c_ctx)
    mod = _adaln(cond, ada_w, ada_b)

    ret_tn = 2048
    gate_col0 = (ret_w_in.shape[2] // 8) * 4
    assert gate_col0 % ret_tn == 0
    gate_gain = jnp.concatenate([jnp.ones((gate_col0,), F32), ret_gn_f[0].astype(F32),
                                 ret_gn_b[0].astype(F32)]).reshape(1, -1)
    ret_proj = dict(tm=ROW_TILE, tn=ret_tn, silu_from_tile=gate_col0 // ret_tn, silu_gain=gate_gain)
    p = _norm_mod_matmul(x, norm1_g[0], mod, 0, lat_row, ret_w_in[0], **ret_proj)
    t_ctx = ctx.shape[1]
    flat = lambda a: a.reshape(1, b * t_ctx, a.shape[-1])
    unflat = lambda a: a.reshape(b, t_ctx, a.shape[-1])
    pc = unflat(_norm_mod_matmul(flat(ctx), norm1_g[0], mod, 0, ctx_row, ret_w_in[0], **ret_proj))
    y, yc = _retention(p, pc, ret_decay_f[0], ret_decay_b[0])
    x1, hx, aff = _outproj_router(y, ret_w_out[0], x, mod, 0, lat_row, norm2_g[0], moe_router[0], tm=ROW_TILE)
    c1, hc, affc = _outproj_router(flat(yc), ret_w_out[0], flat(ctx), mod, 0, ctx_row, norm2_g[0],
                                   moe_router[0], tm=ROW_TILE)
    c1, hc = unflat(c1), unflat(hc)
    affc = jnp.swapaxes(affc.reshape(affc.shape[1], b, t_ctx), 0, 1)
    combine_lat, combine_ctx = _moe([(hx, aff), (hc, affc)], moe_w_gate, moe_w_up, moe_w_down, 0)
    x2 = combine_lat(x1, mod, lat_row)
    c2 = combine_ctx(c1, mod, ctx_row)

    qw = SWA_HEAD_DIM * (swa_w_qkv.shape[2] // SWA_HEAD_DIM - 2 * SWA_KV_HEADS)
    cos, sin_minus, sin_plus = _rope_tables(t)
    rope_groups = (qw + SWA_KV_HEADS * SWA_HEAD_DIM) // LANES
    p = _norm_mod_matmul(x2, norm1_g[1], mod, 1, lat_row, swa_w_qkv[0], tm=ROW_TILE,
                         tn=swa_w_qkv.shape[2],
                         rope=(cos, sin_minus, sin_plus, rope_groups,
                               (qw // LANES, SWA_HEAD_DIM ** -0.5 * LOG2E)))
    w_kv = swa_w_qkv[0][:, qw:]
    pc = unflat(_norm_mod_matmul(flat(c2), norm1_g[1], mod, 1, ctx_row, w_kv, tm=ROW_TILE, tn=w_kv.shape[1]))
    o = _window_attention(p, pc, swa_sink[0])
    x3, hx, aff = _outproj_router(o, swa_w_out[0], x2, mod, 1, lat_row, norm2_g[1], moe_router[1], tm=ROW_TILE)
    (combine_lat,) = _moe([(hx, aff)], moe_w_gate, moe_w_up, moe_w_down, 1)
    return combine_lat(x3, mod, lat_row, final_g)
```

```python
import functools

import jax
import jax.numpy as jnp
from jax import lax
from jax.experimental import pallas as pl
from jax.experimental.pallas import tpu as pltpu

F32 = jnp.float32
BF16 = jnp.bfloat16

NORM_EPS = 1e-6
GRID_W = 64
RET_HEADS = 4
RET_CHUNK = 256
SWA_HEAD_DIM = 64
SWA_KV_HEADS = 2
SWA_WINDOW = 128
SWA_BLOCK = SWA_WINDOW
ROPE_BASE = 10000.0
N_EXPERTS = 16
EC_CAPACITY = 2

ROW_TILE = 1024
MOE_TILE = 256
MOE_WINDOW = 64
MOE_TILES_PER_STEP = 4

LANES = 128
BF16_ROWS = 16
MIB = 1024 * 1024
COND_ROWS = 16

LOG2E = 1.4426950408889634

NT_DIMS = (((1,), (1,)), ((), ()))
TN_DIMS = (((0,), (0,)), ((), ()))


def _cparams(n_axes, vmem_mib):
    return pltpu.CompilerParams(
        dimension_semantics=("arbitrary",) * n_axes,
        vmem_limit_bytes=vmem_mib * MIB,
    )


def _silu(v):
    half = 0.5 * v
    return half + half * jnp.tanh(half)


def _rmsnorm_mod(xv, g, shift, scale):
    h = xv * lax.rsqrt(jnp.mean(xv * xv, axis=-1, keepdims=True) + NORM_EPS) * g
    return h * (1.0 + scale) + shift


def _mod_spec(layer, chunk, d, row_of_batch, n_grid_axes, batch_axis):
    def index_map(*ids):
        return (layer, row_of_batch(ids[batch_axis]), 0, chunk)
    del n_grid_axes
    return pl.BlockSpec((None, None, 1, d), index_map)


def _adaln_body(cond_ref, w_ref, b_ref, o_ref):
    s = _silu(cond_ref[...])
    o_ref[...] = jnp.dot(s.astype(BF16), w_ref[...].astype(BF16),
                         preferred_element_type=F32) + b_ref[...]


def _adaln(cond, ada_w, ada_b):
    layers, d, n = ada_w.shape
    tn = d
    out = pl.pallas_call(
        _adaln_body,
        grid=(layers, n // tn),
        in_specs=[
            pl.BlockSpec((COND_ROWS, d), lambda l, j: (0, 0)),
            pl.BlockSpec((None, d, tn), lambda l, j: (l, 0, j)),
            pl.BlockSpec((None, 1, tn), lambda l, j: (l, 0, j)),
        ],
        out_specs=pl.BlockSpec((None, COND_ROWS, tn), lambda l, j: (l, 0, j)),
        out_shape=jax.ShapeDtypeStruct((layers, COND_ROWS, n), F32),
        compiler_params=_cparams(2, 32),
        name="adaln",
    )(cond, ada_w, ada_b.reshape(layers, 1, n))
    return out.reshape(layers, COND_ROWS, 1, n)


def _nmm_body(x_ref, g_ref, sh_ref, sc_ref, w_ref, *rest, rope_groups, q_scaling, silu_from_tile):
    if rope_groups:
        cos_ref, sm_ref, sp_ref, o_ref, wbf_ref = rest
    elif silu_from_tile is not None:
        gain_ref, o_ref, wbf_ref = rest
    else:
        o_ref, wbf_ref = rest

    @pl.when((pl.program_id(1) == 0) & (pl.program_id(2) == 0))
    def _():
        wbf_ref[...] = w_ref[...].astype(BF16)

    def project():
        h = _rmsnorm_mod(x_ref[...], g_ref[...], sh_ref[...], sc_ref[...])
        return jnp.dot(h.astype(BF16), wbf_ref[...], preferred_element_type=F32)

    if silu_from_tile is not None:
        @pl.when(pl.program_id(0) < silu_from_tile)
        def _():
            o_ref[...] = project().astype(o_ref.dtype)

        @pl.when(pl.program_id(0) >= silu_from_tile)
        def _():
            o_ref[...] = (_silu(project()) * gain_ref[...]).astype(o_ref.dtype)

        return
    acc = project()
    if not rope_groups:
        o_ref[...] = acc.astype(o_ref.dtype)
        return
    c, sm, sp = cos_ref[...], sm_ref[...], sp_ref[...]
    n_groups = acc.shape[1] // LANES
    for j in range(n_groups):
        a = acc[:, j * LANES:(j + 1) * LANES]
        if j < rope_groups:
            a = (a * c + pltpu.roll(a, LANES - 16, axis=1) * sm
                 + pltpu.roll(a, 16, axis=1) * sp)
        if j < q_scaling[0]:
            a = a * q_scaling[1]
        o_ref[:, j * LANES:(j + 1) * LANES] = a.astype(o_ref.dtype)


def _norm_mod_matmul(x, g, mod, layer, row_of_batch, w, *, tm, tn, rope=None, silu_from_tile=None,
                     silu_gain=None):
    b, t, d = x.shape
    n = w.shape[1]
    tm = min(tm, t)
    grid = (n // tn, b, t // tm)
    in_specs = [
        pl.BlockSpec((None, tm, d), lambda j, bi, ti: (bi, ti, 0)),
        pl.BlockSpec((1, d), lambda j, bi, ti: (0, 0)),
        _mod_spec(layer, 0, d, row_of_batch, 3, 1),
        _mod_spec(layer, 1, d, row_of_batch, 3, 1),
        pl.BlockSpec((d, tn), lambda j, bi, ti: (0, j)),
    ]
    args = [x, g.reshape(1, d), mod, mod, w]
    rope_groups, q_scaling = 0, (0, 1.0)
    if rope is not None:
        cos, sm, sp, rope_groups, q_scaling = rope
        for tab in (cos, sm, sp):
            in_specs.append(pl.BlockSpec((tm, LANES), lambda j, bi, ti: (ti, 0)))
            args.append(tab)
    if silu_from_tile is not None:
        assert rope is None and silu_gain.shape == (1, n)
        in_specs.append(pl.BlockSpec((1, tn), lambda j, bi, ti: (0, j)))
        args.append(silu_gain)
    return pl.pallas_call(
        functools.partial(_nmm_body, rope_groups=rope_groups, q_scaling=q_scaling,
                          silu_from_tile=silu_from_tile),
        grid=grid,
        in_specs=in_specs,
        out_specs=pl.BlockSpec((None, tm, tn), lambda j, bi, ti: (bi, ti, j)),
        out_shape=jax.ShapeDtypeStruct((b, t, n), BF16),
        scratch_shapes=[pltpu.VMEM((d, tn), BF16)],
        compiler_params=_cparams(3, 48),
        name="norm_mod_proj",
    )(*args)


def _ret_body(dec_ref,
              qc_ref, kc_ref, vc_ref, gfc_ref, gbc_ref,
              q_ref, k_ref, v_ref, gf_ref, gb_ref,
              y_ref, yc_ref,
              statef_ref, stateb_ref, yacc_ref, intra_ref, qd_ref, kd_ref, cd_ref):
    c = RET_CHUNK
    dk = q_ref.shape[1]
    dv = v_ref.shape[1]
    t_ctx = qc_ref.shape[0]
    t_lat = q_ref.shape[0]

    dec = dec_ref[...]
    lg = jnp.minimum(dec, 0.0) - jnp.log1p(jnp.exp(-jnp.abs(dec)))
    row = lax.broadcasted_iota(jnp.int32, (c, c), 0)
    col = lax.broadcasted_iota(jnp.int32, (c, c), 1)
    pos_v = lax.broadcasted_iota(jnp.int32, (c, dv), 0).astype(F32)
    pos_k = lax.broadcasted_iota(jnp.int32, (c, dk), 0).astype(F32)
    for d in range(2):
        lg_c, lg_v, lg_k = lg[d:d + 1, :c], lg[d:d + 1, :], lg[d:d + 1, :dk]
        diff = (row - col) if d == 0 else (col - row)
        intra_ref[d] = jnp.where(diff >= 0, jnp.exp(jnp.maximum(diff, 0).astype(F32) * lg_c), 0.0)
        if d == 0:
            qd_ref[d] = jnp.exp((pos_v + 1.0) * lg_v)
            kd_ref[d] = jnp.exp((c - 1.0 - pos_k) * lg_k)
        else:
            qd_ref[d] = jnp.exp((c - pos_v) * lg_v)
            kd_ref[d] = jnp.exp(pos_k * lg_k)
        cd_ref[d] = jnp.exp(float(c) * lg_v)

    def chunk(d, refs, r0, yoff, out_ref, first_visit):
        qr, kr, vr, gr = refs
        state_ref = statef_ref if d == 0 else stateb_ref
        if not isinstance(r0, int):
            r0, yoff = pl.multiple_of(r0, c), pl.multiple_of(yoff, c)
        qs = qr[pl.ds(r0, c), :] * jnp.asarray(dk ** -0.5, BF16)
        kc = kr[pl.ds(r0, c), :]
        vc = vr[pl.ds(r0, c), :]
        s = lax.dot_general(qs, kc, NT_DIMS, preferred_element_type=F32) * intra_ref[d]
        st = state_ref[...]
        o = (jnp.dot(s.astype(BF16), vc, preferred_element_type=F32)
             + jnp.dot(qs, st.astype(BF16), preferred_element_type=F32) * qd_ref[d])
        kk = (kc.astype(F32) * kd_ref[d]).astype(BF16)
        state_ref[...] = cd_ref[d] * st + lax.dot_general(kk, vc, TN_DIMS, preferred_element_type=F32)
        mu = jnp.mean(o, axis=-1, keepdims=True)
        oc = o - mu
        var = jnp.mean(oc * oc, axis=-1, keepdims=True)
        y = gr[pl.ds(r0, c), :].astype(F32) * (oc * lax.rsqrt(var + NORM_EPS))
        if first_visit:
            yacc_ref[pl.ds(yoff, c), :] = y
        else:
            out_ref[pl.ds(r0, c), :] = (yacc_ref[pl.ds(yoff, c), :] + y).astype(out_ref.dtype)

    statef_ref[...] = jnp.zeros_like(statef_ref)
    stateb_ref[...] = jnp.zeros_like(stateb_ref)
    segments = (
        ((qc_ref, kc_ref, vc_ref), (gfc_ref, gbc_ref), t_ctx // c, 0, yc_ref),
        ((q_ref, k_ref, v_ref), (gf_ref, gb_ref), t_lat // c, t_ctx, y_ref),
    )
    for qkv, gates, n, ybase, out_ref in segments:
        half = n // 2

        def step(i, carry, first_visit, qkv=qkv, gates=gates, n=n, ybase=ybase, out_ref=out_ref):
            for d in range(2):
                ci = i if d == 0 else n - 1 - i
                chunk(d, qkv + (gates[d],), ci * c, ybase + ci * c, out_ref, first_visit)
            return carry

        unroll = 4 if half >= 4 else 1
        if half:
            lax.fori_loop(0, half, functools.partial(step, first_visit=True), 0, unroll=unroll)
        if n % 2:
            for d in range(2):
                chunk(d, qkv + (gates[d],), half * c, ybase + half * c, out_ref, d == 0)
        if half:
            lax.fori_loop(n - half, n, functools.partial(step, first_visit=False), 0, unroll=unroll)


def _retention(p, pc, decay_f, decay_b):
    b, t, cols = p.shape
    t_ctx = pc.shape[1]
    assert t % RET_CHUNK == 0 and t_ctx % RET_CHUNK == 0
    h = RET_HEADS
    dk = cols // (8 * h)
    dv = 2 * dk
    hv = h * dv
    dec = jnp.broadcast_to(jnp.stack([decay_f, decay_b], axis=1)[:, :, None], (h, 2, dv)).astype(F32)
    k_blk, v_blk, gf_blk, gb_blk = h, (2 * h * dk) // dv, (2 * h * dk + hv) // dv, (2 * h * dk + 2 * hv) // dv

    def specs(rows):
        return [
            pl.BlockSpec((None, rows, dk), lambda bi, hi: (bi, 0, hi)),
            pl.BlockSpec((None, rows, dk), lambda bi, hi: (bi, 0, k_blk + hi)),
            pl.BlockSpec((None, rows, dv), lambda bi, hi: (bi, 0, v_blk + hi)),
            pl.BlockSpec((None, rows, dv), lambda bi, hi: (bi, 0, gf_blk + hi)),
            pl.BlockSpec((None, rows, dv), lambda bi, hi: (bi, 0, gb_blk + hi)),
        ]

    return pl.pallas_call(
        _ret_body,
        grid=(b, h),
        in_specs=[pl.BlockSpec((None, 2, dv), lambda bi, hi: (hi, 0, 0))] + specs(t_ctx) + specs(t),
        out_specs=[
            pl.BlockSpec((None, t, dv), lambda bi, hi: (bi, 0, hi)),
            pl.BlockSpec((None, t_ctx, dv), lambda bi, hi: (bi, 0, hi)),
        ],
        out_shape=[
            jax.ShapeDtypeStruct((b, t, hv), BF16),
            jax.ShapeDtypeStruct((b, t_ctx, hv), BF16),
        ],
        scratch_shapes=[
            pltpu.VMEM((dk, dv), F32),
            pltpu.VMEM((dk, dv), F32),
            pltpu.VMEM((t_ctx + t, dv), F32),
            pltpu.VMEM((2, RET_CHUNK, RET_CHUNK), F32),
            pltpu.VMEM((2, RET_CHUNK, dv), F32),
            pltpu.VMEM((2, RET_CHUNK, dk), F32),
            pltpu.VMEM((2, 1, dv), F32),
        ],
        compiler_params=_cparams(2, 48),
        name="retention",
    )(dec, pc, pc, pc, pc, pc, p, p, p, p, p)


def _outproj_body(y_ref, w_ref, x_ref, g1_ref, ng_ref, sh_ref, sc_ref, wr_ref,
                  x1_ref, hx_ref, aff_ref, wbf_ref):
    @pl.when((pl.program_id(0) == 0) & (pl.program_id(1) == 0))
    def _():
        wbf_ref[...] = w_ref[...].astype(BF16)

    x1 = x_ref[...] + g1_ref[...] * jnp.dot(y_ref[...], wbf_ref[...], preferred_element_type=F32)
    x1_ref[...] = x1
    h = _rmsnorm_mod(x1, ng_ref[...], sh_ref[...], sc_ref[...])
    hb = h.astype(BF16)
    hx_ref[...] = hb
    hlo = (h - hb.astype(F32)).astype(BF16)
    wr = wr_ref[...]
    whi = wr.astype(BF16)
    wlo = (wr - whi.astype(F32)).astype(BF16)
    n_exp = wr.shape[0]
    with_hb = lax.dot_general(jnp.concatenate([whi, wlo], axis=0), hb, NT_DIMS,
                              preferred_element_type=F32)
    logits = (with_hb[:n_exp] + with_hb[n_exp:]
              + lax.dot_general(whi, hlo, NT_DIMS, preferred_element_type=F32))
    e = jnp.exp(logits - jnp.max(logits, axis=0, keepdims=True))
    aff_ref[...] = e / jnp.sum(e, axis=0, keepdims=True)


def _outproj_router(y, w, x, mod, layer, row_of_batch, norm_g, w_router, *, tm):
    b, t, kdim = y.shape
    d = w.shape[1]
    e = w_router.shape[1]
    tm = min(tm, t)
    row = lambda bi, ti: (0, 0)
    return pl.pallas_call(
        _outproj_body,
        grid=(b, t // tm),
        in_specs=[
            pl.BlockSpec((None, tm, kdim), lambda bi, ti: (bi, ti, 0)),
            pl.BlockSpec((kdim, d), row, pipeline_mode=pl.Buffered(1)),
            pl.BlockSpec((None, tm, d), lambda bi, ti: (bi, ti, 0)),
            _mod_spec(layer, 2, d, row_of_batch, 2, 0),
            pl.BlockSpec((1, d), row),
            _mod_spec(layer, 3, d, row_of_batch, 2, 0),
            _mod_spec(layer, 4, d, row_of_batch, 2, 0),
            pl.BlockSpec((e, d), row),
        ],
        out_specs=[
            pl.BlockSpec((None, tm, d), lambda bi, ti: (bi, ti, 0)),
            pl.BlockSpec((None, tm, d), lambda bi, ti: (bi, ti, 0)),
            pl.BlockSpec((None, e, tm), lambda bi, ti: (bi, 0, ti)),
        ],
        out_shape=[
            jax.ShapeDtypeStruct((b, t, d), F32),
            jax.ShapeDtypeStruct((b, t, d), BF16),
            jax.ShapeDtypeStruct((b, e, t), F32),
        ],
        scratch_shapes=[pltpu.VMEM((kdim, d), BF16)],
        compiler_params=_cparams(2, 48),
        name="outproj_router",
    )(y, w, x, mod, norm_g.reshape(1, d), mod, mod, w_router.T)


def _topk_body(aff_ref, slot_ref, wsel_ref, lo_ref, *, cap):
    a = aff_ref[...]
    r, t = a.shape
    capf = float(cap)

    def count_ge(v):
        return jnp.sum(jnp.where(a >= v, 1.0, 0.0), axis=1, keepdims=True)

    def bisect(_, lohi):
        lo, hi = lohi
        mid = 0.5 * (lo + hi)
        ok = count_ge(mid) >= capf
        return jnp.where(ok, mid, lo), jnp.where(ok, hi, mid)

    lo, hi = lax.fori_loop(0, 32, bisect, (jnp.zeros((r, 1), F32), jnp.full((r, 1), 2.0, F32)))

    def unfinished(state):
        return jnp.min(state[2]) < 0.5

    def walk(state):
        lo, hi, done = state
        top = jnp.max(jnp.where(a < hi, a, -1.0), axis=1, keepdims=True)
        found = (count_ge(top) >= capf) & (done < 0.5)
        moved = (done < 0.5) & jnp.logical_not(found)
        return jnp.where(found, top, lo), jnp.where(moved, top, hi), jnp.where(found, 1.0, done)

    thr, _, _ = lax.while_loop(unfinished, walk, (lo, hi, jnp.zeros((r, 1), F32)))

    gt = a > thr
    eq = a == thr
    need = capf - jnp.sum(jnp.where(gt, 1.0, 0.0), axis=1, keepdims=True)
    upper = jnp.where(lax.broadcasted_iota(jnp.int32, (LANES, LANES), 0)
                      < lax.broadcasted_iota(jnp.int32, (LANES, LANES), 1), 1.0, 0.0).astype(BF16)
    carry_eq = jnp.zeros((r, 1), F32)
    carry_sel = jnp.zeros((r, 1), F32)
    lane = lax.broadcasted_iota(jnp.int32, (r, LANES), 1)
    tile_lo = jnp.zeros((r, LANES), F32)
    for j in range(t // LANES):
        sl = slice(j * LANES, (j + 1) * LANES)
        if (j * LANES) % MOE_TILE == 0:
            tile_lo = jnp.where(lane == (j * LANES) // MOE_TILE, carry_sel, tile_lo)
        eqf = jnp.where(eq[:, sl], 1.0, 0.0)
        pre_eq = jnp.dot(eqf.astype(BF16), upper, preferred_element_type=F32) + carry_eq
        self_ = jnp.where(gt[:, sl], 1.0, jnp.where(pre_eq < need, eqf, 0.0))
        pre_sel = jnp.dot(self_.astype(BF16), upper, preferred_element_type=F32) + carry_sel
        sel = self_ > 0.5
        slot_ref[:, sl] = jnp.where(sel, pre_sel, -1.0).astype(jnp.int32)
        wsel_ref[:, sl] = jnp.where(sel, a[:, sl], 0.0)
        carry_eq = carry_eq + jnp.sum(eqf, axis=1, keepdims=True)
        carry_sel = carry_sel + jnp.sum(self_, axis=1, keepdims=True)
    tile_lo = jnp.where(lane == t // MOE_TILE, carry_sel, tile_lo)
    lo_ref[...] = tile_lo.astype(jnp.int32)


def _topk_select(aff):
    b, e, t = aff.shape
    cap = EC_CAPACITY * t // N_EXPERTS
    r = b * e
    n_tiles = t // MOE_TILE
    assert t % MOE_TILE == 0 and n_tiles < LANES
    full = pl.BlockSpec((r, t), lambda i: (0, 0))
    slot, wsel, lo = pl.pallas_call(
        functools.partial(_topk_body, cap=cap),
        grid=(1,),
        in_specs=[full],
        out_specs=[full, full, pl.BlockSpec((r, LANES), lambda i: (0, 0))],
        out_shape=[jax.ShapeDtypeStruct((r, t), jnp.int32), jax.ShapeDtypeStruct((r, t), F32),
                   jax.ShapeDtypeStruct((r, LANES), jnp.int32)],
        compiler_params=_cparams(1, 32),
        name="topk_select",
    )(aff.reshape(r, t))
    tile_lo = jnp.swapaxes(lo.reshape(b, e, LANES)[:, :, :n_tiles + 1], 1, 2).reshape(-1)
    return slot.reshape(b, e, t), wsel.reshape(b, e, t), tile_lo


def _gather_body(slot_ref, h_ref, o_ref, onehot_ref):
    n_exp, cap, d = o_ref.shape
    t = h_ref.shape[0]
    rank = lax.broadcasted_iota(jnp.int32, (cap, t), 0)
    for e in range(n_exp):
        onehot_ref[e * cap:(e + 1) * cap, :] = jnp.where(
            slot_ref[e:e + 1, :] == rank, 1.0, 0.0).astype(BF16)
    xs = jnp.dot(onehot_ref[...], h_ref[...], preferred_element_type=F32)
    o_ref[...] = xs.reshape(n_exp, cap, d).astype(o_ref.dtype)


def _gather(slot, hx, *, experts_per_step=8):
    b, e, t = slot.shape
    d = hx.shape[2]
    cap = EC_CAPACITY * t // N_EXPERTS
    eg = experts_per_step
    return pl.pallas_call(
        _gather_body,
        grid=(b, e // eg),
        in_specs=[
            pl.BlockSpec((None, eg, t), lambda bi, gi: (bi, gi, 0)),
            pl.BlockSpec((None, t, d), lambda bi, gi: (bi, 0, 0)),
        ],
        out_specs=pl.BlockSpec((eg, cap, d), lambda bi, gi: (gi, bi, 0)),
        out_shape=jax.ShapeDtypeStruct((e, b * cap, d), BF16),
        scratch_shapes=[pltpu.VMEM((eg * cap, t), BF16)],
        compiler_params=_cparams(2, 48),
        name="moe_gather",
    )(slot, hx)


def _window_starts(lo_ref, base, n_exp, cap):
    starts = []
    fits = None
    for e in range(n_exp):
        lo = lo_ref[base + e]
        hi = lo_ref[base + n_exp + e]
        start = jnp.minimum(lo & -BF16_ROWS, cap - MOE_WINDOW)
        starts.append(pl.multiple_of(start, BF16_ROWS))
        ok = hi - start <= MOE_WINDOW
        fits = ok if fits is None else fits & ok
    return starts, fits


def _gather_win_body(lo_ref, slot_ref, h_ref, o_ref, onehot_ref):
    n_exp, cap, d = o_ref.shape
    tt = MOE_TILE
    tiles = h_ref.shape[0] // tt
    w = MOE_WINDOW
    i = pl.program_id(1)

    @pl.when(i == 0)
    def _():
        o_ref[...] = jnp.zeros_like(o_ref)

    for sub in range(tiles):
        tok = slice(sub * tt, (sub + 1) * tt)
        base = (pl.program_id(0) * (pl.num_programs(1) * tiles + 1) + i * tiles + sub) * n_exp
        starts, fits = _window_starts(lo_ref, base, n_exp, cap)

        @pl.when(fits)
        def _(tok=tok, starts=starts):
            rank = lax.broadcasted_iota(jnp.int32, (w, tt), 0)
            for e in range(n_exp):
                onehot_ref[e * w:(e + 1) * w, :] = jnp.where(
                    slot_ref[e:e + 1, tok] - starts[e] == rank, 1.0, 0.0).astype(BF16)
            rows = jnp.dot(onehot_ref[...], h_ref[tok, :], preferred_element_type=F32)
            for e in range(n_exp):
                o_ref[e, pl.ds(starts[e], w), :] += rows[e * w:(e + 1) * w].astype(o_ref.dtype)

        @pl.when(jnp.logical_not(fits))
        def _(tok=tok):
            rank = lax.broadcasted_iota(jnp.int32, (cap, tt), 0)
            for e in range(n_exp):
                onehot = jnp.where(slot_ref[e:e + 1, tok] == rank, 1.0, 0.0).astype(BF16)
                o_ref[e] += jnp.dot(onehot, h_ref[tok, :], preferred_element_type=F32).astype(o_ref.dtype)


def _gather_windowed(slot, tile_lo, hx):
    b, e, t = slot.shape
    d = hx.shape[2]
    cap = EC_CAPACITY * t // N_EXPERTS
    rows = MOE_TILE * MOE_TILES_PER_STEP
    return pl.pallas_call(
        _gather_win_body,
        grid_spec=pltpu.PrefetchScalarGridSpec(
            num_scalar_prefetch=1,
            grid=(b, t // rows),
            in_specs=[
                pl.BlockSpec((None, e, rows), lambda bi, ti, lo: (bi, 0, ti)),
                pl.BlockSpec((None, rows, d), lambda bi, ti, lo: (bi, ti, 0)),
            ],
            out_specs=pl.BlockSpec((e, cap, d), lambda bi, ti, lo: (0, bi, 0)),
            scratch_shapes=[pltpu.VMEM((e * MOE_WINDOW, MOE_TILE), BF16)],
        ),
        out_shape=jax.ShapeDtypeStruct((e, b * cap, d), BF16),
        compiler_params=_cparams(2, 48),
        name="moe_gather_windowed",
    )(tile_lo, slot, hx)


def _ffn_body(*refs, n_streams, row_chunk):
    xs_refs = refs[:n_streams]
    wg_ref, wu_ref, wd_ref = refs[n_streams:n_streams + 3]
    ys_refs = refs[n_streams + 3:2 * n_streams + 3]
    acc_refs = refs[2 * n_streams + 3:3 * n_streams + 3]
    wgb_ref, wub_ref, wdb_ref = refs[3 * n_streams + 3:]
    f = pl.program_id(1)
    wgb_ref[...] = wg_ref[...].astype(BF16)
    wub_ref[...] = wu_ref[...].astype(BF16)
    wdb_ref[...] = wd_ref[...].astype(BF16)
    last = pl.num_programs(1) - 1
    for xs_ref, ys_ref, acc_ref in zip(xs_refs, ys_refs, acc_refs):
        rc = min(row_chunk, xs_ref.shape[0])

        def rows(i, carry, mode, xs_ref=xs_ref, ys_ref=ys_ref, acc_ref=acc_ref, rc=rc):
            r0 = pl.multiple_of(i * rc, rc)
            xb = xs_ref[pl.ds(r0, rc), :]
            g = jnp.dot(xb, wgb_ref[...], preferred_element_type=F32)
            u = jnp.dot(xb, wub_ref[...], preferred_element_type=F32)
            hid = (_silu(g) * u).astype(BF16)
            part = jnp.dot(hid, wdb_ref[...], preferred_element_type=F32)
            if mode == "first":
                acc_ref[pl.ds(r0, rc), :] = part
            elif mode == "middle":
                acc_ref[pl.ds(r0, rc), :] += part
            else:
                ys_ref[pl.ds(r0, rc), :] = (acc_ref[pl.ds(r0, rc), :] + part).astype(ys_ref.dtype)
            return carry

        n_chunks = xs_ref.shape[0] // rc
        for mode, cond in (("first", f == 0), ("middle", (f > 0) & (f < last)), ("last", f == last)):
            @pl.when(cond)
            def _(mode=mode, rows=rows, n_chunks=n_chunks):
                lax.fori_loop(0, n_chunks, functools.partial(rows, mode=mode), 0)


def _expert_ffn(xs_list, w_gate, w_up, w_down, layer, *, tf=512, row_chunk=1024):
    _, e, d, ff = w_gate.shape
    assert ff % tf == 0 and ff // tf >= 2
    n = len(xs_list)
    xs_specs = [pl.BlockSpec((None, xs.shape[1], d), lambda ei, fi: (ei, 0, 0)) for xs in xs_list]
    return pl.pallas_call(
        functools.partial(_ffn_body, n_streams=n, row_chunk=row_chunk),
        grid=(e, ff // tf),
        in_specs=xs_specs + [
            pl.BlockSpec((None, None, d, tf), lambda ei, fi: (layer, ei, 0, fi)),
            pl.BlockSpec((None, None, d, tf), lambda ei, fi: (layer, ei, 0, fi)),
            pl.BlockSpec((None, None, tf, d), lambda ei, fi: (layer, ei, fi, 0)),
        ],
        out_specs=xs_specs,
        out_shape=[jax.ShapeDtypeStruct(xs.shape, BF16) for xs in xs_list],
        scratch_shapes=[pltpu.VMEM(xs.shape[1:], F32) for xs in xs_list]
        + [pltpu.VMEM((d, tf), BF16), pltpu.VMEM((d, tf), BF16), pltpu.VMEM((tf, d), BF16)],
        compiler_params=_cparams(2, 56),
        name="expert_ffn",
    )(*xs_list, w_gate, w_up, w_down)


def _combine_body(ys_ref, slot_ref, wsel_ref, x_ref, g2_ref, *rest, final_norm):
    if final_norm:
        fg_ref, o_ref = rest
    else:
        (o_ref,) = rest
    n_exp, cap, _ = ys_ref.shape
    tm = x_ref.shape[0]
    rank = lax.broadcasted_iota(jnp.int32, (tm, cap), 1)
    slots = slot_ref[...]
    wsel = wsel_ref[...]
    o_ref[...] = jnp.zeros_like(o_ref)
    for e in range(n_exp):
        onehot = jnp.where(slots[:, e:e + 1] == rank, 1.0, 0.0).astype(BF16)
        o_ref[...] += wsel[:, e:e + 1] * jnp.dot(onehot, ys_ref[e], preferred_element_type=F32)
    out = x_ref[...] + g2_ref[...] * o_ref[...]
    if final_norm:
        out = out * lax.rsqrt(jnp.mean(out * out, axis=-1, keepdims=True) + NORM_EPS) * fg_ref[...]
    o_ref[...] = out


def _combine(ys, slot_t, wsel_t, x, mod, layer, row_of_batch, final_g=None, *, tm=ROW_TILE):
    b, t, d = x.shape
    e = ys.shape[0]
    cap = ys.shape[1] // b
    tm = min(tm, t)
    in_specs = [
        pl.BlockSpec((e, cap, d), lambda bi, ti: (0, bi, 0)),
        pl.BlockSpec((None, tm, e), lambda bi, ti: (bi, ti, 0)),
        pl.BlockSpec((None, tm, e), lambda bi, ti: (bi, ti, 0)),
        pl.BlockSpec((None, tm, d), lambda bi, ti: (bi, ti, 0)),
        _mod_spec(layer, 5, d, row_of_batch, 2, 0),
    ]
    args = [ys, slot_t, wsel_t, x, mod]
    if final_g is not None:
        in_specs.append(pl.BlockSpec((1, d), lambda bi, ti: (0, 0)))
        args.append(final_g.reshape(1, d))
    return pl.pallas_call(
        functools.partial(_combine_body, final_norm=final_g is not None),
        grid=(b, t // tm),
        in_specs=in_specs,
        out_specs=pl.BlockSpec((None, tm, d), lambda bi, ti: (bi, ti, 0)),
        out_shape=jax.ShapeDtypeStruct((b, t, d), F32),
        compiler_params=_cparams(2, 48),
        name="moe_combine",
    )(*args)


def _combine_win_body(lo_ref, ys_ref, slot_ref, wsel_ref, x_ref, g2_ref, *rest, final_norm):
    if final_norm:
        fg_ref, o_ref, ywin_ref, scat_ref = rest
    else:
        o_ref, ywin_ref, scat_ref = rest
    n_exp, cap, d = ys_ref.shape
    tt = MOE_TILE
    tiles = x_ref.shape[0] // tt
    w = MOE_WINDOW
    i = pl.program_id(1)

    def finish(tok, acc):
        out = x_ref[tok, :] + g2_ref[...] * acc
        if final_norm:
            out = out * lax.rsqrt(jnp.mean(out * out, axis=-1, keepdims=True) + NORM_EPS) * fg_ref[...]
        o_ref[tok, :] = out

    for sub in range(tiles):
        tok = slice(sub * tt, (sub + 1) * tt)
        base = (pl.program_id(0) * (pl.num_programs(1) * tiles + 1) + i * tiles + sub) * n_exp
        starts, fits = _window_starts(lo_ref, base, n_exp, cap)

        @pl.when(fits)
        def _(tok=tok, starts=starts):
            rank = lax.broadcasted_iota(jnp.int32, (w, tt), 0)
            for e in range(n_exp):
                ywin_ref[e * w:(e + 1) * w, :] = ys_ref[e, pl.ds(starts[e], w), :]
                scat_ref[e * w:(e + 1) * w, :] = jnp.where(
                    slot_ref[e:e + 1, tok] - starts[e] == rank, wsel_ref[e:e + 1, tok], 0.0).astype(BF16)
            finish(tok, lax.dot_general(scat_ref[...], ywin_ref[...], TN_DIMS,
                                        preferred_element_type=F32))

        @pl.when(jnp.logical_not(fits))
        def _(tok=tok):
            rank = lax.broadcasted_iota(jnp.int32, (cap, tt), 0)
            acc = jnp.zeros((tt, d), F32)
            for e in range(n_exp):
                scat = jnp.where(slot_ref[e:e + 1, tok] == rank, wsel_ref[e:e + 1, tok], 0.0).astype(BF16)
                acc = acc + lax.dot_general(scat, ys_ref[e], TN_DIMS, preferred_element_type=F32)
            finish(tok, acc)


def _combine_windowed(ys, slot, wsel, tile_lo, x, mod, layer, row_of_batch, final_g=None):
    b, t, d = x.shape
    e = ys.shape[0]
    cap = ys.shape[1] // b
    rows = MOE_TILE * MOE_TILES_PER_STEP
    tile = lambda bi, ti, lo: (bi, ti, 0)
    routing = pl.BlockSpec((None, e, rows), lambda bi, ti, lo: (bi, 0, ti))
    in_specs = [
        pl.BlockSpec((e, cap, d), lambda bi, ti, lo: (0, bi, 0)),
        routing,
        routing,
        pl.BlockSpec((None, rows, d), tile),
        _mod_spec(layer, 5, d, row_of_batch, 2, 0),
    ]
    args = [ys, slot, wsel, x, mod]
    if final_g is not None:
        in_specs.append(pl.BlockSpec((1, d), lambda bi, ti, lo: (0, 0)))
        args.append(final_g.reshape(1, d))
    return pl.pallas_call(
        functools.partial(_combine_win_body, final_norm=final_g is not None),
        grid_spec=pltpu.PrefetchScalarGridSpec(
            num_scalar_prefetch=1,
            grid=(b, t // rows),
            in_specs=in_specs,
            out_specs=pl.BlockSpec((None, rows, d), tile),
            scratch_shapes=[pltpu.VMEM((e * MOE_WINDOW, d), BF16),
                            pltpu.VMEM((e * MOE_WINDOW, MOE_TILE), BF16)],
        ),
        out_shape=jax.ShapeDtypeStruct((b, t, d), F32),
        compiler_params=_cparams(2, 48),
        name="moe_combine_windowed",
    )(tile_lo, *args)


def _attn_body(sink_ref, q_ref, k_ref, v_ref, kc_ref, vc_ref, o_ref):
    blk = SWA_BLOCK
    hd = SWA_HEAD_DIM
    win = 3 * blk
    t = k_ref.shape[0]
    t_ctx = kc_ref.shape[0]
    n_heads = q_ref.shape[1] // hd
    group = n_heads // SWA_KV_HEADS
    blocks_per_step = q_ref.shape[0] // blk
    rel = (lax.broadcasted_iota(jnp.int32, (blk, win), 0)
           - lax.broadcasted_iota(jnp.int32, (blk, win), 1))
    head_idx = lax.broadcasted_iota(jnp.int32, (group, 1, 1), 0)

    def query_block(sub, carry):
        n = pl.program_id(1) * blocks_per_step + sub
        q0 = pl.multiple_of(sub * blk, blk)
        w0 = pl.multiple_of(jnp.clip((n - 1) * blk, 0, t - win), blk)
        dist = rel + (n * blk - w0)
        bias = jnp.where(jnp.abs(dist) <= SWA_WINDOW, 0.0, -jnp.inf)[None]
        for kv in range(SWA_KV_HEADS):
            heads = range(kv * group, (kv + 1) * group)
            cols = slice(kv * hd, (kv + 1) * hd)
            qg = jnp.concatenate([q_ref[pl.ds(q0, blk), h * hd:(h + 1) * hd] for h in heads], axis=0)
            s_ctx = lax.dot_general(qg, kc_ref[:, cols], NT_DIMS, preferred_element_type=F32)
            s_win = lax.dot_general(qg, k_ref[pl.ds(w0, win), cols], NT_DIMS, preferred_element_type=F32)
            s = jnp.concatenate([s_ctx.reshape(group, blk, t_ctx),
                                 s_win.reshape(group, blk, win) + bias], axis=-1)
            sink = jnp.zeros((group, 1, 1), F32)
            for i, h in enumerate(heads):
                sink = jnp.where(head_idx == i, sink_ref[h], sink)
            sink = sink * LOG2E
            m = jnp.maximum(jnp.max(s, axis=-1, keepdims=True), sink)
            p = jnp.exp2(s - m)
            den = jnp.sum(p, axis=-1, keepdims=True) + jnp.exp2(sink - m)
            pb = p.reshape(group * blk, t_ctx + win).astype(BF16)
            o = (jnp.dot(pb[:, :t_ctx], vc_ref[:, cols], preferred_element_type=F32)
                 + jnp.dot(pb[:, t_ctx:], v_ref[pl.ds(w0, win), cols], preferred_element_type=F32))
            o = o.reshape(group, blk, hd) / den
            for i, h in enumerate(heads):
                o_ref[pl.ds(q0, blk), h * hd:(h + 1) * hd] = o[i].astype(o_ref.dtype)
        return carry

    lax.fori_loop(0, blocks_per_step, query_block, 0, unroll=2)


def _window_attention(p, pc, sink, *, blocks_per_step=4):
    b, t, cols = p.shape
    t_ctx = pc.shape[1]
    kvw = SWA_KV_HEADS * SWA_HEAD_DIM
    qw = cols - 2 * kvw
    assert t >= 3 * SWA_BLOCK
    rows = blocks_per_step * SWA_BLOCK
    return pl.pallas_call(
        _attn_body,
        grid=(b, t // rows),
        in_specs=[
            pl.BlockSpec(memory_space=pltpu.SMEM),
            pl.BlockSpec((None, rows, qw), lambda bi, ni: (bi, ni, 0)),
            pl.BlockSpec((None, t, kvw), lambda bi, ni: (bi, 0, qw // kvw)),
            pl.BlockSpec((None, t, kvw), lambda bi, ni: (bi, 0, qw // kvw + 1)),
            pl.BlockSpec((None, t_ctx, kvw), lambda bi, ni: (bi, 0, 0)),
            pl.BlockSpec((None, t_ctx, kvw), lambda bi, ni: (bi, 0, 1)),
        ],
        out_specs=pl.BlockSpec((None, rows, qw), lambda bi, ni: (bi, ni, 0)),
        out_shape=jax.ShapeDtypeStruct((b, t, qw), BF16),
        compiler_params=_cparams(2, 32),
        name="window_attention",
    )(sink.astype(F32), p, p, p, pc, pc)


def _rope_tables(t):
    n_freq = SWA_HEAD_DIM // 4
    pos = jnp.arange(t)
    row = (pos // GRID_W).astype(F32)
    col = (pos % GRID_W).astype(F32)
    inv_freq = ROPE_BASE ** (-jnp.arange(n_freq, dtype=F32) / n_freq)
    ang_r, ang_c = row[:, None] * inv_freq, col[:, None] * inv_freq
    zero = jnp.zeros_like(ang_r)
    cos = jnp.concatenate([jnp.cos(ang_r)] * 2 + [jnp.cos(ang_c)] * 2, axis=1)
    sin_minus = jnp.concatenate([-jnp.sin(ang_r), zero, -jnp.sin(ang_c), zero], axis=1)
    sin_plus = jnp.concatenate([zero, jnp.sin(ang_r), zero, jnp.sin(ang_c)], axis=1)
    reps = LANES // SWA_HEAD_DIM
    return tuple(jnp.tile(tab, (1, reps)) for tab in (cos, sin_minus, sin_plus))


def _use_windows(t):
    cap = EC_CAPACITY * t // N_EXPERTS
    assert 2 * MOE_WINDOW == LANES
    return (t % (MOE_TILE * MOE_TILES_PER_STEP) == 0 and cap >= MOE_WINDOW
            and (cap - MOE_WINDOW) % BF16_ROWS == 0)


def _moe(streams, w_gate, w_up, w_down, layer):
    routed = []
    for hx, aff in streams:
        slot, wsel, tile_lo = _topk_select(aff)
        windowed = _use_windows(hx.shape[1])
        xs = _gather_windowed(slot, tile_lo, hx) if windowed else _gather(slot, hx)
        routed.append((xs, slot, wsel, tile_lo, windowed))
    ys_list = _expert_ffn([r[0] for r in routed], w_gate, w_up, w_down, layer)

    def make_combine(ys, slot, wsel, tile_lo, windowed):
        def combine(x, mod, row_of_batch, final_g=None):
            if windowed:
                return _combine_windowed(ys, slot, wsel, tile_lo, x, mod, layer, row_of_batch, final_g)
            return _combine(ys, jnp.swapaxes(slot, 1, 2), jnp.swapaxes(wsel, 1, 2), x, mod, layer,
                            row_of_batch, final_g)
        return combine

    return [make_combine(ys, *r[1:]) for ys, r in zip(ys_list, routed)]


def kernel(x, c, ctx, c_ctx, ada_w, ada_b, norm1_g, norm2_g, final_g, ret_w_in, ret_decay_f, ret_decay_b, ret_gn_f, ret_gn_b, ret_w_out, swa_w_qkv, swa_sink, swa_w_out, moe_router, moe_w_gate, moe_w_up, moe_w_down):
    b, t, d = x.shape
    lat_row = lambda bi: bi
    ctx_row = lambda bi: b
    cond = jnp.zeros((COND_ROWS, d), F32).at[:b].set(c).at[b].set(c_ctx)
    mod = _adaln(cond, ada_w, ada_b)

    ret_tn = 2048
    gate_col0 = (ret_w_in.shape[2] // 8) * 4
    assert gate_col0 % ret_tn == 0
    gate_gain = jnp.concatenate([jnp.ones((gate_col0,), F32), ret_gn_f[0].astype(F32),
                                 ret_gn_b[0].astype(F32)]).reshape(1, -1)
    ret_proj = dict(tm=ROW_TILE, tn=ret_tn, silu_from_tile=gate_col0 // ret_tn, silu_gain=gate_gain)
    p = _norm_mod_matmul(x, norm1_g[0], mod, 0, lat_row, ret_w_in[0], **ret_proj)
    t_ctx = ctx.shape[1]
    flat = lambda a: a.reshape(1, b * t_ctx, a.shape[-1])
    unflat = lambda a: a.reshape(b, t_ctx, a.shape[-1])
    pc = unflat(_norm_mod_matmul(flat(ctx), norm1_g[0], mod, 0, ctx_row, ret_w_in[0], **ret_proj))
    y, yc = _retention(p, pc, ret_decay_f[0], ret_decay_b[0])
    x1, hx, aff = _outproj_router(y, ret_w_out[0], x, mod, 0, lat_row, norm2_g[0], moe_router[0], tm=ROW_TILE)
    c1, hc, affc = _outproj_router(flat(yc), ret_w_out[0], flat(ctx), mod, 0, ctx_row, norm2_g[0],
                                   moe_router[0], tm=ROW_TILE)
    c1, hc = unflat(c1), unflat(hc)
    affc = jnp.swapaxes(affc.reshape(affc.shape[1], b, t_ctx), 0, 1)
    combine_lat, combine_ctx = _moe([(hx, aff), (hc, affc)], moe_w_gate, moe_w_up, moe_w_down, 0)
    x2 = combine_lat(x1, mod, lat_row)
    c2 = combine_ctx(c1, mod, ctx_row)

    qw = SWA_HEAD_DIM * (swa_w_qkv.shape[2] // SWA_HEAD_DIM - 2 * SWA_KV_HEADS)
    cos, sin_minus, sin_plus = _rope_tables(t)
    rope_groups = (qw + SWA_KV_HEADS * SWA_HEAD_DIM) // LANES
    p = _norm_mod_matmul(x2, norm1_g[1], mod, 1, lat_row, swa_w_qkv[0], tm=ROW_TILE,
                         tn=swa_w_qkv.shape[2],
                         rope=(cos, sin_minus, sin_plus, rope_groups,
                               (qw // LANES, SWA_HEAD_DIM ** -0.5 * LOG2E)))
    w_kv = swa_w_qkv[0][:, qw:]
    pc = unflat(_norm_mod_matmul(flat(c2), norm1_g[1], mod, 1, ctx_row, w_kv, tm=ROW_TILE, tn=w_kv.shape[1]))
    o = _window_attention(p, pc, swa_sink[0])
    x3, hx, aff = _outproj_router(o, swa_w_out[0], x2, mod, 1, lat_row, norm2_g[1], moe_router[1], tm=ROW_TILE)
    (combine_lat,) = _moe([(hx, aff)], moe_w_gate, moe_w_up, moe_w_down, 1)
    return combine_lat(x3, mod, lat_row, final_g)
```
